```python
import math
import jax, jax.numpy as jnp
from jax import lax
import numpy as np

D_MODEL = 2048
BATCH = 16
SEQ = 2048
DEPTH = 4
DEC_BATCH = 8
DEC_SEQ = 32
PAST_LEN = 2048

CHUNK = 64
N_MIXERS = 2
HEAD_DIM = 128
N_HEADS_A = 16
N_KV_HEADS_A = 4
N_IDX_HEADS = 16
IDX_DIM = 64
TOPK_MAX = 256
N_HEADS_B = 16
DIFF_DIM = 64
D_FF = 5632
PLE_DIM = 256
ROPE_FRACTION = 4
ROPE_THETA = 500000.0
Q_BLOCK = 128
DSA_Q_BLOCK = 64
LN_EPS = 1e-5
DEEPNORM_ALPHA = (2 * DEPTH) ** 0.25
DEEPNORM_BETA = (8 * DEPTH) ** -0.25
N_A_LAYERS = (DEPTH + 1) // 2
N_B_LAYERS = DEPTH // 2
A_Q = N_HEADS_A * HEAD_DIM
A_KV = N_KV_HEADS_A * HEAD_DIM
A_QI = N_IDX_HEADS * IDX_DIM
A_SPLITS = [A_Q, A_Q + A_KV, A_Q + 2 * A_KV, A_Q + 2 * A_KV + A_QI, A_Q + 2 * A_KV + A_QI + IDX_DIM]
A_IN = A_Q + 2 * A_KV + A_QI + IDX_DIM + N_IDX_HEADS
B_QK = N_HEADS_B * 2 * DIFF_DIM
B_V = N_HEADS_B * 2 * DIFF_DIM
B_IN = 2 * B_QK + B_V

kernel_name = "hybrid_dsa_diffattn_streaming_encoder_step"


def chunk_mask(qpos, kpos):
    return (kpos // CHUNK)[None, :] <= (qpos // CHUNK)[:, None]


def layer_norm(x, g, b):
    xf = x.astype(jnp.float32)
    mu = jnp.mean(xf, axis=-1, keepdims=True)
    xc = xf - mu
    var = jnp.mean(xc * xc, axis=-1, keepdims=True)
    return (xc * lax.rsqrt(var + LN_EPS) * g + b).astype(x.dtype)


def post_norm(h, sub, g, b):
    return layer_norm(DEEPNORM_ALPHA * h + sub, g, b)


def swiglu(x, wg, wu, wd):
    return (jax.nn.silu(x @ wg) * (x @ wu)) @ wd


def partial_rope(x, pos):
    rd = x.shape[-1] // ROPE_FRACTION
    half = rd // 2
    inv_freq = ROPE_THETA ** (-jnp.arange(half, dtype=jnp.float32) / half)
    ang = pos.astype(jnp.float32)[:, None] * inv_freq[None, :]
    cos = jnp.cos(ang)[None, :, None, :].astype(x.dtype)
    sin = jnp.sin(ang)[None, :, None, :].astype(x.dtype)
    x1, x2, xp = x[..., :half], x[..., half:rd], x[..., rd:]
    return jnp.concatenate([x1 * cos - x2 * sin, x1 * sin + x2 * cos, xp], axis=-1)


def over_query_blocks(fn, block, qpos, *qs):
    nb = qpos.shape[0] // block
    qs_b = tuple(jnp.moveaxis(q.reshape(q.shape[0], nb, block, *q.shape[2:]), 1, 0) for q in qs)
    out = lax.map(lambda a: fn(a[0], *a[1]), (qpos.reshape(nb, block), qs_b))
    out = jnp.moveaxis(out, 0, 1)
    return out.reshape(out.shape[0], nb * block, *out.shape[3:])


def dsa_attend(qpos, q, qi, wi, kpos, k, v, ki, topk):
    B, Q = q.shape[0], q.shape[1]
    adm = chunk_mask(qpos, kpos)
    rel = jax.nn.relu(jnp.einsum('bqhd,bld->bqhl', qi, ki).astype(jnp.float32))
    score = jnp.einsum('bqhl,bqh->bql', rel, wi.astype(jnp.float32))
    score = jnp.where(adm[None], score, -jnp.inf)
    _, idx = lax.top_k(score, topk)
    ok = (kpos[idx] // CHUNK) <= (qpos // CHUNK)[None, :, None]
    kg = jax.vmap(lambda kb, ib: kb[ib])(k, idx)
    vg = jax.vmap(lambda vb, ib: vb[ib])(v, idx)
    qg = q.reshape(B, Q, N_KV_HEADS_A, N_HEADS_A // N_KV_HEADS_A, HEAD_DIM)
    s = jnp.einsum('bqgrd,bqkgd->bqgrk', qg, kg).astype(jnp.float32) * (HEAD_DIM ** -0.5)
    s = jnp.where(ok[:, :, None, None, :], s, -jnp.inf)
    p = jax.nn.softmax(s, axis=-1).astype(v.dtype)
    o = jnp.einsum('bqgrk,bqkgd->bqgrd', p, vg)
    return o.reshape(B, Q, A_Q)


def dsa_mixer(x, pos, w_in, w_out, past):
    B, L, _ = x.shape
    q, k, v, qi, ki, wi = jnp.split(x @ w_in, A_SPLITS, axis=-1)
    q = partial_rope(q.reshape(B, L, N_HEADS_A, HEAD_DIM), pos)
    k = partial_rope(k.reshape(B, L, N_KV_HEADS_A, HEAD_DIM), pos)
    v = v.reshape(B, L, N_KV_HEADS_A, HEAD_DIM)
    qi = partial_rope(qi.reshape(B, L, N_IDX_HEADS, IDX_DIM), pos)
    ki = partial_rope(ki[:, :, None, :], pos)[:, :, 0]
    wi = wi * ((N_IDX_HEADS * IDX_DIM) ** -0.5)
    if past is None:
        kpos, k_all, v_all, ki_all = pos, k, v, ki
    else:
        pk, pv, pki = past
        kpos = jnp.concatenate([jnp.arange(pk.shape[1], dtype=jnp.int32), pos])
        k_all = jnp.concatenate([pk, k], axis=1)
        v_all = jnp.concatenate([pv, v], axis=1)
        ki_all = jnp.concatenate([pki, ki], axis=1)
    topk = min(TOPK_MAX, k_all.shape[1] // 4)
    attend = lambda qp, qb, qib, wib: dsa_attend(qp, qb, qib, wib, kpos, k_all, v_all, ki_all, topk)
    if past is None:
        o = over_query_blocks(attend, DSA_Q_BLOCK, pos, q, qi, wi)
    else:
        o = attend(pos, q, qi, wi)
    return o @ w_out, (k, v, ki)


def diff_attend(qpos, q, kpos, k, v, lam, g, lam_init):
    B, Q = q.shape[0], q.shape[1]
    s = jnp.einsum('bqhcd,blhcd->bhcql', q, k).astype(jnp.float32) * (DIFF_DIM ** -0.5)
    s = jnp.where(chunk_mask(qpos, kpos)[None, None, None], s, -jnp.inf)
    p = jax.nn.softmax(s, axis=-1)
    a = (p[:, :, 0] - lam * p[:, :, 1]).astype(v.dtype)
    o = jnp.einsum('bhql,blhe->bqhe', a, v).astype(jnp.float32)
    o = o * lax.rsqrt(jnp.mean(o * o, axis=-1, keepdims=True) + LN_EPS) * g * (1.0 - lam_init)
    return o.reshape(B, Q, B_V).astype(v.dtype)


def diff_mixer(x, pos, w_in, w_out, lq1, lk1, lq2, lk2, g, lam_init, past):
    B, L, _ = x.shape
    q, k, v = jnp.split(x @ w_in, [B_QK, 2 * B_QK], axis=-1)
    q = partial_rope(q.reshape(B, L, 2 * N_HEADS_B, DIFF_DIM), pos).reshape(B, L, N_HEADS_B, 2, DIFF_DIM)
    k = partial_rope(k.reshape(B, L, 2 * N_HEADS_B, DIFF_DIM), pos).reshape(B, L, N_HEADS_B, 2, DIFF_DIM)
    v = v.reshape(B, L, N_HEADS_B, 2 * DIFF_DIM)
    lam = (jnp.exp(jnp.sum(lq1.astype(jnp.float32) * lk1.astype(jnp.float32)))
           - jnp.exp(jnp.sum(lq2.astype(jnp.float32) * lk2.astype(jnp.float32))) + lam_init)
    if past is None:
        kpos, k_all, v_all = pos, k, v
    else:
        pk, pv = past
        kpos = jnp.concatenate([jnp.arange(pk.shape[1], dtype=jnp.int32), pos])
        k_all = jnp.concatenate([pk, k], axis=1)
        v_all = jnp.concatenate([pv, v], axis=1)
    attend = lambda qp, qb: diff_attend(qp, qb, kpos, k_all, v_all, lam, g, lam_init)
    if past is None:
        o = over_query_blocks(attend, Q_BLOCK, pos, q)
    else:
        o = attend(pos, q)
    return o @ w_out, (k, v)


def run_trunk(x, p, pos, pasts, ln_g, ln_b, ffn_w_gate, ffn_w_up, ffn_w_down, ple_w_gate, ple_b_gate,
              ple_w_proj, a_w_in, a_w_out, b_w_in, b_w_out, b_lambda_q1, b_lambda_k1, b_lambda_q2,
              b_lambda_k2, b_subln):
    rows = []
    for i in range(DEPTH):
        x = post_norm(x, 0.5 * swiglu(x, ffn_w_gate[i, 0], ffn_w_up[i, 0], ffn_w_down[i, 0]), ln_g[i, 0], ln_b[i, 0])
        j = i // N_MIXERS
        if i % N_MIXERS == 0:
            mix, new = dsa_mixer(x, pos, a_w_in[j], a_w_out[j], pasts[i])
        else:
            lam_init = 0.8 - 0.6 * math.exp(-0.3 * i)
            mix, new = diff_mixer(x, pos, b_w_in[j], b_w_out[j], b_lambda_q1[j], b_lambda_k1[j],
                                  b_lambda_q2[j], b_lambda_k2[j], b_subln[j], lam_init, pasts[i])
        rows.append(new)
        x = post_norm(x, mix, ln_g[i, 1], ln_b[i, 1])
        x = post_norm(x, 0.5 * swiglu(x, ffn_w_gate[i, 1], ffn_w_up[i, 1], ffn_w_down[i, 1]), ln_g[i, 2], ln_b[i, 2])
        gate = jax.nn.sigmoid(x @ ple_w_gate[i] + ple_b_gate[i])
        x = post_norm(x, gate * (p[i] @ ple_w_proj[i]), ln_g[i, 3], ln_b[i, 3])
    return x, rows


def setup_inputs(seed: int = 0) -> dict:
    key = jax.random.key(seed)
    ks = iter(jax.random.split(key, 64))

    def nrm(shape, scale=1.0):
        return jax.random.normal(next(ks), shape, jnp.float32) * scale

    s_in = D_MODEL ** -0.5
    beta = DEEPNORM_BETA
    return {
        'x_prompt': nrm((BATCH, SEQ, D_MODEL)),
        'x_sample': nrm((DEC_BATCH, DEC_SEQ, D_MODEL)),
        'cache_l0_k': nrm((DEC_BATCH, PAST_LEN, N_KV_HEADS_A, HEAD_DIM)),
        'cache_l0_v': nrm((DEC_BATCH, PAST_LEN, N_KV_HEADS_A, HEAD_DIM), beta),
        'cache_l0_kidx': nrm((DEC_BATCH, PAST_LEN, IDX_DIM)),
        'cache_l1_k': nrm((DEC_BATCH, PAST_LEN, N_HEADS_B, 2, DIFF_DIM)),
        'cache_l1_v': nrm((DEC_BATCH, PAST_LEN, N_HEADS_B, 2 * DIFF_DIM), beta),
        'cache_l2_k': nrm((DEC_BATCH, PAST_LEN, N_KV_HEADS_A, HEAD_DIM)),
        'cache_l2_v': nrm((DEC_BATCH, PAST_LEN, N_KV_HEADS_A, HEAD_DIM), beta),
        'cache_l2_kidx': nrm((DEC_BATCH, PAST_LEN, IDX_DIM)),
        'cache_l3_k': nrm((DEC_BATCH, PAST_LEN, N_HEADS_B, 2, DIFF_DIM)),
        'cache_l3_v': nrm((DEC_BATCH, PAST_LEN, N_HEADS_B, 2 * DIFF_DIM), beta),
        'p_prompt': nrm((DEPTH, BATCH, SEQ, PLE_DIM)),
        'p_sample': nrm((DEPTH, DEC_BATCH, DEC_SEQ, PLE_DIM)),
        'ln_g': 1.0 + nrm((DEPTH, 4, D_MODEL), 0.01),
        'ln_b': nrm((DEPTH, 4, D_MODEL), 0.01),
        'ffn_w_gate': nrm((DEPTH, 2, D_MODEL, D_FF), s_in),
        'ffn_w_up': nrm((DEPTH, 2, D_MODEL, D_FF), s_in),
        'ffn_w_down': nrm((DEPTH, 2, D_FF, D_MODEL), D_FF ** -0.5 * beta),
        'ple_w_gate': nrm((DEPTH, D_MODEL, D_MODEL), s_in),
        'ple_b_gate': nrm((DEPTH, D_MODEL), 0.01),
        'ple_w_proj': nrm((DEPTH, PLE_DIM, D_MODEL), PLE_DIM ** -0.5 * beta),
        'a_w_in': jnp.concatenate([
            nrm((N_A_LAYERS, D_MODEL, A_Q), s_in),
            nrm((N_A_LAYERS, D_MODEL, A_KV), s_in),
            nrm((N_A_LAYERS, D_MODEL, A_KV), s_in * beta),
            nrm((N_A_LAYERS, D_MODEL, A_QI + IDX_DIM + N_IDX_HEADS), s_in)], axis=-1),
        'a_w_out': nrm((N_A_LAYERS, A_Q, D_MODEL), A_Q ** -0.5 * beta),
        'b_w_in': jnp.concatenate([
            nrm((N_B_LAYERS, D_MODEL, 2 * B_QK), s_in),
            nrm((N_B_LAYERS, D_MODEL, B_V), s_in * beta)], axis=-1),
        'b_w_out': nrm((N_B_LAYERS, B_V, D_MODEL), B_V ** -0.5 * beta),
        'b_lambda_q1': nrm((N_B_LAYERS, DIFF_DIM), 0.1),
        'b_lambda_k1': nrm((N_B_LAYERS, DIFF_DIM), 0.1),
        'b_lambda_q2': nrm((N_B_LAYERS, DIFF_DIM), 0.1),
        'b_lambda_k2': nrm((N_B_LAYERS, DIFF_DIM), 0.1),
        'b_subln': 1.0 + nrm((N_B_LAYERS, 2 * DIFF_DIM), 0.01),
    }


def reference(x_prompt, x_sample, cache_l0_k, cache_l0_v, cache_l0_kidx, cache_l1_k, cache_l1_v,
              cache_l2_k, cache_l2_v, cache_l2_kidx, cache_l3_k, cache_l3_v, p_prompt, p_sample,
              ln_g, ln_b, ffn_w_gate, ffn_w_up, ffn_w_down, ple_w_gate, ple_b_gate, ple_w_proj,
              a_w_in, a_w_out, b_w_in, b_w_out, b_lambda_q1, b_lambda_k1, b_lambda_q2, b_lambda_k2,
              b_subln):
    weights = (ln_g, ln_b, ffn_w_gate, ffn_w_up, ffn_w_down, ple_w_gate, ple_b_gate, ple_w_proj,
               a_w_in, a_w_out, b_w_in, b_w_out, b_lambda_q1, b_lambda_k1, b_lambda_q2, b_lambda_k2,
               b_subln)
    pos_p = jnp.arange(x_prompt.shape[1], dtype=jnp.int32)
    pos_s = cache_l0_k.shape[1] + jnp.arange(x_sample.shape[1], dtype=jnp.int32)
    pasts_s = [(cache_l0_k, cache_l0_v, cache_l0_kidx), (cache_l1_k, cache_l1_v),
               (cache_l2_k, cache_l2_v, cache_l2_kidx), (cache_l3_k, cache_l3_v)]
    y_prompt, rows_p = run_trunk(x_prompt, p_prompt, pos_p, [None] * DEPTH, *weights)
    y_sample, rows_s = run_trunk(x_sample, p_sample, pos_s, pasts_s, *weights)
    (l0_k_p, l0_v_p, l0_kidx_p), (l1_k_p, l1_v_p), (l2_k_p, l2_v_p, l2_kidx_p), (l3_k_p, l3_v_p) = rows_p
    (l0_k_s, l0_v_s, l0_kidx_s), (l1_k_s, l1_v_s), (l2_k_s, l2_v_s, l2_kidx_s), (l3_k_s, l3_v_s) = rows_s
    return (y_prompt, y_sample,
            l0_k_p, l0_v_p, l0_kidx_p, l0_k_s, l0_v_s, l0_kidx_s,
            l1_k_p, l1_v_p, l1_k_s, l1_v_s,
            l2_k_p, l2_v_p, l2_kidx_p, l2_k_s, l2_v_s, l2_kidx_s,
            l3_k_p, l3_v_p, l3_k_s, l3_v_s)
```

```python
import functools
import math

import numpy as np
import jax
import jax.numpy as jnp
from jax import lax
from jax.experimental import pallas as pl
from jax.experimental.pallas import tpu as pltpu

CHUNK = 64
CHUNK_SHIFT = 6
HEAD_DIM = 128
N_HEADS_A = 16
N_KV_HEADS_A = 4
N_IDX_HEADS = 16
IDX_DIM = 64
TOPK_MAX = 256
N_HEADS_B = 16
DIFF_DIM = 64
ROPE_FRACTION = 4
ROPE_THETA = 500000.0
LN_EPS = 1e-5

LANES = 128
KEY_BLOCK = 256
VMEM_LIMIT_BYTES = 56 * 1024 * 1024
NEG_BIG = -1e30
INT_MIN = -(2 ** 31)

F32 = jnp.float32
BF16 = jnp.bfloat16


def _cparams(sem):
    return pltpu.CompilerParams(dimension_semantics=sem, vmem_limit_bytes=VMEM_LIMIT_BYTES)


def _layer_norm(z, g, b):
    mu = jnp.mean(z, axis=-1, keepdims=True)
    zc = z - mu
    var = jnp.mean(zc * zc, axis=-1, keepdims=True)
    return zc * lax.rsqrt(var + LN_EPS) * g + b


def _dot(a, b):
    return jnp.dot(a, b, preferred_element_type=F32)


def _dot_nt(a, b):
    return lax.dot_general(a, b, (((1,), (1,)), ((), ())), preferred_element_type=F32)


def _ffn_body(x_ref, wg_ref, wu_ref, wd_ref, g_ref, b_ref, o_ref, xb_ref, *, alpha, nf):
    f = pl.program_id(1)

    @pl.when(f == 0)
    def _():
        xb_ref[...] = x_ref[...].astype(BF16)
        o_ref[...] = jnp.zeros_like(o_ref)

    xb = xb_ref[...]
    gate = _dot(xb, wg_ref[...])
    up = _dot(xb, wu_ref[...])
    h = (gate * jax.nn.sigmoid(gate) * up).astype(BF16)
    o_ref[...] += _dot(h, wd_ref[...])

    @pl.when(f == nf - 1)
    def _():
        z = alpha * x_ref[...] + 0.5 * o_ref[...]
        o_ref[...] = _layer_norm(z, g_ref[...], b_ref[...])


def _ffn(x, wg, wu, wd, ln_g, ln_b, layer, sub, ln_idx, *, alpha, tm, tf):
    n, d = x.shape
    f_dim = wg.shape[-1]
    nf = f_dim // tf
    return pl.pallas_call(
        functools.partial(_ffn_body, alpha=alpha, nf=nf),
        out_shape=jax.ShapeDtypeStruct((n, d), F32),
        grid=(n // tm, nf),
        in_specs=[
            pl.BlockSpec((tm, d), lambda i, f: (i, 0)),
            pl.BlockSpec((None, None, d, tf), lambda i, f: (layer, sub, 0, f)),
            pl.BlockSpec((None, None, d, tf), lambda i, f: (layer, sub, 0, f)),
            pl.BlockSpec((None, None, tf, d), lambda i, f: (layer, sub, f, 0)),
            pl.BlockSpec((None, None, 1, d), lambda i, f: (layer, ln_idx, 0, 0)),
            pl.BlockSpec((None, None, 1, d), lambda i, f: (layer, ln_idx, 0, 0)),
        ],
        out_specs=pl.BlockSpec((tm, d), lambda i, f: (i, 0)),
        scratch_shapes=[pltpu.VMEM((tm, d), BF16)],
        compiler_params=_cparams(("parallel", "arbitrary")),
        name="ffn_postnorm",
    )(x, wg, wu, wd, ln_g, ln_b)


def _ple_body(x_ref, p_ref, wg_ref, bg_ref, wp_ref, g_ref, b_ref, o_ref, *, alpha):
    x = x_ref[...]
    gate = jax.nn.sigmoid(_dot(x.astype(BF16), wg_ref[...]) + bg_ref[...])
    proj = _dot(p_ref[...].astype(BF16), wp_ref[...])
    o_ref[...] = _layer_norm(alpha * x + gate * proj, g_ref[...], b_ref[...])


def _ple(x, p, wg, bg, wp, ln_g, ln_b, layer, *, alpha, tm):
    n, d = x.shape
    pd = p.shape[-1]
    return pl.pallas_call(
        functools.partial(_ple_body, alpha=alpha),
        out_shape=jax.ShapeDtypeStruct((n, d), F32),
        grid=(n // tm,),
        in_specs=[
            pl.BlockSpec((tm, d), lambda i: (i, 0)),
            pl.BlockSpec((None, tm, pd), lambda i: (layer, i, 0)),
            pl.BlockSpec((None, d, d), lambda i: (layer, 0, 0)),
            pl.BlockSpec((None, 1, d), lambda i: (layer, 0, 0)),
            pl.BlockSpec((None, pd, d), lambda i: (layer, 0, 0)),
            pl.BlockSpec((None, None, 1, d), lambda i: (layer, 3, 0, 0)),
            pl.BlockSpec((None, None, 1, d), lambda i: (layer, 3, 0, 0)),
        ],
        out_specs=pl.BlockSpec((tm, d), lambda i: (i, 0)),
        compiler_params=_cparams(("parallel",)),
        name="ple_postnorm",
    )(x, p, wg, bg, wp, ln_g, ln_b)


def _oproj_body(x_ref, a_ref, w_ref, g_ref, b_ref, o_ref, *, alpha):
    mix = _dot(a_ref[...], w_ref[...])
    o_ref[...] = _layer_norm(alpha * x_ref[...] + mix, g_ref[...], b_ref[...])


def _oproj(x, a, w, ln_g, ln_b, layer, j, *, alpha, tm):
    n, d = x.shape
    ad = a.shape[-1]
    return pl.pallas_call(
        functools.partial(_oproj_body, alpha=alpha),
        out_shape=jax.ShapeDtypeStruct((n, d), F32),
        grid=(n // tm,),
        in_specs=[
            pl.BlockSpec((tm, d), lambda i: (i, 0)),
            pl.BlockSpec((tm, ad), lambda i: (i, 0)),
            pl.BlockSpec((None, ad, d), lambda i: (j, 0, 0)),
            pl.BlockSpec((None, None, 1, d), lambda i: (layer, 1, 0, 0)),
            pl.BlockSpec((None, None, 1, d), lambda i: (layer, 1, 0, 0)),
        ],
        out_specs=pl.BlockSpec((tm, d), lambda i: (i, 0)),
        compiler_params=_cparams(("parallel",)),
        name="oproj_postnorm",
    )(x, a, w, ln_g, ln_b)


def _rope_tables(positions, head_dim, kind):
    rd = head_dim // ROPE_FRACTION
    half = rd // 2
    inv_freq = (np.float32(ROPE_THETA) ** (-(np.arange(half, dtype=np.float32) / np.float32(half)))).astype(np.float32)
    ang = positions.astype(np.float32)[:, None] * inv_freq[None, :]
    cos, sin = np.cos(ang).astype(np.float32), np.sin(ang).astype(np.float32)
    npos = positions.shape[0]
    c = np.ones((npos, LANES), np.float32)
    sa = np.zeros((npos, LANES), np.float32)
    sb = np.zeros((npos, LANES), np.float32)
    starts = range(0, LANES, head_dim) if kind == "head" else [0]
    for h0 in starts:
        c[:, h0:h0 + half] = cos
        c[:, h0 + half:h0 + rd] = cos
        sa[:, h0:h0 + half] = -sin
        sb[:, h0 + half:h0 + rd] = sin
    if kind == "kiwi":
        c[:, IDX_DIM:IDX_DIM + N_IDX_HEADS] = (N_IDX_HEADS * IDX_DIM) ** -0.5
    return jnp.asarray(c), jnp.asarray(sa), jnp.asarray(sb), half


def _inproj_body(*refs, segs, n_tab, col_step):
    x_ref, w_ref = refs[0], refs[1]
    tab_refs = refs[2:2 + 3 * n_tab]
    out_refs = refs[2 + 3 * n_tab:]
    xb = x_ref[...].astype(BF16)
    oi = 0
    for (c0, c1, tab, half, scale, dtypes) in segs:
        outs = out_refs[oi:oi + len(dtypes)]
        oi += len(dtypes)
        for j0 in range(c0, c1, col_step):
            wj = min(col_step, c1 - j0)
            y = _dot(xb, w_ref[:, j0:j0 + wj])
            for c in range(wj // LANES):
                yc = y[:, c * LANES:(c + 1) * LANES]
                if tab is not None:
                    cc, sa, sb = (r[...] for r in tab_refs[3 * tab:3 * tab + 3])
                    yc = (yc * cc + pltpu.roll(yc, LANES - half, 1) * sa + pltpu.roll(yc, half, 1) * sb)
                if scale != 1.0:
                    yc = yc * scale
                lo = j0 - c0 + c * LANES
                for o_ref, dt in zip(outs, dtypes):
                    o_ref[:, lo:lo + LANES] = yc.astype(dt)


def _inproj(x, w, tables, segs, *, tm, period_blocks):
    n, d = x.shape
    ctot = w.shape[1]
    segs_full = tuple((c0, c1, tab, (tables[tab][3] if tab is not None else 0), scale, tuple(dts))
                      for (c0, c1, tab, scale, dts) in segs)
    in_specs = [pl.BlockSpec((tm, d), lambda i: (i, 0)), pl.BlockSpec((d, ctot), lambda i: (0, 0))]
    args = [x, w]
    for (c, sa, sb, _) in tables:
        for t in (c, sa, sb):
            in_specs.append(pl.BlockSpec((tm, LANES), lambda i: (i % period_blocks, 0)))
            args.append(t)
    out_shape, out_specs = [], []
    for (c0, c1, _, _, dts) in segs:
        for dt in dts:
            out_shape.append(jax.ShapeDtypeStruct((n, c1 - c0), dt))
            out_specs.append(pl.BlockSpec((tm, c1 - c0), lambda i: (i, 0)))
    return pl.pallas_call(
        functools.partial(_inproj_body, segs=segs_full, n_tab=len(tables), col_step=512),
        out_shape=out_shape,
        grid=(n // tm,),
        in_specs=in_specs,
        out_specs=out_specs,
        compiler_params=_cparams(("parallel",)),
        name="inproj_rope",
    )(*args)


def _sortable_key(score):
    score = jnp.where(score == 0.0, 0.0, score)
    bits = pltpu.bitcast(score, jnp.int32)
    return bits ^ (lax.shift_right_arithmetic(bits, 31) & 0x7FFFFFFF)


def _dsa_body(q_ref, qi_ref, kiwi_ref, k_ref, v_ref, kid_ref, tri_ref, o_ref, key_ref, bias_ref, *,
              tq, tk, nq, l_valid, q_pos0, topk):
    t = pl.program_id(1)
    q_first = q_pos0 + t * tq
    q_last_chunk = (q_first + tq - 1) // CHUNK
    kmax = jnp.minimum(l_valid, (q_last_chunk + 1) * CHUNK)
    nkb = (kmax + tk - 1) // tk

    lane128 = lax.broadcasted_iota(jnp.int32, (tq, LANES), 1)
    wi = kiwi_ref[:, IDX_DIM:IDX_DIM + N_IDX_HEADS]
    q_chunk = lax.shift_right_arithmetic(q_first + lax.broadcasted_iota(jnp.int32, (tq, tk), 0), CHUNK_SHIFT)
    k_iota = lax.broadcasted_iota(jnp.int32, (tq, tk), 1)

    def idx_block(kb, carry):
        k0 = pl.multiple_of(kb * tk, tk)
        kblk = kid_ref[pl.ds(k0, tk), :]
        score = jnp.zeros((tq, tk), F32)
        for c in range(N_IDX_HEADS // 2):
            qc = qi_ref[:, c * LANES:(c + 1) * LANES]
            lo = jnp.where(lane128 < IDX_DIM, qc, jnp.zeros_like(qc))
            hi = jnp.where(lane128 >= IDX_DIM, qc, jnp.zeros_like(qc))
            rel = jnp.maximum(_dot_nt(jnp.concatenate([lo, hi], axis=0), kblk), 0.0)
            score = score + wi[:, 2 * c:2 * c + 1] * rel[:tq] + wi[:, 2 * c + 1:2 * c + 2] * rel[tq:]
        kpos = k0 + k_iota
        adm = (lax.shift_right_arithmetic(kpos, CHUNK_SHIFT) <= q_chunk) & (kpos < l_valid)
        key_ref[kb] = jnp.where(adm, _sortable_key(score), INT_MIN)
        return carry

    lax.fori_loop(0, nkb, idx_block, 0)

    def bit_step(i, tu):
        cand_u = tu | lax.shift_left(jnp.int32(1), 31 - i)
        cand_s = cand_u ^ INT_MIN

        def cnt_block(kb, cnt):
            ge = jnp.where(key_ref[kb] >= cand_s, 1, 0)
            for c in range(tk // LANES):
                cnt = cnt + ge[:, c * LANES:(c + 1) * LANES]
            return cnt

        cnt = lax.fori_loop(0, nkb, cnt_block, jnp.zeros((tq, LANES), jnp.int32))
        total = jnp.sum(cnt, axis=1, keepdims=True)
        return jnp.where(total >= topk, cand_u, tu)

    thr = lax.fori_loop(0, 32, bit_step, jnp.zeros((tq, 1), jnp.int32)) ^ INT_MIN

    def gt_block(kb, cnt):
        gt = jnp.where(key_ref[kb] > thr, 1, 0)
        for c in range(tk // LANES):
            cnt = cnt + gt[:, c * LANES:(c + 1) * LANES]
        return cnt

    n_gt = jnp.sum(lax.fori_loop(0, nkb, gt_block, jnp.zeros((tq, LANES), jnp.int32)), axis=1, keepdims=True)
    need = (topk - n_gt).astype(F32)

    def sel_block(kb, seen):
        key = key_ref[kb]
        eq = key == thr
        eqf = jnp.where(eq, 1.0, 0.0)
        pref = _dot(eqf.astype(BF16), tri_ref[...]) + seen
        keep_tie = jnp.where(eq, jnp.where(pref <= need, 1, 0), 0)
        sel = jnp.where(key > thr, 1, keep_tie)
        sel = jnp.where(key == INT_MIN, 0, sel)
        bias_ref[kb] = jnp.where(sel == 1, 0.0, NEG_BIG)
        return pref[:, tk - 1:tk]

    lax.fori_loop(0, nkb, sel_block, jnp.zeros((tq, 1), F32))

    rep = N_HEADS_A // N_KV_HEADS_A
    for g in range(N_KV_HEADS_A):
        lhs = jnp.concatenate(
            [q_ref[:, (g * rep + r) * HEAD_DIM:(g * rep + r + 1) * HEAD_DIM] for r in range(rep)], axis=0)

        def kv_block(kb, carry, g=g, lhs=lhs):
            m, l, acc = carry
            k0 = pl.multiple_of(kb * tk, tk)
            kblk = k_ref[pl.ds(k0, tk), g * HEAD_DIM:(g + 1) * HEAD_DIM]
            vblk = v_ref[pl.ds(k0, tk), g * HEAD_DIM:(g + 1) * HEAD_DIM]
            bias = bias_ref[kb]
            s = _dot_nt(lhs, kblk) + jnp.concatenate([bias] * rep, axis=0)
            m_new = jnp.maximum(m, jnp.max(s, axis=1, keepdims=True))
            a = jnp.exp(m - m_new)
            p = jnp.exp(s - m_new)
            l = a * l + jnp.sum(p, axis=1, keepdims=True)
            acc = a * acc + _dot(p.astype(BF16), vblk)
            return m_new, l, acc

        init = (jnp.full((rep * tq, 1), NEG_BIG, F32), jnp.zeros((rep * tq, 1), F32),
                jnp.zeros((rep * tq, HEAD_DIM), F32))
        _, l, acc = lax.fori_loop(0, nkb, kv_block, init)
        o = acc / l
        for r in range(rep):
            h0 = (g * rep + r) * HEAD_DIM
            o_ref[:, h0:h0 + HEAD_DIM] = o[r * tq:(r + 1) * tq].astype(o_ref.dtype)


def _dsa_attention(q, qi, kiwi, k_all, v_all, kid_all, *, batch, lq, tq, tk, l_valid, q_pos0, topk):
    lk_pad = k_all.shape[1]
    nq = lq // tq
    nkb_max = lk_pad // tk
    tri = jnp.asarray(np.triu(np.ones((tk, tk), np.float32)), BF16)
    a_q = N_HEADS_A * HEAD_DIM
    a_kv = N_KV_HEADS_A * HEAD_DIM
    return pl.pallas_call(
        functools.partial(_dsa_body, tq=tq, tk=tk, nq=nq, l_valid=l_valid, q_pos0=q_pos0, topk=topk),
        out_shape=jax.ShapeDtypeStruct((batch * lq, a_q), BF16),
        grid=(batch, nq),
        in_specs=[
            pl.BlockSpec((tq, a_q), lambda b, t: (b * nq + t, 0)),
            pl.BlockSpec((tq, N_IDX_HEADS * IDX_DIM), lambda b, t: (b * nq + t, 0)),
            pl.BlockSpec((tq, LANES), lambda b, t: (b * nq + t, 0)),
            pl.BlockSpec((None, lk_pad, a_kv), lambda b, t: (b, 0, 0)),
            pl.BlockSpec((None, lk_pad, a_kv), lambda b, t: (b, 0, 0)),
            pl.BlockSpec((None, lk_pad, LANES), lambda b, t: (b, 0, 0)),
            pl.BlockSpec((tk, tk), lambda b, t: (0, 0)),
        ],
        out_specs=pl.BlockSpec((tq, a_q), lambda b, t: (b * nq + t, 0)),
        scratch_shapes=[pltpu.VMEM((nkb_max, tq, tk), jnp.int32), pltpu.VMEM((nkb_max, tq, tk), F32)],
        compiler_params=_cparams(("parallel", "arbitrary")),
        name="dsa_attention",
    )(q, qi, kiwi, k_all, v_all, kid_all, tri)


def _diff_body(q_ref, k_ref, v_ref, lq1_ref, lk1_ref, lq2_ref, lk2_ref, g_ref, o_ref, *,
               tq, tk, l_valid, q_pos0, lam_init):
    t = pl.program_id(2)
    q_first = q_pos0 + t * tq
    q_last_chunk = (q_first + tq - 1) // CHUNK
    kmax = jnp.minimum(l_valid, (q_last_chunk + 1) * CHUNK)
    nkb = (kmax + tk - 1) // tk

    lam = (jnp.exp(jnp.sum(lq1_ref[...] * lk1_ref[...], axis=1, keepdims=True))
           - jnp.exp(jnp.sum(lq2_ref[...] * lk2_ref[...], axis=1, keepdims=True)) + lam_init)

    lane = lax.broadcasted_iota(jnp.int32, (tq, LANES), 1)
    qc = q_ref[...]
    lhs = jnp.concatenate([jnp.where(lane < DIFF_DIM, qc, jnp.zeros_like(qc)),
                           jnp.where(lane >= DIFF_DIM, qc, jnp.zeros_like(qc))], axis=0)
    q_chunk = lax.shift_right_arithmetic(q_first + lax.broadcasted_iota(jnp.int32, (tq, tk), 0), CHUNK_SHIFT)
    k_iota = lax.broadcasted_iota(jnp.int32, (tq, tk), 1)

    def kv_block(kb, carry):
        m, l, acc = carry
        k0 = pl.multiple_of(kb * tk, tk)
        kblk = k_ref[pl.ds(k0, tk), :]
        vblk = v_ref[pl.ds(k0, tk), :]
        kpos = k0 + k_iota
        adm = (lax.shift_right_arithmetic(kpos, CHUNK_SHIFT) <= q_chunk) & (kpos < l_valid)
        bias = jnp.where(adm, 0.0, NEG_BIG)
        s = _dot_nt(lhs, kblk) + jnp.concatenate([bias, bias], axis=0)
        m_new = jnp.maximum(m, jnp.max(s, axis=1, keepdims=True))
        a = jnp.exp(m - m_new)
        p = jnp.exp(s - m_new)
        l = a * l + jnp.sum(p, axis=1, keepdims=True)
        acc = a * acc + _dot(p.astype(BF16), vblk)
        return m_new, l, acc

    init = (jnp.full((2 * tq, 1), NEG_BIG, F32), jnp.zeros((2 * tq, 1), F32),
            jnp.zeros((2 * tq, 2 * DIFF_DIM), F32))
    _, l, acc = lax.fori_loop(0, nkb, kv_block, init)
    o = acc / l
    o = o[:tq] - lam * o[tq:]
    o = o * lax.rsqrt(jnp.mean(o * o, axis=-1, keepdims=True) + LN_EPS) * g_ref[...] * (1.0 - lam_init)
    o_ref[...] = o.astype(o_ref.dtype)


def _diff_attention(q, k_all, v_all, lq1, lk1, lq2, lk2, g, j, *, batch, lq, tq, tk, l_valid, q_pos0, lam_init):
    lk_pad = k_all.shape[1]
    nq = lq // tq
    hd = 2 * DIFF_DIM
    lam_spec = pl.BlockSpec((None, 1, DIFF_DIM), lambda b, h, t: (j, 0, 0))
    return pl.pallas_call(
        functools.partial(_diff_body, tq=tq, tk=tk, l_valid=l_valid, q_pos0=q_pos0, lam_init=lam_init),
        out_shape=jax.ShapeDtypeStruct((batch * lq, N_HEADS_B * hd), BF16),
        grid=(batch, N_HEADS_B, nq),
        in_specs=[
            pl.BlockSpec((tq, hd), lambda b, h, t: (b * nq + t, h)),
            pl.BlockSpec((None, lk_pad, hd), lambda b, h, t: (b, 0, h)),
            pl.BlockSpec((None, lk_pad, hd), lambda b, h, t: (b, 0, h)),
            lam_spec, lam_spec, lam_spec, lam_spec,
            pl.BlockSpec((None, 1, hd), lambda b, h, t: (j, 0, 0)),
        ],
        out_specs=pl.BlockSpec((tq, hd), lambda b, h, t: (b * nq + t, h)),
        compiler_params=_cparams(("parallel", "parallel", "arbitrary")),
        name="diff_attention",
    )(q, k_all, v_all, lq1, lk1, lq2, lk2, g)


def _pad_keys(x, lk_pad):
    pad = lk_pad - x.shape[1]
    return x if pad == 0 else jnp.pad(x, ((0, 0), (0, pad), (0, 0)))


def _trunk(x, p, pos0, pasts, w, *, tm):
    batch, seq, d = x.shape
    n = batch * seq
    depth = w["ln_g"].shape[0]
    alpha = float((2 * depth) ** 0.25)
    a_q = N_HEADS_A * HEAD_DIM
    a_kv = N_KV_HEADS_A * HEAD_DIM
    a_qi = N_IDX_HEADS * IDX_DIM
    b_qk = N_HEADS_B * 2 * DIFF_DIM

    period = max(1, seq // tm)
    table_rows = max(seq, tm)
    positions = (pos0 + np.arange(seq)).astype(np.int64)
    positions = np.tile(positions, table_rows // seq)
    tab128 = _rope_tables(positions, HEAD_DIM, "head")
    tab64 = _rope_tables(positions, IDX_DIM, "head")
    tabkw = _rope_tables(positions, IDX_DIM, "kiwi")

    xf = x.reshape(n, d)
    pf = p.reshape(depth, n, p.shape[-1])
    tf = 512 if w["ffn_w_gate"].shape[-1] % 512 == 0 else 128
    rows = []
    for i in range(depth):
        j = i // 2
        xf = _ffn(xf, w["ffn_w_gate"], w["ffn_w_up"], w["ffn_w_down"], w["ln_g"], w["ln_b"], i, 0, 0,
                  alpha=alpha, tm=tm, tf=tf)
        past = pasts[i]
        lk_new = seq if past is None else past[0].shape[1] + seq
        if i % 2 == 0:
            wa = w["a_w_in"]
            (q,) = _inproj(xf, wa[j, :, :a_q], [tab128], [(0, a_q, 0, HEAD_DIM ** -0.5, (BF16,))],
                           tm=tm, period_blocks=period)
            k, kb, v, vb = _inproj(xf, wa[j, :, a_q:a_q + 2 * a_kv], [tab128],
                                   [(0, a_kv, 0, 1.0, (F32, BF16)), (a_kv, 2 * a_kv, None, 1.0, (F32, BF16))],
                                   tm=tm, period_blocks=period)
            qi, kiwi = _inproj(xf, wa[j, :, a_q + 2 * a_kv:], [tab64, tabkw],
                               [(0, a_qi, 0, 1.0, (BF16,)), (a_qi, a_qi + LANES, 1, 1.0, (F32,))],
                               tm=tm, period_blocks=period)
            ki = kiwi[:, :IDX_DIM]
            rows.append((k.reshape(batch, seq, N_KV_HEADS_A, HEAD_DIM), v.reshape(batch, seq, N_KV_HEADS_A, HEAD_DIM),
                         ki.reshape(batch, seq, IDX_DIM)))
            k_all, v_all, ki_all = kb.reshape(batch, seq, a_kv), vb.reshape(batch, seq, a_kv), ki.reshape(batch, seq, IDX_DIM)
            if past is not None:
                pk, pv, pki = past
                k_all = jnp.concatenate([pk.reshape(batch, -1, a_kv).astype(BF16), k_all], axis=1)
                v_all = jnp.concatenate([pv.reshape(batch, -1, a_kv).astype(BF16), v_all], axis=1)
                ki_all = jnp.concatenate([pki, ki_all], axis=1)
            kid_all = jnp.concatenate([ki_all, ki_all], axis=-1).astype(BF16)
            tq = min(128, seq)
            tk = KEY_BLOCK
            lk_pad = -(-lk_new // tk) * tk
            topk = min(TOPK_MAX, lk_new // 4)
            o = _dsa_attention(q, qi, kiwi, _pad_keys(k_all, lk_pad), _pad_keys(v_all, lk_pad), _pad_keys(kid_all, lk_pad),
                               batch=batch, lq=seq, tq=tq, tk=tk, l_valid=lk_new, q_pos0=pos0, topk=topk)
            xf = _oproj(xf, o, w["a_w_out"], w["ln_g"], w["ln_b"], i, j, alpha=alpha, tm=tm)
        else:
            wb = w["b_w_in"]
            lam_init = 0.8 - 0.6 * math.exp(-0.3 * i)
            (q,) = _inproj(xf, wb[j, :, :b_qk], [tab64], [(0, b_qk, 0, DIFF_DIM ** -0.5, (BF16,))],
                           tm=tm, period_blocks=period)
            k, kb = _inproj(xf, wb[j, :, b_qk:2 * b_qk], [tab64], [(0, b_qk, 0, 1.0, (F32, BF16))],
                            tm=tm, period_blocks=period)
            v, vb = _inproj(xf, wb[j, :, 2 * b_qk:], [], [(0, b_qk, None, 1.0, (F32, BF16))],
                            tm=tm, period_blocks=period)
            rows.append((k.reshape(batch, seq, N_HEADS_B, 2, DIFF_DIM), v.reshape(batch, seq, N_HEADS_B, 2 * DIFF_DIM)))
            k_all, v_all = kb.reshape(batch, seq, b_qk), vb.reshape(batch, seq, b_qk)
            if past is not None:
                pk, pv = past
                k_all = jnp.concatenate([pk.reshape(batch, -1, b_qk).astype(BF16), k_all], axis=1)
                v_all = jnp.concatenate([pv.reshape(batch, -1, b_qk).astype(BF16), v_all], axis=1)
            tq = min(256, seq)
            tk = KEY_BLOCK
            lk_pad = -(-lk_new // tk) * tk
            o = _diff_attention(q, _pad_keys(k_all, lk_pad), _pad_keys(v_all, lk_pad),
                                w["b_lambda_q1"], w["b_lambda_k1"], w["b_lambda_q2"], w["b_lambda_k2"], w["b_subln"], j,
                                batch=batch, lq=seq, tq=tq, tk=tk, l_valid=lk_new, q_pos0=pos0, lam_init=lam_init)
            xf = _oproj(xf, o, w["b_w_out"], w["ln_g"], w["ln_b"], i, j, alpha=alpha, tm=tm)
        xf = _ffn(xf, w["ffn_w_gate"], w["ffn_w_up"], w["ffn_w_down"], w["ln_g"], w["ln_b"], i, 1, 2,
                  alpha=alpha, tm=tm, tf=tf)
        xf = _ple(xf, pf, w["ple_w_gate"], w["ple_b_gate"], w["ple_w_proj"], w["ln_g"], w["ln_b"], i,
                  alpha=alpha, tm=tm)
    return xf.reshape(batch, seq, d), rows


def kernel(x_prompt, x_sample, cache_l0_k, cache_l0_v, cache_l0_kidx, cache_l1_k, cache_l1_v, cache_l2_k, cache_l2_v, cache_l2_kidx, cache_l3_k, cache_l3_v, p_prompt, p_sample, ln_g, ln_b, ffn_w_gate, ffn_w_up, ffn_w_down, ple_w_gate, ple_b_gate, ple_w_proj, a_w_in, a_w_out, b_w_in, b_w_out, b_lambda_q1, b_lambda_k1, b_lambda_q2, b_lambda_k2, b_subln):
    depth, _, d = ln_g.shape
    a_cols = a_w_in.shape[-1]
    a_pad = -(-a_cols // LANES) * LANES - a_cols
    w = {
        "ln_g": ln_g.reshape(depth, 4, 1, d),
        "ln_b": ln_b.reshape(depth, 4, 1, d),
        "ffn_w_gate": ffn_w_gate.astype(BF16),
        "ffn_w_up": ffn_w_up.astype(BF16),
        "ffn_w_down": ffn_w_down.astype(BF16),
        "ple_w_gate": ple_w_gate.astype(BF16),
        "ple_b_gate": ple_b_gate.reshape(depth, 1, d),
        "ple_w_proj": ple_w_proj.astype(BF16),
        "a_w_in": jnp.pad(a_w_in, ((0, 0), (0, 0), (0, a_pad))).astype(BF16),
        "a_w_out": a_w_out.astype(BF16),
        "b_w_in": b_w_in.astype(BF16),
        "b_w_out": b_w_out.astype(BF16),
        "b_lambda_q1": b_lambda_q1.reshape(-1, 1, DIFF_DIM),
        "b_lambda_k1": b_lambda_k1.reshape(-1, 1, DIFF_DIM),
        "b_lambda_q2": b_lambda_q2.reshape(-1, 1, DIFF_DIM),
        "b_lambda_k2": b_lambda_k2.reshape(-1, 1, DIFF_DIM),
        "b_subln": b_subln.reshape(-1, 1, 2 * DIFF_DIM),
    }
    past_len = cache_l0_k.shape[1]
    pasts_s = [(cache_l0_k, cache_l0_v, cache_l0_kidx), (cache_l1_k, cache_l1_v),
               (cache_l2_k, cache_l2_v, cache_l2_kidx), (cache_l3_k, cache_l3_v)]
    n_p = x_prompt.shape[0] * x_prompt.shape[1]
    n_s = x_sample.shape[0] * x_sample.shape[1]
    y_p, rows_p = _trunk(x_prompt, p_prompt, 0, [None] * depth, w, tm=min(512, n_p))
    y_s, rows_s = _trunk(x_sample, p_sample, past_len, pasts_s, w, tm=min(256, n_s))
    out = [y_p, y_s]
    for rp, rs in zip(rows_p, rows_s):
        out.extend(rp)
        out.extend(rs)
    return tuple(out)
```

```python
import functools
import math

import numpy as np
import jax
import jax.numpy as jnp
from jax import lax
from jax.experimental import pallas as pl
from jax.experimental.pallas import tpu as pltpu

CHUNK = 64
CHUNK_SHIFT = 6
HEAD_DIM = 128
N_HEADS_A = 16
N_KV_HEADS_A = 4
N_IDX_HEADS = 16
IDX_DIM = 64
TOPK_MAX = 256
N_HEADS_B = 16
DIFF_DIM = 64
ROPE_FRACTION = 4
ROPE_THETA = 500000.0
LN_EPS = 1e-5

LANES = 128
KEY_BLOCK = 256
ROW_CHUNK = 128
DIFF_HEADS_PER_STEP = 8
DSA_Q_TILE = 256
LOG2_E = 1.4426950408889634
VMEM_LIMIT_BYTES = 56 * 1024 * 1024
NEG_BIG = -1e30
INT_MIN = -(2 ** 31)

F32 = jnp.float32
BF16 = jnp.bfloat16


def _cparams(sem):
    return pltpu.CompilerParams(dimension_semantics=sem, vmem_limit_bytes=VMEM_LIMIT_BYTES)


def _layer_norm(z, g, b):
    mu = jnp.mean(z, axis=-1, keepdims=True)
    zc = z - mu
    var = jnp.mean(zc * zc, axis=-1, keepdims=True)
    return zc * lax.rsqrt(var + LN_EPS) * g + b


def _dot(a, b):
    return jnp.dot(a, b, preferred_element_type=F32)


def _dot_nt(a, b):
    return lax.dot_general(a, b, (((1,), (1,)), ((), ())), preferred_element_type=F32)


def _ffn_body(x_ref, wg_ref, wu_ref, wd_ref, g_ref, b_ref, o_ref, xb_ref, *, alpha, nf):
    f = pl.program_id(1)

    @pl.when(f == 0)
    def _():
        xb_ref[...] = x_ref[...].astype(BF16)
        o_ref[...] = jnp.zeros_like(o_ref)

    xb = xb_ref[...]
    gate = _dot(xb, wg_ref[...])
    up = _dot(xb, wu_ref[...])
    h = (gate * jax.nn.sigmoid(gate) * up).astype(BF16)
    o_ref[...] += _dot(h, wd_ref[...])

    @pl.when(f == nf - 1)
    def _():
        z = alpha * x_ref[...] + 0.5 * o_ref[...]
        o_ref[...] = _layer_norm(z, g_ref[...], b_ref[...])


def _ffn(x, wg, wu, wd, ln_g, ln_b, layer, sub, ln_idx, *, alpha, tm, tf):
    n, d = x.shape
    f_dim = wg.shape[-1]
    nf = f_dim // tf
    return pl.pallas_call(
        functools.partial(_ffn_body, alpha=alpha, nf=nf),
        out_shape=jax.ShapeDtypeStruct((n, d), F32),
        grid=(n // tm, nf),
        in_specs=[
            pl.BlockSpec((tm, d), lambda i, f: (i, 0)),
            pl.BlockSpec((None, None, d, tf), lambda i, f: (layer, sub, 0, f)),
            pl.BlockSpec((None, None, d, tf), lambda i, f: (layer, sub, 0, f)),
            pl.BlockSpec((None, None, tf, d), lambda i, f: (layer, sub, f, 0)),
            pl.BlockSpec((None, None, 1, d), lambda i, f: (layer, ln_idx, 0, 0)),
            pl.BlockSpec((None, None, 1, d), lambda i, f: (layer, ln_idx, 0, 0)),
        ],
        out_specs=pl.BlockSpec((tm, d), lambda i, f: (i, 0)),
        scratch_shapes=[pltpu.VMEM((tm, d), BF16)],
        compiler_params=_cparams(("parallel", "arbitrary")),
        name="ffn_postnorm",
    )(x, wg, wu, wd, ln_g, ln_b)


def _ple_body(x_ref, p_ref, wg_ref, bg_ref, wp_ref, g_ref, b_ref, o_ref, *, alpha):
    x = x_ref[...]
    gate = jax.nn.sigmoid(_dot(x.astype(BF16), wg_ref[...]) + bg_ref[...])
    proj = _dot(p_ref[...].astype(BF16), wp_ref[...])
    o_ref[...] = _layer_norm(alpha * x + gate * proj, g_ref[...], b_ref[...])


def _ple(x, p, wg, bg, wp, ln_g, ln_b, layer, *, alpha, tm):
    n, d = x.shape
    pd = p.shape[-1]
    return pl.pallas_call(
        functools.partial(_ple_body, alpha=alpha),
        out_shape=jax.ShapeDtypeStruct((n, d), F32),
        grid=(n // tm,),
        in_specs=[
            pl.BlockSpec((tm, d), lambda i: (i, 0)),
            pl.BlockSpec((None, tm, pd), lambda i: (layer, i, 0)),
            pl.BlockSpec((None, d, d), lambda i: (layer, 0, 0)),
            pl.BlockSpec((None, 1, d), lambda i: (layer, 0, 0)),
            pl.BlockSpec((None, pd, d), lambda i: (layer, 0, 0)),
            pl.BlockSpec((None, None, 1, d), lambda i: (layer, 3, 0, 0)),
            pl.BlockSpec((None, None, 1, d), lambda i: (layer, 3, 0, 0)),
        ],
        out_specs=pl.BlockSpec((tm, d), lambda i: (i, 0)),
        compiler_params=_cparams(("parallel",)),
        name="ple_postnorm",
    )(x, p, wg, bg, wp, ln_g, ln_b)


def _oproj_body(x_ref, a_ref, w_ref, g_ref, b_ref, o_ref, *, alpha):
    mix = _dot(a_ref[...], w_ref[...])
    o_ref[...] = _layer_norm(alpha * x_ref[...] + mix, g_ref[...], b_ref[...])


def _oproj(x, a, w, ln_g, ln_b, layer, j, *, alpha, tm):
    n, d = x.shape
    ad = a.shape[-1]
    return pl.pallas_call(
        functools.partial(_oproj_body, alpha=alpha),
        out_shape=jax.ShapeDtypeStruct((n, d), F32),
        grid=(n // tm,),
        in_specs=[
            pl.BlockSpec((tm, d), lambda i: (i, 0)),
            pl.BlockSpec((tm, ad), lambda i: (i, 0)),
            pl.BlockSpec((None, ad, d), lambda i: (j, 0, 0)),
            pl.BlockSpec((None, None, 1, d), lambda i: (layer, 1, 0, 0)),
            pl.BlockSpec((None, None, 1, d), lambda i: (layer, 1, 0, 0)),
        ],
        out_specs=pl.BlockSpec((tm, d), lambda i: (i, 0)),
        compiler_params=_cparams(("parallel",)),
        name="oproj_postnorm",
    )(x, a, w, ln_g, ln_b)


def _rope_tables(positions, head_dim, kind):
    rd = head_dim // ROPE_FRACTION
    half = rd // 2
    inv_freq = (np.float32(ROPE_THETA) ** (-(np.arange(half, dtype=np.float32) / np.float32(half)))).astype(np.float32)
    ang = positions.astype(np.float32)[:, None] * inv_freq[None, :]
    cos, sin = np.cos(ang).astype(np.float32), np.sin(ang).astype(np.float32)
    npos = positions.shape[0]
    c = np.ones((npos, LANES), np.float32)
    sa = np.zeros((npos, LANES), np.float32)
    sb = np.zeros((npos, LANES), np.float32)
    starts = range(0, LANES, head_dim) if kind == "head" else [0]
    for h0 in starts:
        c[:, h0:h0 + half] = cos
        c[:, h0 + half:h0 + rd] = cos
        sa[:, h0:h0 + half] = -sin
        sb[:, h0 + half:h0 + rd] = sin
    if kind == "kiwi":
        c[:, IDX_DIM:IDX_DIM + N_IDX_HEADS] = (N_IDX_HEADS * IDX_DIM) ** -0.5
    return jnp.asarray(c), jnp.asarray(sa), jnp.asarray(sb), half


def _inproj_body(*refs, segs, n_tab, col_step):
    x_ref, w_ref = refs[0], refs[1]
    tab_refs = refs[2:2 + 3 * n_tab]
    out_refs = refs[2 + 3 * n_tab:]
    xb = x_ref[...].astype(BF16)
    oi = 0
    for (c0, c1, tab, half, scale, dtypes) in segs:
        outs = out_refs[oi:oi + len(dtypes)]
        oi += len(dtypes)
        for j0 in range(c0, c1, col_step):
            wj = min(col_step, c1 - j0)
            y = _dot(xb, w_ref[:, j0:j0 + wj])
            for c in range(wj // LANES):
                yc = y[:, c * LANES:(c + 1) * LANES]
                if tab is not None:
                    cc, sa, sb = (r[...] for r in tab_refs[3 * tab:3 * tab + 3])
                    yc = (yc * cc + pltpu.roll(yc, LANES - half, 1) * sa + pltpu.roll(yc, half, 1) * sb)
                if scale != 1.0:
                    yc = yc * scale
                lo = j0 - c0 + c * LANES
                for o_ref, dt in zip(outs, dtypes):
                    o_ref[:, lo:lo + LANES] = yc.astype(dt)


def _inproj(x, w, tables, segs, *, tm, period_blocks):
    n, d = x.shape
    ctot = w.shape[1]
    segs_full = tuple((c0, c1, tab, (tables[tab][3] if tab is not None else 0), scale, tuple(dts))
                      for (c0, c1, tab, scale, dts) in segs)
    in_specs = [pl.BlockSpec((tm, d), lambda i: (i, 0)), pl.BlockSpec((d, ctot), lambda i: (0, 0))]
    args = [x, w]
    for (c, sa, sb, _) in tables:
        for t in (c, sa, sb):
            in_specs.append(pl.BlockSpec((tm, LANES), lambda i: (i % period_blocks, 0)))
            args.append(t)
    out_shape, out_specs = [], []
    for (c0, c1, _, _, dts) in segs:
        for dt in dts:
            out_shape.append(jax.ShapeDtypeStruct((n, c1 - c0), dt))
            out_specs.append(pl.BlockSpec((tm, c1 - c0), lambda i: (i, 0)))
    return pl.pallas_call(
        functools.partial(_inproj_body, segs=segs_full, n_tab=len(tables), col_step=512),
        out_shape=out_shape,
        grid=(n // tm,),
        in_specs=in_specs,
        out_specs=out_specs,
        compiler_params=_cparams(("parallel",)),
        name="inproj_rope",
    )(*args)


def _softmax_step(lhs_ref, m_ref, acc_ref, kblk, vext, bias_fn, *, rows, rc, tk):
    for r in range(rows // rc):
        r0, r1 = r * rc, (r + 1) * rc
        s = _dot_nt(lhs_ref[r0:r1, :], kblk)
        if bias_fn is not None:
            s = s + bias_fn(r)
        m_prev = m_ref[r0:r1, :]
        m_next = jnp.maximum(m_prev, jnp.max(s, axis=1, keepdims=True))
        a = jnp.exp2(m_prev - m_next)
        p = jnp.exp2(s - pltpu.repeat(m_next, tk // LANES, axis=1))
        acc_ref[r0:r1, :] = acc_ref[r0:r1, :] * pltpu.repeat(a, 2, axis=1) + _dot(p.astype(BF16), vext)
        m_ref[r0:r1, :] = m_next


def _sortable_key(score):
    score = jnp.where(score == 0.0, 0.0, score)
    bits = pltpu.bitcast(score, jnp.int32)
    return bits ^ (lax.shift_right_arithmetic(bits, 31) & 0x7FFFFFFF)


def _dsa_body(q_ref, qi_ref, kiwi_ref, k_ref, v_ref, kid_ref, tri_ref, o_ref, key_ref, bias_ref,
              lhs_ref, m_ref, acc_ref, qlhs_ref, wb_ref, *, tq, tk, nq, l_valid, q_pos0, topk):
    t = pl.program_id(1)
    q_first = q_pos0 + t * tq
    q_last_chunk = (q_first + tq - 1) // CHUNK
    kmax = jnp.minimum(l_valid, (q_last_chunk + 1) * CHUNK)
    nkb = (kmax + tk - 1) // tk

    rc = min(ROW_CHUNK, tq)
    nrc = tq // rc
    nrep = tk // LANES

    lane128 = lax.broadcasted_iota(jnp.int32, (rc, LANES), 1)
    for rh in range(nrc):
        for c in range(N_IDX_HEADS // 2):
            qc = qi_ref[rh * rc:(rh + 1) * rc, c * LANES:(c + 1) * LANES]
            qlhs_ref[rh, c, :rc, :] = jnp.where(lane128 < IDX_DIM, qc, jnp.zeros_like(qc))
            qlhs_ref[rh, c, rc:, :] = jnp.where(lane128 >= IDX_DIM, qc, jnp.zeros_like(qc))
        wi = kiwi_ref[rh * rc:(rh + 1) * rc, IDX_DIM:IDX_DIM + N_IDX_HEADS]
        for h in range(N_IDX_HEADS):
            wb_ref[rh, h] = jnp.broadcast_to(wi[:, h:h + 1], (rc, LANES))

    def idx_block(kb, carry):
        k0 = pl.multiple_of(kb * tk, tk)
        kblk = kid_ref[pl.ds(k0, tk), :]
        kpos = k0 + lax.broadcasted_iota(jnp.int32, (rc, tk), 1)
        for rh in range(nrc):
            score = jnp.zeros((rc, tk), F32)
            for c in range(N_IDX_HEADS // 2):
                rel = jnp.maximum(_dot_nt(qlhs_ref[rh, c], kblk), 0.0)
                score = (score + pltpu.repeat(wb_ref[rh, 2 * c], nrep, axis=1) * rel[:rc]
                         + pltpu.repeat(wb_ref[rh, 2 * c + 1], nrep, axis=1) * rel[rc:])
            q_chunk = lax.shift_right_arithmetic(
                q_first + rh * rc + lax.broadcasted_iota(jnp.int32, (rc, tk), 0), CHUNK_SHIFT)
            adm = (lax.shift_right_arithmetic(kpos, CHUNK_SHIFT) <= q_chunk) & (kpos < l_valid)
            key_ref[kb, rh * rc:(rh + 1) * rc, :] = jnp.where(adm, _sortable_key(score), INT_MIN)
        return carry

    lax.fori_loop(0, nkb, idx_block, 0)

    def find_threshold(rh):
        def count(pred):
            def cnt_block(kb, cnt):
                for c in range(nrep):
                    cnt = cnt + jnp.where(pred(key_ref[kb, rh * rc:(rh + 1) * rc, c * LANES:(c + 1) * LANES]), 1, 0)
                return cnt
            cnt = lax.fori_loop(0, nkb, cnt_block, jnp.zeros((rc, LANES), jnp.int32))
            return jnp.sum(cnt, axis=1, keepdims=True)

        def bit_step(i, tu):
            cand_u = tu | lax.shift_left(jnp.int32(1), 31 - i)
            cand_s = cand_u ^ INT_MIN
            return jnp.where(count(lambda key: key >= cand_s) >= topk, cand_u, tu)

        thr128 = lax.fori_loop(0, 32, bit_step, jnp.zeros((rc, LANES), jnp.int32)) ^ INT_MIN
        n_gt = count(lambda key: key > thr128)
        return thr128[:, :1], (topk - n_gt).astype(F32)

    found = [find_threshold(rh) for rh in range(nrc)]
    thr = found[0][0] if nrc == 1 else jnp.concatenate([f[0] for f in found], axis=0)
    need = found[0][1] if nrc == 1 else jnp.concatenate([f[1] for f in found], axis=0)

    def sel_block(kb, seen):
        key = key_ref[kb]
        eq = key == thr
        eqf = jnp.where(eq, 1.0, 0.0)
        pref = _dot(eqf.astype(BF16), tri_ref[...]) + seen
        keep_tie = jnp.where(eq, jnp.where(pref <= need, 1, 0), 0)
        sel = jnp.where(key > thr, 1, keep_tie)
        sel = jnp.where(key == INT_MIN, 0, sel)
        bias_ref[kb] = jnp.where(sel == 1, 0.0, NEG_BIG)
        return pref[:, tk - 1:tk]

    lax.fori_loop(0, nkb, sel_block, jnp.zeros((tq, 1), F32))

    rep = N_HEADS_A // N_KV_HEADS_A
    rc = min(ROW_CHUNK, tq)
    ones = jnp.ones((tk, LANES), BF16)
    for g in range(N_KV_HEADS_A):
        for r in range(rep):
            lhs_ref[g, r * tq:(r + 1) * tq, :] = q_ref[:, (g * rep + r) * HEAD_DIM:(g * rep + r + 1) * HEAD_DIM]
    m_ref[...] = jnp.full(m_ref.shape, NEG_BIG, F32)
    acc_ref[...] = jnp.zeros(acc_ref.shape, F32)

    def kv_block(kb, c):
        k0 = pl.multiple_of(kb * tk, tk)
        for g in range(N_KV_HEADS_A):
            kblk = k_ref[pl.ds(k0, tk), g * HEAD_DIM:(g + 1) * HEAD_DIM]
            vext = jnp.concatenate([v_ref[pl.ds(k0, tk), g * HEAD_DIM:(g + 1) * HEAD_DIM], ones], axis=1)
            _softmax_step(lhs_ref.at[g], m_ref.at[g], acc_ref.at[g], kblk, vext,
                          lambda r: bias_ref[kb, (r * rc) % tq:(r * rc) % tq + rc, :],
                          rows=rep * tq, rc=rc, tk=tk)
        return c

    lax.fori_loop(0, nkb, kv_block, 0)
    for g in range(N_KV_HEADS_A):
        acc = acc_ref[g]
        o = acc[:, :HEAD_DIM] / acc[:, HEAD_DIM:]
        for r in range(rep):
            h0 = (g * rep + r) * HEAD_DIM
            o_ref[:, h0:h0 + HEAD_DIM] = o[r * tq:(r + 1) * tq].astype(o_ref.dtype)


def _dsa_attention(q, qi, kiwi, k_all, v_all, kid_all, *, batch, lq, tq, tk, l_valid, q_pos0, topk):
    lk_pad = k_all.shape[1]
    nq = lq // tq
    nkb_max = lk_pad // tk
    tri = jnp.asarray(np.triu(np.ones((tk, tk), np.float32)), BF16)
    a_q = N_HEADS_A * HEAD_DIM
    a_kv = N_KV_HEADS_A * HEAD_DIM
    rep = N_HEADS_A // N_KV_HEADS_A
    rc = min(ROW_CHUNK, tq)
    return pl.pallas_call(
        functools.partial(_dsa_body, tq=tq, tk=tk, nq=nq, l_valid=l_valid, q_pos0=q_pos0, topk=topk),
        out_shape=jax.ShapeDtypeStruct((batch * lq, a_q), BF16),
        grid=(batch, nq),
        in_specs=[
            pl.BlockSpec((tq, a_q), lambda b, t: (b * nq + t, 0)),
            pl.BlockSpec((tq, N_IDX_HEADS * IDX_DIM), lambda b, t: (b * nq + t, 0)),
            pl.BlockSpec((tq, LANES), lambda b, t: (b * nq + t, 0)),
            pl.BlockSpec((None, lk_pad, a_kv), lambda b, t: (b, 0, 0)),
            pl.BlockSpec((None, lk_pad, a_kv), lambda b, t: (b, 0, 0)),
            pl.BlockSpec((None, lk_pad, LANES), lambda b, t: (b, 0, 0)),
            pl.BlockSpec((tk, tk), lambda b, t: (0, 0)),
        ],
        out_specs=pl.BlockSpec((tq, a_q), lambda b, t: (b * nq + t, 0)),
        scratch_shapes=[pltpu.VMEM((nkb_max, tq, tk), jnp.int32), pltpu.VMEM((nkb_max, tq, tk), F32),
                        pltpu.VMEM((N_KV_HEADS_A, rep * tq, HEAD_DIM), BF16),
                        pltpu.VMEM((N_KV_HEADS_A, rep * tq, LANES), F32),
                        pltpu.VMEM((N_KV_HEADS_A, rep * tq, 2 * HEAD_DIM), F32),
                        pltpu.VMEM((tq // rc, N_IDX_HEADS // 2, 2 * rc, LANES), BF16),
                        pltpu.VMEM((tq // rc, N_IDX_HEADS, rc, LANES), F32)],
        compiler_params=_cparams(("parallel", "arbitrary")),
        name="dsa_attention",
    )(q, qi, kiwi, k_all, v_all, kid_all, tri)


def _diff_body(q_ref, k_ref, v_ref, lq1_ref, lk1_ref, lq2_ref, lk2_ref, g_ref, o_ref, lhs_ref, m_ref, acc_ref, *,
               tq, tk, nq, l_valid, q_pos0, lam_init, rc, heads):
    lam = (jnp.exp(jnp.sum(lq1_ref[...] * lk1_ref[...], axis=1, keepdims=True))
           - jnp.exp(jnp.sum(lq2_ref[...] * lk2_ref[...], axis=1, keepdims=True)) + lam_init)
    lane = lax.broadcasted_iota(jnp.int32, (tq, LANES), 1)
    ones = jnp.ones((tk, LANES), BF16)
    hd = 2 * DIFF_DIM

    def tile(t, carry):
        q0 = pl.multiple_of(t * tq, tq)
        q_first = q_pos0 + t * tq
        for h in range(heads):
            qc = q_ref[pl.ds(q0, tq), h * hd:(h + 1) * hd]
            lhs_ref[h, :tq, :] = jnp.where(lane < DIFF_DIM, qc, jnp.zeros_like(qc))
            lhs_ref[h, tq:, :] = jnp.where(lane >= DIFF_DIM, qc, jnp.zeros_like(qc))
        m_ref[...] = jnp.full(m_ref.shape, NEG_BIG, F32)
        acc_ref[...] = jnp.zeros(acc_ref.shape, F32)
        nfull = jnp.minimum(((q_first // CHUNK + 1) * CHUNK) // tk, l_valid // tk)
        nkb = (jnp.minimum(l_valid, ((q_first + tq - 1) // CHUNK + 1) * CHUNK) + tk - 1) // tk

        def step(kb, c, masked):
            k0 = pl.multiple_of(kb * tk, tk)

            def bias_fn(r):
                row0 = q_first + (r * rc) % tq
                q_chunk = lax.shift_right_arithmetic(row0 + lax.broadcasted_iota(jnp.int32, (rc, tk), 0), CHUNK_SHIFT)
                kpos = k0 + lax.broadcasted_iota(jnp.int32, (rc, tk), 1)
                adm = (lax.shift_right_arithmetic(kpos, CHUNK_SHIFT) <= q_chunk) & (kpos < l_valid)
                return jnp.where(adm, 0.0, NEG_BIG)

            for h in range(heads):
                kblk = k_ref[pl.ds(k0, tk), h * hd:(h + 1) * hd]
                vext = jnp.concatenate([v_ref[pl.ds(k0, tk), h * hd:(h + 1) * hd], ones], axis=1)
                _softmax_step(lhs_ref.at[h], m_ref.at[h], acc_ref.at[h], kblk, vext, bias_fn if masked else None,
                              rows=2 * tq, rc=rc, tk=tk)
            return c

        lax.fori_loop(0, nfull, functools.partial(step, masked=False), 0)
        lax.fori_loop(nfull, nkb, functools.partial(step, masked=True), 0)
        for h in range(heads):
            acc = acc_ref[h]
            o = acc[:, :hd] / acc[:, hd:]
            o = o[:tq] - lam * o[tq:]
            o = o * lax.rsqrt(jnp.mean(o * o, axis=-1, keepdims=True) + LN_EPS) * g_ref[...] * (1.0 - lam_init)
            o_ref[pl.ds(q0, tq), h * hd:(h + 1) * hd] = o.astype(o_ref.dtype)
        return carry

    lax.fori_loop(0, nq, tile, 0)


def _diff_attention(q, k_all, v_all, lq1, lk1, lq2, lk2, g, j, *, batch, lq, tq, tk, l_valid, q_pos0, lam_init):
    lk_pad = k_all.shape[1]
    nq = lq // tq
    hd = 2 * DIFF_DIM
    heads = DIFF_HEADS_PER_STEP
    rc = min(ROW_CHUNK, 2 * tq)
    lam_spec = pl.BlockSpec((None, 1, DIFF_DIM), lambda b, h: (j, 0, 0))
    return pl.pallas_call(
        functools.partial(_diff_body, tq=tq, tk=tk, nq=nq, l_valid=l_valid, q_pos0=q_pos0, lam_init=lam_init,
                          rc=rc, heads=heads),
        out_shape=jax.ShapeDtypeStruct((batch * lq, N_HEADS_B * hd), BF16),
        grid=(batch, N_HEADS_B // heads),
        in_specs=[
            pl.BlockSpec((lq, heads * hd), lambda b, h: (b, h)),
            pl.BlockSpec((None, lk_pad, heads * hd), lambda b, h: (b, 0, h)),
            pl.BlockSpec((None, lk_pad, heads * hd), lambda b, h: (b, 0, h)),
            lam_spec, lam_spec, lam_spec, lam_spec,
            pl.BlockSpec((None, 1, hd), lambda b, h: (j, 0, 0)),
        ],
        out_specs=pl.BlockSpec((lq, heads * hd), lambda b, h: (b, h)),
        scratch_shapes=[pltpu.VMEM((heads, 2 * tq, hd), BF16), pltpu.VMEM((heads, 2 * tq, LANES), F32),
                        pltpu.VMEM((heads, 2 * tq, 2 * hd), F32)],
        compiler_params=_cparams(("parallel", "parallel")),
        name="diff_attention",
    )(q, k_all, v_all, lq1, lk1, lq2, lk2, g)


def _pad_keys(x, lk_pad):
    pad = lk_pad - x.shape[1]
    return x if pad == 0 else jnp.pad(x, ((0, 0), (0, pad), (0, 0)))


def _trunk(x, p, pos0, pasts, w, *, tm):
    batch, seq, d = x.shape
    n = batch * seq
    depth = w["ln_g"].shape[0]
    alpha = float((2 * depth) ** 0.25)
    a_q = N_HEADS_A * HEAD_DIM
    a_kv = N_KV_HEADS_A * HEAD_DIM
    a_qi = N_IDX_HEADS * IDX_DIM
    b_qk = N_HEADS_B * 2 * DIFF_DIM

    period = max(1, seq // tm)
    table_rows = max(seq, tm)
    positions = (pos0 + np.arange(seq)).astype(np.int64)
    positions = np.tile(positions, table_rows // seq)
    tab128 = _rope_tables(positions, HEAD_DIM, "head")
    tab64 = _rope_tables(positions, IDX_DIM, "head")
    tabkw = _rope_tables(positions, IDX_DIM, "kiwi")

    xf = x.reshape(n, d)
    pf = p.reshape(depth, n, p.shape[-1])
    tf = 512 if w["ffn_w_gate"].shape[-1] % 512 == 0 else 128
    rows = []
    for i in range(depth):
        j = i // 2
        xf = _ffn(xf, w["ffn_w_gate"], w["ffn_w_up"], w["ffn_w_down"], w["ln_g"], w["ln_b"], i, 0, 0,
                  alpha=alpha, tm=tm, tf=tf)
        past = pasts[i]
        lk_new = seq if past is None else past[0].shape[1] + seq
        if i % 2 == 0:
            wa = w["a_w_in"]
            (q,) = _inproj(xf, wa[j, :, :a_q], [tab128], [(0, a_q, 0, HEAD_DIM ** -0.5 * LOG2_E, (BF16,))],
                           tm=tm, period_blocks=period)
            k, kb, v, vb = _inproj(xf, wa[j, :, a_q:a_q + 2 * a_kv], [tab128],
                                   [(0, a_kv, 0, 1.0, (F32, BF16)), (a_kv, 2 * a_kv, None, 1.0, (F32, BF16))],
                                   tm=tm, period_blocks=period)
            qi, kiwi = _inproj(xf, wa[j, :, a_q + 2 * a_kv:], [tab64, tabkw],
                               [(0, a_qi, 0, 1.0, (BF16,)), (a_qi, a_qi + LANES, 1, 1.0, (F32,))],
                               tm=tm, period_blocks=period)
            ki = kiwi[:, :IDX_DIM]
            rows.append((k.reshape(batch, seq, N_KV_HEADS_A, HEAD_DIM), v.reshape(batch, seq, N_KV_HEADS_A, HEAD_DIM),
                         ki.reshape(batch, seq, IDX_DIM)))
            k_all, v_all, ki_all = kb.reshape(batch, seq, a_kv), vb.reshape(batch, seq, a_kv), ki.reshape(batch, seq, IDX_DIM)
            if past is not None:
                pk, pv, pki = past
                k_all = jnp.concatenate([pk.reshape(batch, -1, a_kv).astype(BF16), k_all], axis=1)
                v_all = jnp.concatenate([pv.reshape(batch, -1, a_kv).astype(BF16), v_all], axis=1)
                ki_all = jnp.concatenate([pki, ki_all], axis=1)
            kid_all = jnp.concatenate([ki_all, ki_all], axis=-1).astype(BF16)
            tq = min(DSA_Q_TILE, seq)
            tk = KEY_BLOCK
            lk_pad = -(-lk_new // tk) * tk
            topk = min(TOPK_MAX, lk_new // 4)
            o = _dsa_attention(q, qi, kiwi, _pad_keys(k_all, lk_pad), _pad_keys(v_all, lk_pad), _pad_keys(kid_all, lk_pad),
                               batch=batch, lq=seq, tq=tq, tk=tk, l_valid=lk_new, q_pos0=pos0, topk=topk)
            xf = _oproj(xf, o, w["a_w_out"], w["ln_g"], w["ln_b"], i, j, alpha=alpha, tm=tm)
        else:
            wb = w["b_w_in"]
            lam_init = 0.8 - 0.6 * math.exp(-0.3 * i)
            (q,) = _inproj(xf, wb[j, :, :b_qk], [tab64], [(0, b_qk, 0, DIFF_DIM ** -0.5 * LOG2_E, (BF16,))],
                           tm=tm, period_blocks=period)
            k, kb = _inproj(xf, wb[j, :, b_qk:2 * b_qk], [tab64], [(0, b_qk, 0, 1.0, (F32, BF16))],
                            tm=tm, period_blocks=period)
            v, vb = _inproj(xf, wb[j, :, 2 * b_qk:], [], [(0, b_qk, None, 1.0, (F32, BF16))],
                            tm=tm, period_blocks=period)
            rows.append((k.reshape(batch, seq, N_HEADS_B, 2, DIFF_DIM), v.reshape(batch, seq, N_HEADS_B, 2 * DIFF_DIM)))
            k_all, v_all = kb.reshape(batch, seq, b_qk), vb.reshape(batch, seq, b_qk)
            if past is not None:
                pk, pv = past
                k_all = jnp.concatenate([pk.reshape(batch, -1, b_qk).astype(BF16), k_all], axis=1)
                v_all = jnp.concatenate([pv.reshape(batch, -1, b_qk).astype(BF16), v_all], axis=1)
            tq = min(256, seq)
            tk = KEY_BLOCK
            lk_pad = -(-lk_new // tk) * tk
            o = _diff_attention(q, _pad_keys(k_all, lk_pad), _pad_keys(v_all, lk_pad),
                                w["b_lambda_q1"], w["b_lambda_k1"], w["b_lambda_q2"], w["b_lambda_k2"], w["b_subln"], j,
                                batch=batch, lq=seq, tq=tq, tk=tk, l_valid=lk_new, q_pos0=pos0, lam_init=lam_init)
            xf = _oproj(xf, o, w["b_w_out"], w["ln_g"], w["ln_b"], i, j, alpha=alpha, tm=tm)
        xf = _ffn(xf, w["ffn_w_gate"], w["ffn_w_up"], w["ffn_w_down"], w["ln_g"], w["ln_b"], i, 1, 2,
                  alpha=alpha, tm=tm, tf=tf)
        xf = _ple(xf, pf, w["ple_w_gate"], w["ple_b_gate"], w["ple_w_proj"], w["ln_g"], w["ln_b"], i,
                  alpha=alpha, tm=tm)
    return xf.reshape(batch, seq, d), rows


def kernel(x_prompt, x_sample, cache_l0_k, cache_l0_v, cache_l0_kidx, cache_l1_k, cache_l1_v, cache_l2_k, cache_l2_v, cache_l2_kidx, cache_l3_k, cache_l3_v, p_prompt, p_sample, ln_g, ln_b, ffn_w_gate, ffn_w_up, ffn_w_down, ple_w_gate, ple_b_gate, ple_w_proj, a_w_in, a_w_out, b_w_in, b_w_out, b_lambda_q1, b_lambda_k1, b_lambda_q2, b_lambda_k2, b_subln):
    depth, _, d = ln_g.shape
    a_cols = a_w_in.shape[-1]
    a_pad = -(-a_cols // LANES) * LANES - a_cols
    w = {
        "ln_g": ln_g.reshape(depth, 4, 1, d),
        "ln_b": ln_b.reshape(depth, 4, 1, d),
        "ffn_w_gate": ffn_w_gate.astype(BF16),
        "ffn_w_up": ffn_w_up.astype(BF16),
        "ffn_w_down": ffn_w_down.astype(BF16),
        "ple_w_gate": ple_w_gate.astype(BF16),
        "ple_b_gate": ple_b_gate.reshape(depth, 1, d),
        "ple_w_proj": ple_w_proj.astype(BF16),
        "a_w_in": jnp.pad(a_w_in, ((0, 0), (0, 0), (0, a_pad))).astype(BF16),
        "a_w_out": a_w_out.astype(BF16),
        "b_w_in": b_w_in.astype(BF16),
        "b_w_out": b_w_out.astype(BF16),
        "b_lambda_q1": b_lambda_q1.reshape(-1, 1, DIFF_DIM),
        "b_lambda_k1": b_lambda_k1.reshape(-1, 1, DIFF_DIM),
        "b_lambda_q2": b_lambda_q2.reshape(-1, 1, DIFF_DIM),
        "b_lambda_k2": b_lambda_k2.reshape(-1, 1, DIFF_DIM),
        "b_subln": b_subln.reshape(-1, 1, 2 * DIFF_DIM),
    }
    past_len = cache_l0_k.shape[1]
    pasts_s = [(cache_l0_k, cache_l0_v, cache_l0_kidx), (cache_l1_k, cache_l1_v),
               (cache_l2_k, cache_l2_v, cache_l2_kidx), (cache_l3_k, cache_l3_v)]
    n_p = x_prompt.shape[0] * x_prompt.shape[1]
    n_s = x_sample.shape[0] * x_sample.shape[1]
    y_p, rows_p = _trunk(x_prompt, p_prompt, 0, [None] * depth, w, tm=min(512, n_p))
    y_s, rows_s = _trunk(x_sample, p_sample, past_len, pasts_s, w, tm=min(256, n_s))
    out = [y_p, y_s]
    for rp, rs in zip(rows_p, rows_s):
        out.extend(rp)
        out.extend(rs)
    return tuple(out)
```

```python
import functools
import math

import numpy as np
import jax
import jax.numpy as jnp
from jax import lax
from jax.experimental import pallas as pl
from jax.experimental.pallas import tpu as pltpu

CHUNK = 64
CHUNK_SHIFT = 6
HEAD_DIM = 128
N_HEADS_A = 16
N_KV_HEADS_A = 4
N_IDX_HEADS = 16
IDX_DIM = 64
TOPK_MAX = 256
N_HEADS_B = 16
DIFF_DIM = 64
ROPE_FRACTION = 4
ROPE_THETA = 500000.0
LN_EPS = 1e-5

LANES = 128
KEY_BLOCK = 256
ROW_CHUNK = 128
DIFF_HEADS_PER_STEP = 8
DSA_Q_TILE = 256
ROW_GROUPS = 2
LOG2_E = 1.4426950408889634
VMEM_LIMIT_BYTES = 56 * 1024 * 1024
NEG_BIG = -1e30
INT_MIN = -(2 ** 31)

F32 = jnp.float32
BF16 = jnp.bfloat16
HEAD_MAJOR_T = "f32 (batch, head, map, 64, seq)"


def _cparams(sem):
    return pltpu.CompilerParams(dimension_semantics=sem, vmem_limit_bytes=VMEM_LIMIT_BYTES)


def _layer_norm(z, g, b):
    mu = jnp.mean(z, axis=-1, keepdims=True)
    zc = z - mu
    var = jnp.mean(zc * zc, axis=-1, keepdims=True)
    return zc * lax.rsqrt(var + LN_EPS) * g + b


def _dot(a, b):
    return jnp.dot(a, b, preferred_element_type=F32)


def _lane_tile(x, n):
    return x if n == 1 else jnp.concatenate([x] * n, axis=1)


def _dot_nt(a, b):
    return lax.dot_general(a, b, (((1,), (1,)), ((), ())), preferred_element_type=F32)


def _ffn_body(x_ref, wg_ref, wu_ref, wd_ref, g_ref, b_ref, o_ref, xb_ref, *, alpha, nf):
    f = pl.program_id(1)

    @pl.when(f == 0)
    def _():
        xb_ref[...] = x_ref[...].astype(BF16)
        o_ref[...] = jnp.zeros_like(o_ref)

    xb = xb_ref[...]
    gate = _dot(xb, wg_ref[...])
    up = _dot(xb, wu_ref[...])
    h = (gate * jax.nn.sigmoid(gate) * up).astype(BF16)
    o_ref[...] += _dot(h, wd_ref[...])

    @pl.when(f == nf - 1)
    def _():
        z = alpha * x_ref[...] + 0.5 * o_ref[...]
        o_ref[...] = _layer_norm(z, g_ref[...], b_ref[...])


def _ffn(x, wg, wu, wd, ln_g, ln_b, layer, sub, ln_idx, *, alpha, tm, tf):
    n, d = x.shape
    f_dim = wg.shape[-1]
    nf = f_dim // tf
    return pl.pallas_call(
        functools.partial(_ffn_body, alpha=alpha, nf=nf),
        out_shape=jax.ShapeDtypeStruct((n, d), F32),
        grid=(n // tm, nf),
        in_specs=[
            pl.BlockSpec((tm, d), lambda i, f: (i, 0)),
            pl.BlockSpec((None, None, d, tf), lambda i, f: (layer, sub, 0, f)),
            pl.BlockSpec((None, None, d, tf), lambda i, f: (layer, sub, 0, f)),
            pl.BlockSpec((None, None, tf, d), lambda i, f: (layer, sub, f, 0)),
            pl.BlockSpec((None, None, 1, d), lambda i, f: (layer, ln_idx, 0, 0)),
            pl.BlockSpec((None, None, 1, d), lambda i, f: (layer, ln_idx, 0, 0)),
        ],
        out_specs=pl.BlockSpec((tm, d), lambda i, f: (i, 0)),
        scratch_shapes=[pltpu.VMEM((tm, d), BF16)],
        compiler_params=_cparams(("parallel", "arbitrary")),
        name="ffn_postnorm",
    )(x, wg, wu, wd, ln_g, ln_b)


def _ple_body(x_ref, p_ref, wg_ref, bg_ref, wp_ref, g_ref, b_ref, o_ref, *, alpha):
    rows = x_ref.shape[0] // ROW_GROUPS if x_ref.shape[0] % (ROW_GROUPS * 8) == 0 else x_ref.shape[0]
    for r0 in range(0, x_ref.shape[0], rows):
        x = x_ref[r0:r0 + rows, :]
        gate = jax.nn.sigmoid(_dot(x.astype(BF16), wg_ref[...]) + bg_ref[...])
        proj = _dot(p_ref[r0:r0 + rows, :].astype(BF16), wp_ref[...])
        o_ref[r0:r0 + rows, :] = _layer_norm(alpha * x + gate * proj, g_ref[...], b_ref[...])


def _ple(x, p, wg, bg, wp, ln_g, ln_b, layer, *, alpha, tm):
    n, d = x.shape
    pd = p.shape[-1]
    return pl.pallas_call(
        functools.partial(_ple_body, alpha=alpha),
        out_shape=jax.ShapeDtypeStruct((n, d), F32),
        grid=(n // tm,),
        in_specs=[
            pl.BlockSpec((tm, d), lambda i: (i, 0)),
            pl.BlockSpec((None, tm, pd), lambda i: (layer, i, 0)),
            pl.BlockSpec((None, d, d), lambda i: (layer, 0, 0)),
            pl.BlockSpec((None, 1, d), lambda i: (layer, 0, 0)),
            pl.BlockSpec((None, pd, d), lambda i: (layer, 0, 0)),
            pl.BlockSpec((None, None, 1, d), lambda i: (layer, 3, 0, 0)),
            pl.BlockSpec((None, None, 1, d), lambda i: (layer, 3, 0, 0)),
        ],
        out_specs=pl.BlockSpec((tm, d), lambda i: (i, 0)),
        compiler_params=_cparams(("parallel",)),
        name="ple_postnorm",
    )(x, p, wg, bg, wp, ln_g, ln_b)


def _oproj_body(x_ref, a_ref, w_ref, g_ref, b_ref, o_ref, *, alpha):
    rows = x_ref.shape[0] // ROW_GROUPS if x_ref.shape[0] % (ROW_GROUPS * 8) == 0 else x_ref.shape[0]
    for r0 in range(0, x_ref.shape[0], rows):
        mix = _dot(a_ref[r0:r0 + rows, :], w_ref[...])
        o_ref[r0:r0 + rows, :] = _layer_norm(alpha * x_ref[r0:r0 + rows, :] + mix, g_ref[...], b_ref[...])


def _oproj(x, a, w, ln_g, ln_b, layer, j, *, alpha, tm):
    n, d = x.shape
    ad = a.shape[-1]
    return pl.pallas_call(
        functools.partial(_oproj_body, alpha=alpha),
        out_shape=jax.ShapeDtypeStruct((n, d), F32),
        grid=(n // tm,),
        in_specs=[
            pl.BlockSpec((tm, d), lambda i: (i, 0)),
            pl.BlockSpec((tm, ad), lambda i: (i, 0)),
            pl.BlockSpec((None, ad, d), lambda i: (j, 0, 0)),
            pl.BlockSpec((None, None, 1, d), lambda i: (layer, 1, 0, 0)),
            pl.BlockSpec((None, None, 1, d), lambda i: (layer, 1, 0, 0)),
        ],
        out_specs=pl.BlockSpec((tm, d), lambda i: (i, 0)),
        compiler_params=_cparams(("parallel",)),
        name="oproj_postnorm",
    )(x, a, w, ln_g, ln_b)


def _rope_tables(positions, head_dim, kind):
    rd = head_dim // ROPE_FRACTION
    half = rd // 2
    inv_freq = (np.float32(ROPE_THETA) ** (-(np.arange(half, dtype=np.float32) / np.float32(half)))).astype(np.float32)
    ang = positions.astype(np.float32)[:, None] * inv_freq[None, :]
    cos, sin = np.cos(ang).astype(np.float32), np.sin(ang).astype(np.float32)
    npos = positions.shape[0]
    c = np.ones((npos, LANES), np.float32)
    sa = np.zeros((npos, LANES), np.float32)
    sb = np.zeros((npos, LANES), np.float32)
    starts = range(0, LANES, head_dim) if kind == "head" else [0]
    for h0 in starts:
        c[:, h0:h0 + half] = cos
        c[:, h0 + half:h0 + rd] = cos
        sa[:, h0:h0 + half] = -sin
        sb[:, h0 + half:h0 + rd] = sin
    if kind == "kiwi":
        c[:, IDX_DIM:IDX_DIM + N_IDX_HEADS] = (N_IDX_HEADS * IDX_DIM) ** -0.5
    return jnp.asarray(c), jnp.asarray(sa), jnp.asarray(sb), half


def _inproj_body(*refs, segs, n_tab, col_step):
    x_ref, w_ref = refs[0], refs[1]
    tab_refs = refs[2:2 + 3 * n_tab]
    out_refs = refs[2 + 3 * n_tab:]
    xb = x_ref[...].astype(BF16)
    oi = 0
    for (c0, c1, tab, half, scale, dtypes) in segs:
        outs = out_refs[oi:oi + len(dtypes)]
        oi += len(dtypes)
        for j0 in range(c0, c1, col_step):
            wj = min(col_step, c1 - j0)
            y = _dot(xb, w_ref[:, j0:j0 + wj])
            for c in range(wj // LANES):
                yc = y[:, c * LANES:(c + 1) * LANES]
                if tab is not None:
                    cc, sa, sb = (r[...] for r in tab_refs[3 * tab:3 * tab + 3])
                    yc = (yc * cc + pltpu.roll(yc, LANES - half, 1) * sa + pltpu.roll(yc, half, 1) * sb)
                if scale != 1.0:
                    yc = yc * scale
                lo = j0 - c0 + c * LANES
                for o_ref, dt in zip(outs, dtypes):
                    if dt == HEAD_MAJOR_T:
                        yt = yc.T
                        o_ref[lo // LANES, 0] = yt[:DIFF_DIM]
                        o_ref[lo // LANES, 1] = yt[DIFF_DIM:]
                    else:
                        o_ref[:, lo:lo + LANES] = yc.astype(dt)


def _inproj(x, w, tables, segs, *, tm, period_blocks, seq=None):
    n, d = x.shape
    ctot = w.shape[1]
    segs_full = tuple((c0, c1, tab, (tables[tab][3] if tab is not None else 0), scale, tuple(dts))
                      for (c0, c1, tab, scale, dts) in segs)
    in_specs = [pl.BlockSpec((tm, d), lambda i: (i, 0)), pl.BlockSpec((d, ctot), lambda i: (0, 0))]
    args = [x, w]
    for (c, sa, sb, _) in tables:
        for t in (c, sa, sb):
            in_specs.append(pl.BlockSpec((tm, LANES), lambda i: (i % period_blocks, 0)))
            args.append(t)
    out_shape, out_specs = [], []
    for (c0, c1, _, _, dts) in segs:
        for dt in dts:
            if dt == HEAD_MAJOR_T:
                heads = (c1 - c0) // (2 * DIFF_DIM)
                out_shape.append(jax.ShapeDtypeStruct((n // seq, heads, 2, DIFF_DIM, seq), F32))
                out_specs.append(pl.BlockSpec((None, heads, 2, DIFF_DIM, tm),
                                              lambda i: (i // (seq // tm), 0, 0, 0, i % (seq // tm))))
            else:
                out_shape.append(jax.ShapeDtypeStruct((n, c1 - c0), dt))
                out_specs.append(pl.BlockSpec((tm, c1 - c0), lambda i: (i, 0)))
    return pl.pallas_call(
        functools.partial(_inproj_body, segs=segs_full, n_tab=len(tables), col_step=512),
        out_shape=out_shape,
        grid=(n // tm,),
        in_specs=in_specs,
        out_specs=out_specs,
        compiler_params=_cparams(("parallel",)),
        name="inproj_rope",
    )(*args)


def _softmax_step(lhs_ref, m_ref, acc_ref, kblk, vext, bias_fn, *, rows, rc, tk):
    for r in range(rows // rc):
        r0, r1 = r * rc, (r + 1) * rc
        s = _dot_nt(lhs_ref[r0:r1, :], kblk)
        if bias_fn is not None:
            s = s + bias_fn(r)
        m_prev = m_ref[r0:r1, :]
        m_next = jnp.maximum(m_prev, jnp.max(s, axis=1, keepdims=True))
        a = jnp.exp2(m_prev - m_next)
        p = jnp.exp2(s - _lane_tile(m_next, tk // LANES))
        acc_ref[r0:r1, :] = acc_ref[r0:r1, :] * _lane_tile(a, 2) + _dot(p.astype(BF16), vext)
        m_ref[r0:r1, :] = m_next


def _sortable_key(score):
    score = jnp.where(score == 0.0, 0.0, score)
    bits = pltpu.bitcast(score, jnp.int32)
    return bits ^ (lax.shift_right_arithmetic(bits, 31) & 0x7FFFFFFF)


def _dsa_body(q_ref, qi_ref, kiwi_ref, k_ref, v_ref, kid_ref, tri_ref, o_ref, key_ref, bias_ref,
              lhs_ref, m_ref, acc_ref, qlhs_ref, wb_ref, *, tq, tk, nq, l_valid, q_pos0, topk):
    t = pl.program_id(1)
    q_first = q_pos0 + t * tq
    q_last_chunk = (q_first + tq - 1) // CHUNK
    kmax = jnp.minimum(l_valid, (q_last_chunk + 1) * CHUNK)
    nkb = (kmax + tk - 1) // tk

    rc = min(ROW_CHUNK, tq)
    nrc = tq // rc
    nrep = tk // LANES

    lane128 = lax.broadcasted_iota(jnp.int32, (rc, LANES), 1)
    for rh in range(nrc):
        for c in range(N_IDX_HEADS // 2):
            qc = qi_ref[rh * rc:(rh + 1) * rc, c * LANES:(c + 1) * LANES]
            qlhs_ref[rh, c, :rc, :] = jnp.where(lane128 < IDX_DIM, qc, jnp.zeros_like(qc))
            qlhs_ref[rh, c, rc:, :] = jnp.where(lane128 >= IDX_DIM, qc, jnp.zeros_like(qc))
        wi = kiwi_ref[rh * rc:(rh + 1) * rc, IDX_DIM:IDX_DIM + N_IDX_HEADS]
        for h in range(N_IDX_HEADS):
            wb_ref[rh, h] = jnp.broadcast_to(wi[:, h:h + 1], (rc, LANES))

    def idx_block(kb, carry):
        k0 = pl.multiple_of(kb * tk, tk)
        kblk = kid_ref[pl.ds(k0, tk), :]
        kpos = k0 + lax.broadcasted_iota(jnp.int32, (rc, tk), 1)
        for rh in range(nrc):
            score = jnp.zeros((rc, tk), F32)
            for c in range(N_IDX_HEADS // 2):
                rel = jnp.maximum(_dot_nt(qlhs_ref[rh, c], kblk), 0.0)
                score = (score + _lane_tile(wb_ref[rh, 2 * c], nrep) * rel[:rc]
                         + _lane_tile(wb_ref[rh, 2 * c + 1], nrep) * rel[rc:])
            q_chunk = lax.shift_right_arithmetic(
                q_first + rh * rc + lax.broadcasted_iota(jnp.int32, (rc, tk), 0), CHUNK_SHIFT)
            adm = (lax.shift_right_arithmetic(kpos, CHUNK_SHIFT) <= q_chunk) & (kpos < l_valid)
            key_ref[kb, rh * rc:(rh + 1) * rc, :] = jnp.where(adm, _sortable_key(score), INT_MIN)
        return carry

    lax.fori_loop(0, nkb, idx_block, 0)
    key_ref[nkb] = jnp.full((tq, tk), INT_MIN, jnp.int32)

    def find_threshold(rh):
        def count(pred):
            def cnt_pair(kp, cnt):
                for kb in (2 * kp, 2 * kp + 1):
                    for c in range(nrep):
                        cnt = cnt + jnp.where(pred(key_ref[kb, rh * rc:(rh + 1) * rc, c * LANES:(c + 1) * LANES]), 1, 0)
                return cnt
            cnt = lax.fori_loop(0, (nkb + 1) // 2, cnt_pair, jnp.zeros((rc, LANES), jnp.int32))
            return jnp.sum(cnt, axis=1, keepdims=True)

        def bit_step(i, tu):
            cand_u = tu | lax.shift_left(jnp.int32(1), 31 - i)
            cand_s = cand_u ^ INT_MIN
            return jnp.where(count(lambda key: key >= cand_s) >= topk, cand_u, tu)

        thr128 = lax.fori_loop(0, 32, bit_step, jnp.zeros((rc, LANES), jnp.int32)) ^ INT_MIN
        n_gt = count(lambda key: key > thr128)
        return thr128[:, :1], (topk - n_gt).astype(F32)

    found = [find_threshold(rh) for rh in range(nrc)]
    thr = found[0][0] if nrc == 1 else jnp.concatenate([f[0] for f in found], axis=0)
    need = found[0][1] if nrc == 1 else jnp.concatenate([f[1] for f in found], axis=0)

    def sel_block(kb, seen):
        key = key_ref[kb]
        eq = key == thr
        eqf = jnp.where(eq, 1.0, 0.0)
        pref = _dot(eqf.astype(BF16), tri_ref[...]) + seen
        keep_tie = jnp.where(eq, jnp.where(pref <= need, 1, 0), 0)
        sel = jnp.where(key > thr, 1, keep_tie)
        sel = jnp.where(key == INT_MIN, 0, sel)
        bias_ref[kb] = jnp.where(sel == 1, 0.0, NEG_BIG)
        return pref[:, tk - 1:tk]

    lax.fori_loop(0, nkb, sel_block, jnp.zeros((tq, 1), F32))

    rep = N_HEADS_A // N_KV_HEADS_A
    rc = min(ROW_CHUNK, tq)
    ones = jnp.ones((tk, LANES), BF16)
    for g in range(N_KV_HEADS_A):
        for r in range(rep):
            lhs_ref[g, r * tq:(r + 1) * tq, :] = q_ref[:, (g * rep + r) * HEAD_DIM:(g * rep + r + 1) * HEAD_DIM]
    m_ref[...] = jnp.full(m_ref.shape, NEG_BIG, F32)
    acc_ref[...] = jnp.zeros(acc_ref.shape, F32)

    def kv_block(kb, c):
        k0 = pl.multiple_of(kb * tk, tk)
        for g in range(N_KV_HEADS_A):
            kblk = k_ref[pl.ds(k0, tk), g * HEAD_DIM:(g + 1) * HEAD_DIM]
            vext = jnp.concatenate([v_ref[pl.ds(k0, tk), g * HEAD_DIM:(g + 1) * HEAD_DIM], ones], axis=1)
            _softmax_step(lhs_ref.at[g], m_ref.at[g], acc_ref.at[g], kblk, vext,
                          lambda r: bias_ref[kb, (r * rc) % tq:(r * rc) % tq + rc, :],
                          rows=rep * tq, rc=rc, tk=tk)
        return c

    lax.fori_loop(0, nkb, kv_block, 0)
    for g in range(N_KV_HEADS_A):
        acc = acc_ref[g]
        o = acc[:, :HEAD_DIM] / acc[:, HEAD_DIM:]
        for r in range(rep):
            h0 = (g * rep + r) * HEAD_DIM
            o_ref[:, h0:h0 + HEAD_DIM] = o[r * tq:(r + 1) * tq].astype(o_ref.dtype)


def _dsa_attention(q, qi, kiwi, k_all, v_all, kid_all, *, batch, lq, tq, tk, l_valid, q_pos0, topk):
    lk_pad = k_all.shape[1]
    nq = lq // tq
    nkb_max = lk_pad // tk
    tri = jnp.asarray(np.triu(np.ones((tk, tk), np.float32)), BF16)
    a_q = N_HEADS_A * HEAD_DIM
    a_kv = N_KV_HEADS_A * HEAD_DIM
    rep = N_HEADS_A // N_KV_HEADS_A
    rc = min(ROW_CHUNK, tq)
    return pl.pallas_call(
        functools.partial(_dsa_body, tq=tq, tk=tk, nq=nq, l_valid=l_valid, q_pos0=q_pos0, topk=topk),
        out_shape=jax.ShapeDtypeStruct((batch * lq, a_q), BF16),
        grid=(batch, nq),
        in_specs=[
            pl.BlockSpec((tq, a_q), lambda b, t: (b * nq + t, 0)),
            pl.BlockSpec((tq, N_IDX_HEADS * IDX_DIM), lambda b, t: (b * nq + t, 0)),
            pl.BlockSpec((tq, LANES), lambda b, t: (b * nq + t, 0)),
            pl.BlockSpec((None, lk_pad, a_kv), lambda b, t: (b, 0, 0)),
            pl.BlockSpec((None, lk_pad, a_kv), lambda b, t: (b, 0, 0)),
            pl.BlockSpec((None, lk_pad, LANES), lambda b, t: (b, 0, 0)),
            pl.BlockSpec((tk, tk), lambda b, t: (0, 0)),
        ],
        out_specs=pl.BlockSpec((tq, a_q), lambda b, t: (b * nq + t, 0)),
        scratch_shapes=[pltpu.VMEM((nkb_max + 1, tq, tk), jnp.int32), pltpu.VMEM((nkb_max, tq, tk), F32),
                        pltpu.VMEM((N_KV_HEADS_A, rep * tq, HEAD_DIM), BF16),
                        pltpu.VMEM((N_KV_HEADS_A, rep * tq, LANES), F32),
                        pltpu.VMEM((N_KV_HEADS_A, rep * tq, 2 * HEAD_DIM), F32),
                        pltpu.VMEM((tq // rc, N_IDX_HEADS // 2, 2 * rc, LANES), BF16),
                        pltpu.VMEM((tq // rc, N_IDX_HEADS, rc, LANES), F32)],
        compiler_params=_cparams(("parallel", "arbitrary")),
        name="dsa_attention",
    )(q, qi, kiwi, k_all, v_all, kid_all, tri)


def _diff_body(q_ref, k_ref, v_ref, lq1_ref, lk1_ref, lq2_ref, lk2_ref, g_ref, o_ref, lhs_ref, m_ref, acc_ref, *,
               tq, tk, nq, l_valid, q_pos0, lam_init, rc, heads):
    lam = (jnp.exp(jnp.sum(lq1_ref[...] * lk1_ref[...], axis=1, keepdims=True))
           - jnp.exp(jnp.sum(lq2_ref[...] * lk2_ref[...], axis=1, keepdims=True)) + lam_init)
    lane = lax.broadcasted_iota(jnp.int32, (tq, LANES), 1)
    ones = jnp.ones((tk, LANES), BF16)
    hd = 2 * DIFF_DIM

    def tile(t, carry):
        q0 = pl.multiple_of(t * tq, tq)
        q_first = q_pos0 + t * tq
        for h in range(heads):
            qc = q_ref[pl.ds(q0, tq), h * hd:(h + 1) * hd]
            lhs_ref[h, :tq, :] = jnp.where(lane < DIFF_DIM, qc, jnp.zeros_like(qc))
            lhs_ref[h, tq:, :] = jnp.where(lane >= DIFF_DIM, qc, jnp.zeros_like(qc))
        m_ref[...] = jnp.full(m_ref.shape, NEG_BIG, F32)
        acc_ref[...] = jnp.zeros(acc_ref.shape, F32)
        nfull = jnp.minimum(((q_first // CHUNK + 1) * CHUNK) // tk, l_valid // tk)
        nkb = (jnp.minimum(l_valid, ((q_first + tq - 1) // CHUNK + 1) * CHUNK) + tk - 1) // tk

        def step(kb, c, masked):
            k0 = pl.multiple_of(kb * tk, tk)

            def bias_fn(r):
                row0 = q_first + (r * rc) % tq
                q_chunk = lax.shift_right_arithmetic(row0 + lax.broadcasted_iota(jnp.int32, (rc, tk), 0), CHUNK_SHIFT)
                kpos = k0 + lax.broadcasted_iota(jnp.int32, (rc, tk), 1)
                adm = (lax.shift_right_arithmetic(kpos, CHUNK_SHIFT) <= q_chunk) & (kpos < l_valid)
                return jnp.where(adm, 0.0, NEG_BIG)

            for h in range(heads):
                kblk = k_ref[pl.ds(k0, tk), h * hd:(h + 1) * hd]
                vext = jnp.concatenate([v_ref[pl.ds(k0, tk), h * hd:(h + 1) * hd], ones], axis=1)
                _softmax_step(lhs_ref.at[h], m_ref.at[h], acc_ref.at[h], kblk, vext, bias_fn if masked else None,
                              rows=2 * tq, rc=rc, tk=tk)
            return c

        lax.fori_loop(0, nfull, functools.partial(step, masked=False), 0)
        lax.fori_loop(nfull, nkb, functools.partial(step, masked=True), 0)
        for h in range(heads):
            acc = acc_ref[h]
            o = acc[:, :hd] / acc[:, hd:]
            o = o[:tq] - lam * o[tq:]
            o = o * lax.rsqrt(jnp.mean(o * o, axis=-1, keepdims=True) + LN_EPS) * g_ref[...] * (1.0 - lam_init)
            o_ref[pl.ds(q0, tq), h * hd:(h + 1) * hd] = o.astype(o_ref.dtype)
        return carry

    lax.fori_loop(0, nq, tile, 0)


def _diff_attention(q, k_all, v_all, lq1, lk1, lq2, lk2, g, j, *, batch, lq, tq, tk, l_valid, q_pos0, lam_init):
    lk_pad = k_all.shape[1]
    nq = lq // tq
    hd = 2 * DIFF_DIM
    heads = DIFF_HEADS_PER_STEP
    rc = min(ROW_CHUNK, 2 * tq)
    lam_spec = pl.BlockSpec((None, 1, DIFF_DIM), lambda b, h: (j, 0, 0))
    return pl.pallas_call(
        functools.partial(_diff_body, tq=tq, tk=tk, nq=nq, l_valid=l_valid, q_pos0=q_pos0, lam_init=lam_init,
                          rc=rc, heads=heads),
        out_shape=jax.ShapeDtypeStruct((batch * lq, N_HEADS_B * hd), BF16),
        grid=(batch, N_HEADS_B // heads),
        in_specs=[
            pl.BlockSpec((lq, heads * hd), lambda b, h: (b, h)),
            pl.BlockSpec((None, lk_pad, heads * hd), lambda b, h: (b, 0, h)),
            pl.BlockSpec((None, lk_pad, heads * hd), lambda b, h: (b, 0, h)),
            lam_spec, lam_spec, lam_spec, lam_spec,
            pl.BlockSpec((None, 1, hd), lambda b, h: (j, 0, 0)),
        ],
        out_specs=pl.BlockSpec((lq, heads * hd), lambda b, h: (b, h)),
        scratch_shapes=[pltpu.VMEM((heads, 2 * tq, hd), BF16), pltpu.VMEM((heads, 2 * tq, LANES), F32),
                        pltpu.VMEM((heads, 2 * tq, 2 * hd), F32)],
        compiler_params=_cparams(("parallel", "parallel")),
        name="diff_attention",
    )(q, k_all, v_all, lq1, lk1, lq2, lk2, g)


def _pad_keys(x, lk_pad):
    pad = lk_pad - x.shape[1]
    return x if pad == 0 else jnp.pad(x, ((0, 0), (0, pad), (0, 0)))


def _trunk(x, p, pos0, pasts, w, *, tm):
    batch, seq, d = x.shape
    n = batch * seq
    depth = w["ln_g"].shape[0]
    alpha = float((2 * depth) ** 0.25)
    a_q = N_HEADS_A * HEAD_DIM
    a_kv = N_KV_HEADS_A * HEAD_DIM
    a_qi = N_IDX_HEADS * IDX_DIM
    b_qk = N_HEADS_B * 2 * DIFF_DIM

    period = max(1, seq // tm)
    table_rows = max(seq, tm)
    positions = (pos0 + np.arange(seq)).astype(np.int64)
    positions = np.tile(positions, table_rows // seq)
    tab128 = _rope_tables(positions, HEAD_DIM, "head")
    tab64 = _rope_tables(positions, IDX_DIM, "head")
    tabkw = _rope_tables(positions, IDX_DIM, "kiwi")

    xf = x.reshape(n, d)
    pf = p.reshape(depth, n, p.shape[-1])
    tf = 512 if w["ffn_w_gate"].shape[-1] % 512 == 0 else 128
    rows = []
    for i in range(depth):
        j = i // 2
        xf = _ffn(xf, w["ffn_w_gate"], w["ffn_w_up"], w["ffn_w_down"], w["ln_g"], w["ln_b"], i, 0, 0,
                  alpha=alpha, tm=tm, tf=tf)
        past = pasts[i]
        lk_new = seq if past is None else past[0].shape[1] + seq
        if i % 2 == 0:
            wa = w["a_w_in"]
            (q,) = _inproj(xf, wa[j, :, :a_q], [tab128], [(0, a_q, 0, HEAD_DIM ** -0.5 * LOG2_E, (BF16,))],
                           tm=tm, period_blocks=period)
            k, kb, v, vb = _inproj(xf, wa[j, :, a_q:a_q + 2 * a_kv], [tab128],
                                   [(0, a_kv, 0, 1.0, (F32, BF16)), (a_kv, 2 * a_kv, None, 1.0, (F32, BF16))],
                                   tm=tm, period_blocks=period)
            qi, kiwi = _inproj(xf, wa[j, :, a_q + 2 * a_kv:], [tab64, tabkw],
                               [(0, a_qi, 0, 1.0, (BF16,)), (a_qi, a_qi + LANES, 1, 1.0, (F32,))],
                               tm=tm, period_blocks=period)
            ki = kiwi[:, :IDX_DIM]
            rows.append((k.reshape(batch, seq, N_KV_HEADS_A, HEAD_DIM), v.reshape(batch, seq, N_KV_HEADS_A, HEAD_DIM),
                         ki.reshape(batch, seq, IDX_DIM)))
            k_all, v_all, ki_all = kb.reshape(batch, seq, a_kv), vb.reshape(batch, seq, a_kv), ki.reshape(batch, seq, IDX_DIM)
            if past is not None:
                pk, pv, pki = past
                k_all = jnp.concatenate([pk.reshape(batch, -1, a_kv).astype(BF16), k_all], axis=1)
                v_all = jnp.concatenate([pv.reshape(batch, -1, a_kv).astype(BF16), v_all], axis=1)
                ki_all = jnp.concatenate([pki, ki_all], axis=1)
            kid_all = jnp.concatenate([ki_all, ki_all], axis=-1).astype(BF16)
            tq = min(DSA_Q_TILE, seq)
            tk = KEY_BLOCK
            lk_pad = -(-lk_new // tk) * tk
            topk = min(TOPK_MAX, lk_new // 4)
            o = _dsa_attention(q, qi, kiwi, _pad_keys(k_all, lk_pad), _pad_keys(v_all, lk_pad), _pad_keys(kid_all, lk_pad),
                               batch=batch, lq=seq, tq=tq, tk=tk, l_valid=lk_new, q_pos0=pos0, topk=topk)
            xf = _oproj(xf, o, w["a_w_out"], w["ln_g"], w["ln_b"], i, j, alpha=alpha, tm=tm)
        else:
            wb = w["b_w_in"]
            lam_init = 0.8 - 0.6 * math.exp(-0.3 * i)
            (q,) = _inproj(xf, wb[j, :, :b_qk], [tab64], [(0, b_qk, 0, DIFF_DIM ** -0.5 * LOG2_E, (BF16,))],
                           tm=tm, period_blocks=period)
            if seq % tm == 0:
                kt, kb = _inproj(xf, wb[j, :, b_qk:2 * b_qk], [tab64], [(0, b_qk, 0, 1.0, (HEAD_MAJOR_T, BF16))],
                                 tm=tm, period_blocks=period, seq=seq)
                k_rows = jnp.transpose(kt, (0, 4, 1, 2, 3))
            else:
                k, kb = _inproj(xf, wb[j, :, b_qk:2 * b_qk], [tab64], [(0, b_qk, 0, 1.0, (F32, BF16))],
                                tm=tm, period_blocks=period)
                k_rows = k.reshape(batch, seq, N_HEADS_B, 2, DIFF_DIM)
            v, vb = _inproj(xf, wb[j, :, 2 * b_qk:], [], [(0, b_qk, None, 1.0, (F32, BF16))],
                            tm=tm, period_blocks=period)
            rows.append((k_rows, v.reshape(batch, seq, N_HEADS_B, 2 * DIFF_DIM)))
            k_all, v_all = kb.reshape(batch, seq, b_qk), vb.reshape(batch, seq, b_qk)
            if past is not None:
                pk, pv = past
                k_all = jnp.concatenate([pk.reshape(batch, -1, b_qk).astype(BF16), k_all], axis=1)
                v_all = jnp.concatenate([pv.reshape(batch, -1, b_qk).astype(BF16), v_all], axis=1)
            tq = min(256, seq)
            tk = KEY_BLOCK
            lk_pad = -(-lk_new // tk) * tk
            o = _diff_attention(q, _pad_keys(k_all, lk_pad), _pad_keys(v_all, lk_pad),
                                w["b_lambda_q1"], w["b_lambda_k1"], w["b_lambda_q2"], w["b_lambda_k2"], w["b_subln"], j,
                                batch=batch, lq=seq, tq=tq, tk=tk, l_valid=lk_new, q_pos0=pos0, lam_init=lam_init)
            xf = _oproj(xf, o, w["b_w_out"], w["ln_g"], w["ln_b"], i, j, alpha=alpha, tm=tm)
        xf = _ffn(xf, w["ffn_w_gate"], w["ffn_w_up"], w["ffn_w_down"], w["ln_g"], w["ln_b"], i, 1, 2,
                  alpha=alpha, tm=tm, tf=tf)
        xf = _ple(xf, pf, w["ple_w_gate"], w["ple_b_gate"], w["ple_w_proj"], w["ln_g"], w["ln_b"], i,
                  alpha=alpha, tm=tm)
    return xf.reshape(batch, seq, d), rows


def kernel(x_prompt, x_sample, cache_l0_k, cache_l0_v, cache_l0_kidx, cache_l1_k, cache_l1_v, cache_l2_k, cache_l2_v, cache_l2_kidx, cache_l3_k, cache_l3_v, p_prompt, p_sample, ln_g, ln_b, ffn_w_gate, ffn_w_up, ffn_w_down, ple_w_gate, ple_b_gate, ple_w_proj, a_w_in, a_w_out, b_w_in, b_w_out, b_lambda_q1, b_lambda_k1, b_lambda_q2, b_lambda_k2, b_subln):
    depth, _, d = ln_g.shape
    a_cols = a_w_in.shape[-1]
    a_pad = -(-a_cols // LANES) * LANES - a_cols
    w = {
        "ln_g": ln_g.reshape(depth, 4, 1, d),
        "ln_b": ln_b.reshape(depth, 4, 1, d),
        "ffn_w_gate": ffn_w_gate.astype(BF16),
        "ffn_w_up": ffn_w_up.astype(BF16),
        "ffn_w_down": ffn_w_down.astype(BF16),
        "ple_w_gate": ple_w_gate.astype(BF16),
        "ple_b_gate": ple_b_gate.reshape(depth, 1, d),
        "ple_w_proj": ple_w_proj.astype(BF16),
        "a_w_in": jnp.pad(a_w_in, ((0, 0), (0, 0), (0, a_pad))).astype(BF16),
        "a_w_out": a_w_out.astype(BF16),
        "b_w_in": b_w_in.astype(BF16),
        "b_w_out": b_w_out.astype(BF16),
        "b_lambda_q1": b_lambda_q1.reshape(-1, 1, DIFF_DIM),
        "b_lambda_k1": b_lambda_k1.reshape(-1, 1, DIFF_DIM),
        "b_lambda_q2": b_lambda_q2.reshape(-1, 1, DIFF_DIM),
        "b_lambda_k2": b_lambda_k2.reshape(-1, 1, DIFF_DIM),
        "b_subln": b_subln.reshape(-1, 1, 2 * DIFF_DIM),
    }
    past_len = cache_l0_k.shape[1]
    pasts_s = [(cache_l0_k, cache_l0_v, cache_l0_kidx), (cache_l1_k, cache_l1_v),
               (cache_l2_k, cache_l2_v, cache_l2_kidx), (cache_l3_k, cache_l3_v)]
    n_p = x_prompt.shape[0] * x_prompt.shape[1]
    n_s = x_sample.shape[0] * x_sample.shape[1]
    y_p, rows_p = _trunk(x_prompt, p_prompt, 0, [None] * depth, w, tm=min(512, n_p))
    y_s, rows_s = _trunk(x_sample, p_sample, past_len, pasts_s, w, tm=min(256, n_s))
    out = [y_p, y_s]
    for rp, rs in zip(rows_p, rows_s):
        out.extend(rp)
        out.extend(rs)
    return tuple(out)
```

```python
import functools
import math

import numpy as np
import jax
import jax.numpy as jnp
from jax import lax
from jax.experimental import pallas as pl
from jax.experimental.pallas import tpu as pltpu

CHUNK = 64
CHUNK_SHIFT = 6
HEAD_DIM = 128
N_HEADS_A = 16
N_KV_HEADS_A = 4
N_IDX_HEADS = 16
IDX_DIM = 64
TOPK_MAX = 256
N_HEADS_B = 16
DIFF_DIM = 64
ROPE_FRACTION = 4
ROPE_THETA = 500000.0
LN_EPS = 1e-5

LANES = 128
KEY_BLOCK = 256
ROW_CHUNK = 128
DIFF_HEADS_PER_STEP = 8
DSA_Q_TILE = 256
ROW_GROUPS = 2
LOG2_E = 1.4426950408889634
VMEM_LIMIT_BYTES = 56 * 1024 * 1024
NEG_BIG = -1e30
INT_MIN = -(2 ** 31)

F32 = jnp.float32
BF16 = jnp.bfloat16
HEAD_MAJOR_T = "f32 (batch, head, map, 64, seq)"


def _cparams(sem):
    return pltpu.CompilerParams(dimension_semantics=sem, vmem_limit_bytes=VMEM_LIMIT_BYTES)


def _layer_norm(z, g, b):
    mu = jnp.mean(z, axis=-1, keepdims=True)
    zc = z - mu
    var = jnp.mean(zc * zc, axis=-1, keepdims=True)
    return zc * lax.rsqrt(var + LN_EPS) * g + b


def _dot(a, b):
    return jnp.dot(a, b, preferred_element_type=F32)


def _lane_tile(x, n):
    return x if n == 1 else jnp.concatenate([x] * n, axis=1)


def _dot_nt(a, b):
    return lax.dot_general(a, b, (((1,), (1,)), ((), ())), preferred_element_type=F32)


def _ffn_body(x_ref, wg_ref, wu_ref, wd_ref, g_ref, b_ref, o_ref, xb_ref, *, alpha, nf):
    f = pl.program_id(1)

    @pl.when(f == 0)
    def _():
        xb_ref[...] = x_ref[...].astype(BF16)
        o_ref[...] = jnp.zeros_like(o_ref)

    xb = xb_ref[...]
    gate = _dot(xb, wg_ref[...])
    up = _dot(xb, wu_ref[...])
    h = (gate * jax.nn.sigmoid(gate) * up).astype(BF16)
    o_ref[...] += _dot(h, wd_ref[...])

    @pl.when(f == nf - 1)
    def _():
        z = alpha * x_ref[...] + 0.5 * o_ref[...]
        o_ref[...] = _layer_norm(z, g_ref[...], b_ref[...])


def _ffn(x, wg, wu, wd, ln_g, ln_b, layer, sub, ln_idx, *, alpha, tm, tf):
    n, d = x.shape
    f_dim = wg.shape[-1]
    nf = f_dim // tf
    return pl.pallas_call(
        functools.partial(_ffn_body, alpha=alpha, nf=nf),
        out_shape=jax.ShapeDtypeStruct((n, d), F32),
        grid=(n // tm, nf),
        in_specs=[
            pl.BlockSpec((tm, d), lambda i, f: (i, 0)),
            pl.BlockSpec((None, None, d, tf), lambda i, f: (layer, sub, 0, f)),
            pl.BlockSpec((None, None, d, tf), lambda i, f: (layer, sub, 0, f)),
            pl.BlockSpec((None, None, tf, d), lambda i, f: (layer, sub, f, 0)),
            pl.BlockSpec((None, None, 1, d), lambda i, f: (layer, ln_idx, 0, 0)),
            pl.BlockSpec((None, None, 1, d), lambda i, f: (layer, ln_idx, 0, 0)),
        ],
        out_specs=pl.BlockSpec((tm, d), lambda i, f: (i, 0)),
        scratch_shapes=[pltpu.VMEM((tm, d), BF16)],
        compiler_params=_cparams(("parallel", "arbitrary")),
        name="ffn_postnorm",
    )(x, wg, wu, wd, ln_g, ln_b)


def _ple_body(x_ref, p_ref, wg_ref, bg_ref, wp_ref, g_ref, b_ref, o_ref, *, alpha):
    rows = x_ref.shape[0] // ROW_GROUPS if x_ref.shape[0] % (ROW_GROUPS * 8) == 0 else x_ref.shape[0]
    for r0 in range(0, x_ref.shape[0], rows):
        x = x_ref[r0:r0 + rows, :]
        gate = jax.nn.sigmoid(_dot(x.astype(BF16), wg_ref[...]) + bg_ref[...])
        proj = _dot(p_ref[r0:r0 + rows, :].astype(BF16), wp_ref[...])
        o_ref[r0:r0 + rows, :] = _layer_norm(alpha * x + gate * proj, g_ref[...], b_ref[...])


def _ple(x, p, wg, bg, wp, ln_g, ln_b, layer, *, alpha, tm):
    n, d = x.shape
    pd = p.shape[-1]
    return pl.pallas_call(
        functools.partial(_ple_body, alpha=alpha),
        out_shape=jax.ShapeDtypeStruct((n, d), F32),
        grid=(n // tm,),
        in_specs=[
            pl.BlockSpec((tm, d), lambda i: (i, 0)),
            pl.BlockSpec((None, tm, pd), lambda i: (layer, i, 0)),
            pl.BlockSpec((None, d, d), lambda i: (layer, 0, 0)),
            pl.BlockSpec((None, 1, d), lambda i: (layer, 0, 0)),
            pl.BlockSpec((None, pd, d), lambda i: (layer, 0, 0)),
            pl.BlockSpec((None, None, 1, d), lambda i: (layer, 3, 0, 0)),
            pl.BlockSpec((None, None, 1, d), lambda i: (layer, 3, 0, 0)),
        ],
        out_specs=pl.BlockSpec((tm, d), lambda i: (i, 0)),
        compiler_params=_cparams(("parallel",)),
        name="ple_postnorm",
    )(x, p, wg, bg, wp, ln_g, ln_b)


def _oproj_body(x_ref, a_ref, w_ref, g_ref, b_ref, o_ref, *, alpha):
    rows = x_ref.shape[0] // ROW_GROUPS if x_ref.shape[0] % (ROW_GROUPS * 8) == 0 else x_ref.shape[0]
    for r0 in range(0, x_ref.shape[0], rows):
        mix = _dot(a_ref[r0:r0 + rows, :], w_ref[...])
        o_ref[r0:r0 + rows, :] = _layer_norm(alpha * x_ref[r0:r0 + rows, :] + mix, g_ref[...], b_ref[...])


def _oproj(x, a, w, ln_g, ln_b, layer, j, *, alpha, tm):
    n, d = x.shape
    ad = a.shape[-1]
    return pl.pallas_call(
        functools.partial(_oproj_body, alpha=alpha),
        out_shape=jax.ShapeDtypeStruct((n, d), F32),
        grid=(n // tm,),
        in_specs=[
            pl.BlockSpec((tm, d), lambda i: (i, 0)),
            pl.BlockSpec((tm, ad), lambda i: (i, 0)),
            pl.BlockSpec((None, ad, d), lambda i: (j, 0, 0)),
            pl.BlockSpec((None, None, 1, d), lambda i: (layer, 1, 0, 0)),
            pl.BlockSpec((None, None, 1, d), lambda i: (layer, 1, 0, 0)),
        ],
        out_specs=pl.BlockSpec((tm, d), lambda i: (i, 0)),
        compiler_params=_cparams(("parallel",)),
        name="oproj_postnorm",
    )(x, a, w, ln_g, ln_b)


def _rope_tables(positions, head_dim, kind):
    rd = head_dim // ROPE_FRACTION
    half = rd // 2
    inv_freq = (np.float32(ROPE_THETA) ** (-(np.arange(half, dtype=np.float32) / np.float32(half)))).astype(np.float32)
    ang = positions.astype(np.float32)[:, None] * inv_freq[None, :]
    cos, sin = np.cos(ang).astype(np.float32), np.sin(ang).astype(np.float32)
    npos = positions.shape[0]
    c = np.ones((npos, LANES), np.float32)
    sa = np.zeros((npos, LANES), np.float32)
    sb = np.zeros((npos, LANES), np.float32)
    starts = range(0, LANES, head_dim) if kind == "head" else [0]
    for h0 in starts:
        c[:, h0:h0 + half] = cos
        c[:, h0 + half:h0 + rd] = cos
        sa[:, h0:h0 + half] = -sin
        sb[:, h0 + half:h0 + rd] = sin
    if kind == "kiwi":
        c[:, IDX_DIM:IDX_DIM + N_IDX_HEADS] = (N_IDX_HEADS * IDX_DIM) ** -0.5
    return jnp.asarray(c), jnp.asarray(sa), jnp.asarray(sb), half


def _inproj_body(*refs, segs, n_tab, col_step):
    x_ref, w_ref = refs[0], refs[1]
    tab_refs = refs[2:2 + 3 * n_tab]
    out_refs = refs[2 + 3 * n_tab:]
    xb = x_ref[...].astype(BF16)
    oi = 0
    for (c0, c1, tab, half, scale, dtypes) in segs:
        outs = out_refs[oi:oi + len(dtypes)]
        oi += len(dtypes)
        for j0 in range(c0, c1, col_step):
            wj = min(col_step, c1 - j0)
            y = _dot(xb, w_ref[:, j0:j0 + wj])
            for c in range(wj // LANES):
                yc = y[:, c * LANES:(c + 1) * LANES]
                if tab is not None:
                    cc, sa, sb = (r[...] for r in tab_refs[3 * tab:3 * tab + 3])
                    yc = (yc * cc + pltpu.roll(yc, LANES - half, 1) * sa + pltpu.roll(yc, half, 1) * sb)
                if scale != 1.0:
                    yc = yc * scale
                lo = j0 - c0 + c * LANES
                for o_ref, dt in zip(outs, dtypes):
                    if dt == HEAD_MAJOR_T:
                        yt = yc.T
                        o_ref[lo // LANES, 0] = yt[:DIFF_DIM]
                        o_ref[lo // LANES, 1] = yt[DIFF_DIM:]
                    else:
                        o_ref[:, lo:lo + LANES] = yc.astype(dt)


def _inproj(x, w, tables, segs, *, tm, period_blocks, seq=None):
    n, d = x.shape
    ctot = w.shape[1]
    segs_full = tuple((c0, c1, tab, (tables[tab][3] if tab is not None else 0), scale, tuple(dts))
                      for (c0, c1, tab, scale, dts) in segs)
    in_specs = [pl.BlockSpec((tm, d), lambda i: (i, 0)), pl.BlockSpec((d, ctot), lambda i: (0, 0))]
    args = [x, w]
    for (c, sa, sb, _) in tables:
        for t in (c, sa, sb):
            in_specs.append(pl.BlockSpec((tm, LANES), lambda i: (i % period_blocks, 0)))
            args.append(t)
    out_shape, out_specs = [], []
    for (c0, c1, _, _, dts) in segs:
        for dt in dts:
            if dt == HEAD_MAJOR_T:
                heads = (c1 - c0) // (2 * DIFF_DIM)
                out_shape.append(jax.ShapeDtypeStruct((n // seq, heads, 2, DIFF_DIM, seq), F32))
                out_specs.append(pl.BlockSpec((None, heads, 2, DIFF_DIM, tm),
                                              lambda i: (i // (seq // tm), 0, 0, 0, i % (seq // tm))))
            else:
                out_shape.append(jax.ShapeDtypeStruct((n, c1 - c0), dt))
                out_specs.append(pl.BlockSpec((tm, c1 - c0), lambda i: (i, 0)))
    return pl.pallas_call(
        functools.partial(_inproj_body, segs=segs_full, n_tab=len(tables), col_step=512),
        out_shape=out_shape,
        grid=(n // tm,),
        in_specs=in_specs,
        out_specs=out_specs,
        compiler_params=_cparams(("parallel",)),
        name="inproj_rope",
    )(*args)


def _softmax_step(lhs_ref, m_ref, acc_ref, kblk, vext, bias_fn, *, rows, rc, tk):
    for r in range(rows // rc):
        r0, r1 = r * rc, (r + 1) * rc
        s = _dot_nt(lhs_ref[r0:r1, :], kblk)
        if bias_fn is not None:
            s = s + bias_fn(r)
        m_prev = m_ref[r0:r1, :]
        m_next = jnp.maximum(m_prev, jnp.max(s, axis=1, keepdims=True))
        a = jnp.exp2(m_prev - m_next)
        p = jnp.exp2(s - _lane_tile(m_next, tk // LANES))
        acc_ref[r0:r1, :] = acc_ref[r0:r1, :] * _lane_tile(a, 2) + _dot(p.astype(BF16), vext)
        m_ref[r0:r1, :] = m_next


def _sortable_key(score):
    score = jnp.where(score == 0.0, 0.0, score)
    bits = pltpu.bitcast(score, jnp.int32)
    return bits ^ (lax.shift_right_arithmetic(bits, 31) & 0x7FFFFFFF)


def _dsa_select_key_major(qi_ref, kiwi_ref, kid_ref, tril_ref, key_ref, bias_ref, qlhs_ref, wt_ref, *,
                          tq, tk, nkb, q_first, l_valid, topk):
    rc = LANES
    nrc = tq // rc
    lane128 = lax.broadcasted_iota(jnp.int32, (rc, LANES), 1)
    for rh in range(nrc):
        for c in range(N_IDX_HEADS // 2):
            qc = qi_ref[rh * rc:(rh + 1) * rc, c * LANES:(c + 1) * LANES]
            qlhs_ref[rh, c, :rc, :] = jnp.where(lane128 < IDX_DIM, qc, jnp.zeros_like(qc))
            qlhs_ref[rh, c, rc:, :] = jnp.where(lane128 >= IDX_DIM, qc, jnp.zeros_like(qc))
        kw_t = kiwi_ref[rh * rc:(rh + 1) * rc, :].T
        wt_ref[rh] = kw_t[IDX_DIM:IDX_DIM + N_IDX_HEADS, :]

    def idx_block(kb, carry):
        k0 = pl.multiple_of(kb * tk, tk)
        kblk = kid_ref[pl.ds(k0, tk), :]
        kpos = k0 + lax.broadcasted_iota(jnp.int32, (tk, rc), 0)
        k_ok = kpos < l_valid
        k_chunk = lax.shift_right_arithmetic(kpos, CHUNK_SHIFT)
        for rh in range(nrc):
            score = jnp.zeros((tk, rc), F32)
            for c in range(N_IDX_HEADS // 2):
                rel = jnp.maximum(_dot_nt(kblk, qlhs_ref[rh, c]), 0.0)
                score = (score + wt_ref[rh, 2 * c:2 * c + 1, :] * rel[:, :rc]
                         + wt_ref[rh, 2 * c + 1:2 * c + 2, :] * rel[:, rc:])
            q_chunk = lax.shift_right_arithmetic(
                q_first + rh * rc + lax.broadcasted_iota(jnp.int32, (tk, rc), 1), CHUNK_SHIFT)
            key_ref[kb, rh] = jnp.where((k_chunk <= q_chunk) & k_ok, _sortable_key(score), INT_MIN)
        return carry

    lax.fori_loop(0, nkb, idx_block, 0)
    key_ref[nkb] = jnp.full((nrc, tk, rc), INT_MIN, jnp.int32)

    n_acc = 4

    def count(preds):
        def cnt_pair(kp, accs):
            accs = [list(a) for a in accs]
            for kb in (2 * kp, 2 * kp + 1):
                for rh in range(nrc):
                    hit = jnp.where(preds[rh](key_ref[kb, rh]), 1.0, 0.0)
                    for j in range(tk // 8):
                        accs[rh][j % n_acc] = accs[rh][j % n_acc] + hit[j * 8:(j + 1) * 8, :]
            return tuple(tuple(a) for a in accs)

        zero = tuple(tuple(jnp.zeros((8, rc), F32) for _ in range(n_acc)) for _ in range(nrc))
        accs = lax.fori_loop(0, (nkb + 1) // 2, cnt_pair, zero)
        return [jnp.sum(sum(a[1:], a[0]), axis=0, keepdims=True) for a in accs]

    def bit_step(i, tus):
        bit = lax.shift_left(jnp.int32(1), 31 - i)
        cands_u = [tu | bit for tu in tus]
        totals = count([(lambda key, cs=cu ^ INT_MIN: key >= cs) for cu in cands_u])
        return tuple(jnp.where(tot >= topk, cu, tu) for tot, cu, tu in zip(totals, cands_u, tus))

    tus = lax.fori_loop(0, 32, bit_step, tuple(jnp.zeros((1, rc), jnp.int32) for _ in range(nrc)))
    thr = [tu ^ INT_MIN for tu in tus]
    n_gt = count([(lambda key, th=th: key > th) for th in thr])
    need = [topk - n for n in n_gt]

    def sel_block(kb, seen):
        new_seen = []
        for rh in range(nrc):
            key = key_ref[kb, rh]
            eq = key == thr[rh]
            eqf = jnp.where(eq, 1.0, 0.0)
            pref = _dot(tril_ref[...], eqf.astype(BF16)) + seen[rh]
            keep_tie = jnp.where(eq, jnp.where(pref <= need[rh], 1, 0), 0)
            sel = jnp.where(key > thr[rh], 1, keep_tie)
            sel = jnp.where(key == INT_MIN, 0, sel)
            bias_ref[kb, rh * rc:(rh + 1) * rc, :] = jnp.where(sel == 1, 0.0, NEG_BIG).T
            new_seen.append(pref[tk - 1:tk, :])
        return tuple(new_seen)

    lax.fori_loop(0, nkb, sel_block, tuple(jnp.zeros((1, rc), F32) for _ in range(nrc)))


def _dsa_body(q_ref, qi_ref, kiwi_ref, k_ref, v_ref, kid_ref, tri_ref, o_ref, key_ref, bias_ref,
              lhs_ref, m_ref, acc_ref, qlhs_ref, wb_ref, *, tq, tk, nq, l_valid, q_pos0, topk):
    t = pl.program_id(1)
    q_first = q_pos0 + t * tq
    q_last_chunk = (q_first + tq - 1) // CHUNK
    kmax = jnp.minimum(l_valid, (q_last_chunk + 1) * CHUNK)
    nkb = (kmax + tk - 1) // tk

    if tq % LANES == 0:
        _dsa_select_key_major(qi_ref, kiwi_ref, kid_ref, tri_ref, key_ref, bias_ref, qlhs_ref, wb_ref,
                              tq=tq, tk=tk, nkb=nkb, q_first=q_first, l_valid=l_valid, topk=topk)
    else:
        _dsa_select_query_major(qi_ref, kiwi_ref, kid_ref, tri_ref, key_ref, bias_ref, qlhs_ref, wb_ref,
                                tq=tq, tk=tk, nkb=nkb, q_first=q_first, l_valid=l_valid, topk=topk)
    _dsa_attend(q_ref, k_ref, v_ref, o_ref, bias_ref, lhs_ref, m_ref, acc_ref, tq=tq, tk=tk, nkb=nkb)


def _dsa_select_query_major(qi_ref, kiwi_ref, kid_ref, tri_ref, key_ref, bias_ref, qlhs_ref, wb_ref, *,
                            tq, tk, nkb, q_first, l_valid, topk):
    rc = min(ROW_CHUNK, tq)
    nrc = tq // rc
    nrep = tk // LANES

    lane128 = lax.broadcasted_iota(jnp.int32, (rc, LANES), 1)
    for rh in range(nrc):
        for c in range(N_IDX_HEADS // 2):
            qc = qi_ref[rh * rc:(rh + 1) * rc, c * LANES:(c + 1) * LANES]
            qlhs_ref[rh, c, :rc, :] = jnp.where(lane128 < IDX_DIM, qc, jnp.zeros_like(qc))
            qlhs_ref[rh, c, rc:, :] = jnp.where(lane128 >= IDX_DIM, qc, jnp.zeros_like(qc))
        wi = kiwi_ref[rh * rc:(rh + 1) * rc, IDX_DIM:IDX_DIM + N_IDX_HEADS]
        for h in range(N_IDX_HEADS):
            wb_ref[rh, h] = jnp.broadcast_to(wi[:, h:h + 1], (rc, LANES))

    def idx_block(kb, carry):
        k0 = pl.multiple_of(kb * tk, tk)
        kblk = kid_ref[pl.ds(k0, tk), :]
        kpos = k0 + lax.broadcasted_iota(jnp.int32, (rc, tk), 1)
        for rh in range(nrc):
            score = jnp.zeros((rc, tk), F32)
            for c in range(N_IDX_HEADS // 2):
                rel = jnp.maximum(_dot_nt(qlhs_ref[rh, c], kblk), 0.0)
                score = (score + _lane_tile(wb_ref[rh, 2 * c], nrep) * rel[:rc]
                         + _lane_tile(wb_ref[rh, 2 * c + 1], nrep) * rel[rc:])
            q_chunk = lax.shift_right_arithmetic(
                q_first + rh * rc + lax.broadcasted_iota(jnp.int32, (rc, tk), 0), CHUNK_SHIFT)
            adm = (lax.shift_right_arithmetic(kpos, CHUNK_SHIFT) <= q_chunk) & (kpos < l_valid)
            key_ref[kb, rh * rc:(rh + 1) * rc, :] = jnp.where(adm, _sortable_key(score), INT_MIN)
        return carry

    lax.fori_loop(0, nkb, idx_block, 0)
    key_ref[nkb] = jnp.full((tq, tk), INT_MIN, jnp.int32)

    def lane_counts(rh, pred):
        def cnt_pair(kp, cnt):
            for kb in (2 * kp, 2 * kp + 1):
                for c in range(nrep):
                    cnt = cnt + jnp.where(pred(key_ref[kb, rh * rc:(rh + 1) * rc, c * LANES:(c + 1) * LANES]), 1.0, 0.0)
            return cnt
        return lax.fori_loop(0, (nkb + 1) // 2, cnt_pair, jnp.zeros((rc, LANES), F32))

    def count(preds):
        partial = [lane_counts(rh, preds[rh]) for rh in range(nrc)]
        return [jnp.sum(c, axis=1, keepdims=True) for c in partial]

    def bit_step(i, tus):
        bit = lax.shift_left(jnp.int32(1), 31 - i)
        cands_u = [tu | bit for tu in tus]
        totals = count([(lambda key, cs=cu ^ INT_MIN: key >= cs) for cu in cands_u])
        return tuple(jnp.where(tot >= topk, cu, tu) for tot, cu, tu in zip(totals, cands_u, tus))

    tus = lax.fori_loop(0, 32, bit_step, tuple(jnp.zeros((rc, LANES), jnp.int32) for _ in range(nrc)))
    thr128 = [tu ^ INT_MIN for tu in tus]
    n_gt = count([(lambda key, th=th: key > th) for th in thr128])
    thr = thr128[0][:, :1] if nrc == 1 else jnp.concatenate([th[:, :1] for th in thr128], axis=0)
    need = topk - (n_gt[0] if nrc == 1 else jnp.concatenate(n_gt, axis=0))

    def sel_block(kb, seen):
        key = key_ref[kb]
        eq = key == thr
        eqf = jnp.where(eq, 1.0, 0.0)
        pref = _dot(eqf.astype(BF16), tri_ref[...]) + seen
        keep_tie = jnp.where(eq, jnp.where(pref <= need, 1, 0), 0)
        sel = jnp.where(key > thr, 1, keep_tie)
        sel = jnp.where(key == INT_MIN, 0, sel)
        bias_ref[kb] = jnp.where(sel == 1, 0.0, NEG_BIG)
        return pref[:, tk - 1:tk]

    lax.fori_loop(0, nkb, sel_block, jnp.zeros((tq, 1), F32))


def _dsa_attend(q_ref, k_ref, v_ref, o_ref, bias_ref, lhs_ref, m_ref, acc_ref, *, tq, tk, nkb):
    rep = N_HEADS_A // N_KV_HEADS_A
    rc = min(ROW_CHUNK, tq)
    ones = jnp.ones((tk, LANES), BF16)
    for g in range(N_KV_HEADS_A):
        for r in range(rep):
            lhs_ref[g, r * tq:(r + 1) * tq, :] = q_ref[:, (g * rep + r) * HEAD_DIM:(g * rep + r + 1) * HEAD_DIM]
    m_ref[...] = jnp.full(m_ref.shape, NEG_BIG, F32)
    acc_ref[...] = jnp.zeros(acc_ref.shape, F32)

    def kv_block(kb, c):
        k0 = pl.multiple_of(kb * tk, tk)
        for g in range(N_KV_HEADS_A):
            kblk = k_ref[pl.ds(k0, tk), g * HEAD_DIM:(g + 1) * HEAD_DIM]
            vext = jnp.concatenate([v_ref[pl.ds(k0, tk), g * HEAD_DIM:(g + 1) * HEAD_DIM], ones], axis=1)
            _softmax_step(lhs_ref.at[g], m_ref.at[g], acc_ref.at[g], kblk, vext,
                          lambda r: bias_ref[kb, (r * rc) % tq:(r * rc) % tq + rc, :],
                          rows=rep * tq, rc=rc, tk=tk)
        return c

    lax.fori_loop(0, nkb, kv_block, 0)
    for g in range(N_KV_HEADS_A):
        acc = acc_ref[g]
        o = acc[:, :HEAD_DIM] / acc[:, HEAD_DIM:]
        for r in range(rep):
            h0 = (g * rep + r) * HEAD_DIM
            o_ref[:, h0:h0 + HEAD_DIM] = o[r * tq:(r + 1) * tq].astype(o_ref.dtype)


def _dsa_attention(q, qi, kiwi, k_all, v_all, kid_all, *, batch, lq, tq, tk, l_valid, q_pos0, topk):
    lk_pad = k_all.shape[1]
    nq = lq // tq
    nkb_max = lk_pad // tk
    a_q = N_HEADS_A * HEAD_DIM
    a_kv = N_KV_HEADS_A * HEAD_DIM
    rep = N_HEADS_A // N_KV_HEADS_A
    rc = min(ROW_CHUNK, tq)
    ones_tri = np.triu(np.ones((tk, tk), np.float32))
    if tq % LANES == 0:
        tri = jnp.asarray(ones_tri.T, BF16)
        key_scratch = pltpu.VMEM((nkb_max + 1, tq // LANES, tk, LANES), jnp.int32)
        w_scratch = pltpu.VMEM((tq // LANES, N_IDX_HEADS, LANES), F32)
    else:
        tri = jnp.asarray(ones_tri, BF16)
        key_scratch = pltpu.VMEM((nkb_max + 1, tq, tk), jnp.int32)
        w_scratch = pltpu.VMEM((tq // rc, N_IDX_HEADS, rc, LANES), F32)
    return pl.pallas_call(
        functools.partial(_dsa_body, tq=tq, tk=tk, nq=nq, l_valid=l_valid, q_pos0=q_pos0, topk=topk),
        out_shape=jax.ShapeDtypeStruct((batch * lq, a_q), BF16),
        grid=(batch, nq),
        in_specs=[
            pl.BlockSpec((tq, a_q), lambda b, t: (b * nq + t, 0)),
            pl.BlockSpec((tq, N_IDX_HEADS * IDX_DIM), lambda b, t: (b * nq + t, 0)),
            pl.BlockSpec((tq, LANES), lambda b, t: (b * nq + t, 0)),
            pl.BlockSpec((None, lk_pad, a_kv), lambda b, t: (b, 0, 0)),
            pl.BlockSpec((None, lk_pad, a_kv), lambda b, t: (b, 0, 0)),
            pl.BlockSpec((None, lk_pad, LANES), lambda b, t: (b, 0, 0)),
            pl.BlockSpec((tk, tk), lambda b, t: (0, 0)),
        ],
        out_specs=pl.BlockSpec((tq, a_q), lambda b, t: (b * nq + t, 0)),
        scratch_shapes=[key_scratch, pltpu.VMEM((nkb_max, tq, tk), F32),
                        pltpu.VMEM((N_KV_HEADS_A, rep * tq, HEAD_DIM), BF16),
                        pltpu.VMEM((N_KV_HEADS_A, rep * tq, LANES), F32),
                        pltpu.VMEM((N_KV_HEADS_A, rep * tq, 2 * HEAD_DIM), F32),
                        pltpu.VMEM((tq // rc, N_IDX_HEADS // 2, 2 * rc, LANES), BF16),
                        w_scratch],
        compiler_params=_cparams(("parallel", "arbitrary")),
        name="dsa_attention",
    )(q, qi, kiwi, k_all, v_all, kid_all, tri)


def _diff_body(q_ref, k_ref, v_ref, lq1_ref, lk1_ref, lq2_ref, lk2_ref, g_ref, o_ref, lhs_ref, m_ref, acc_ref, *,
               tq, tk, nq, l_valid, q_pos0, lam_init, rc, heads):
    lam = (jnp.exp(jnp.sum(lq1_ref[...] * lk1_ref[...], axis=1, keepdims=True))
           - jnp.exp(jnp.sum(lq2_ref[...] * lk2_ref[...], axis=1, keepdims=True)) + lam_init)
    lane = lax.broadcasted_iota(jnp.int32, (tq, LANES), 1)
    ones = jnp.ones((tk, LANES), BF16)
    hd = 2 * DIFF_DIM

    def tile(t, carry):
        q0 = pl.multiple_of(t * tq, tq)
        q_first = q_pos0 + t * tq
        for h in range(heads):
            qc = q_ref[pl.ds(q0, tq), h * hd:(h + 1) * hd]
            lhs_ref[h, :tq, :] = jnp.where(lane < DIFF_DIM, qc, jnp.zeros_like(qc))
            lhs_ref[h, tq:, :] = jnp.where(lane >= DIFF_DIM, qc, jnp.zeros_like(qc))
        m_ref[...] = jnp.full(m_ref.shape, NEG_BIG, F32)
        acc_ref[...] = jnp.zeros(acc_ref.shape, F32)
        nfull = jnp.minimum(((q_first // CHUNK + 1) * CHUNK) // tk, l_valid // tk)
        nkb = (jnp.minimum(l_valid, ((q_first + tq - 1) // CHUNK + 1) * CHUNK) + tk - 1) // tk

        def step(kb, c, masked):
            k0 = pl.multiple_of(kb * tk, tk)

            def bias_fn(r):
                row0 = q_first + (r * rc) % tq
                q_chunk = lax.shift_right_arithmetic(row0 + lax.broadcasted_iota(jnp.int32, (rc, tk), 0), CHUNK_SHIFT)
                kpos = k0 + lax.broadcasted_iota(jnp.int32, (rc, tk), 1)
                adm = (lax.shift_right_arithmetic(kpos, CHUNK_SHIFT) <= q_chunk) & (kpos < l_valid)
                return jnp.where(adm, 0.0, NEG_BIG)

            for h in range(heads):
                kblk = k_ref[pl.ds(k0, tk), h * hd:(h + 1) * hd]
                vext = jnp.concatenate([v_ref[pl.ds(k0, tk), h * hd:(h + 1) * hd], ones], axis=1)
                _softmax_step(lhs_ref.at[h], m_ref.at[h], acc_ref.at[h], kblk, vext, bias_fn if masked else None,
                              rows=2 * tq, rc=rc, tk=tk)
            return c

        lax.fori_loop(0, nfull, functools.partial(step, masked=False), 0)
        lax.fori_loop(nfull, nkb, functools.partial(step, masked=True), 0)
        for h in range(heads):
            acc = acc_ref[h]
            o = acc[:, :hd] / acc[:, hd:]
            o = o[:tq] - lam * o[tq:]
            o = o * lax.rsqrt(jnp.mean(o * o, axis=-1, keepdims=True) + LN_EPS) * g_ref[...] * (1.0 - lam_init)
            o_ref[pl.ds(q0, tq), h * hd:(h + 1) * hd] = o.astype(o_ref.dtype)
        return carry

    lax.fori_loop(0, nq, tile, 0)


def _diff_attention(q, k_all, v_all, lq1, lk1, lq2, lk2, g, j, *, batch, lq, tq, tk, l_valid, q_pos0, lam_init):
    lk_pad = k_all.shape[1]
    nq = lq // tq
    hd = 2 * DIFF_DIM
    heads = DIFF_HEADS_PER_STEP
    rc = min(ROW_CHUNK, 2 * tq)
    lam_spec = pl.BlockSpec((None, 1, DIFF_DIM), lambda b, h: (j, 0, 0))
    return pl.pallas_call(
        functools.partial(_diff_body, tq=tq, tk=tk, nq=nq, l_valid=l_valid, q_pos0=q_pos0, lam_init=lam_init,
                          rc=rc, heads=heads),
        out_shape=jax.ShapeDtypeStruct((batch * lq, N_HEADS_B * hd), BF16),
        grid=(batch, N_HEADS_B // heads),
        in_specs=[
            pl.BlockSpec((lq, heads * hd), lambda b, h: (b, h)),
            pl.BlockSpec((None, lk_pad, heads * hd), lambda b, h: (b, 0, h)),
            pl.BlockSpec((None, lk_pad, heads * hd), lambda b, h: (b, 0, h)),
            lam_spec, lam_spec, lam_spec, lam_spec,
            pl.BlockSpec((None, 1, hd), lambda b, h: (j, 0, 0)),
        ],
        out_specs=pl.BlockSpec((lq, heads * hd), lambda b, h: (b, h)),
        scratch_shapes=[pltpu.VMEM((heads, 2 * tq, hd), BF16), pltpu.VMEM((heads, 2 * tq, LANES), F32),
                        pltpu.VMEM((heads, 2 * tq, 2 * hd), F32)],
        compiler_params=_cparams(("parallel", "parallel")),
        name="diff_attention",
    )(q, k_all, v_all, lq1, lk1, lq2, lk2, g)


def _pad_keys(x, lk_pad):
    pad = lk_pad - x.shape[1]
    return x if pad == 0 else jnp.pad(x, ((0, 0), (0, pad), (0, 0)))


def _trunk(x, p, pos0, pasts, w, *, tm):
    batch, seq, d = x.shape
    n = batch * seq
    depth = w["ln_g"].shape[0]
    alpha = float((2 * depth) ** 0.25)
    a_q = N_HEADS_A * HEAD_DIM
    a_kv = N_KV_HEADS_A * HEAD_DIM
    a_qi = N_IDX_HEADS * IDX_DIM
    b_qk = N_HEADS_B * 2 * DIFF_DIM

    period = max(1, seq // tm)
    table_rows = max(seq, tm)
    positions = (pos0 + np.arange(seq)).astype(np.int64)
    positions = np.tile(positions, table_rows // seq)
    tab128 = _rope_tables(positions, HEAD_DIM, "head")
    tab64 = _rope_tables(positions, IDX_DIM, "head")
    tabkw = _rope_tables(positions, IDX_DIM, "kiwi")

    xf = x.reshape(n, d)
    pf = p.reshape(depth, n, p.shape[-1])
    tf = 512 if w["ffn_w_gate"].shape[-1] % 512 == 0 else 128
    rows = []
    for i in range(depth):
        j = i // 2
        xf = _ffn(xf, w["ffn_w_gate"], w["ffn_w_up"], w["ffn_w_down"], w["ln_g"], w["ln_b"], i, 0, 0,
                  alpha=alpha, tm=tm, tf=tf)
        past = pasts[i]
        lk_new = seq if past is None else past[0].shape[1] + seq
        if i % 2 == 0:
            wa = w["a_w_in"]
            (q,) = _inproj(xf, wa[j, :, :a_q], [tab128], [(0, a_q, 0, HEAD_DIM ** -0.5 * LOG2_E, (BF16,))],
                           tm=tm, period_blocks=period)
            k, kb, v, vb = _inproj(xf, wa[j, :, a_q:a_q + 2 * a_kv], [tab128],
                                   [(0, a_kv, 0, 1.0, (F32, BF16)), (a_kv, 2 * a_kv, None, 1.0, (F32, BF16))],
                                   tm=tm, period_blocks=period)
            qi, kiwi = _inproj(xf, wa[j, :, a_q + 2 * a_kv:], [tab64, tabkw],
                               [(0, a_qi, 0, 1.0, (BF16,)), (a_qi, a_qi + LANES, 1, 1.0, (F32,))],
                               tm=tm, period_blocks=period)
            ki = kiwi[:, :IDX_DIM]
            rows.append((k.reshape(batch, seq, N_KV_HEADS_A, HEAD_DIM), v.reshape(batch, seq, N_KV_HEADS_A, HEAD_DIM),
                         ki.reshape(batch, seq, IDX_DIM)))
            k_all, v_all, ki_all = kb.reshape(batch, seq, a_kv), vb.reshape(batch, seq, a_kv), ki.reshape(batch, seq, IDX_DIM)
            if past is not None:
                pk, pv, pki = past
                k_all = jnp.concatenate([pk.reshape(batch, -1, a_kv).astype(BF16), k_all], axis=1)
                v_all = jnp.concatenate([pv.reshape(batch, -1, a_kv).astype(BF16), v_all], axis=1)
                ki_all = jnp.concatenate([pki, ki_all], axis=1)
            kid_all = jnp.concatenate([ki_all, ki_all], axis=-1).astype(BF16)
            tq = min(DSA_Q_TILE, seq)
            tk = KEY_BLOCK
            lk_pad = -(-lk_new // tk) * tk
            topk = min(TOPK_MAX, lk_new // 4)
            o = _dsa_attention(q, qi, kiwi, _pad_keys(k_all, lk_pad), _pad_keys(v_all, lk_pad), _pad_keys(kid_all, lk_pad),
                               batch=batch, lq=seq, tq=tq, tk=tk, l_valid=lk_new, q_pos0=pos0, topk=topk)
            xf = _oproj(xf, o, w["a_w_out"], w["ln_g"], w["ln_b"], i, j, alpha=alpha, tm=tm)
        else:
            wb = w["b_w_in"]
            lam_init = 0.8 - 0.6 * math.exp(-0.3 * i)
            (q,) = _inproj(xf, wb[j, :, :b_qk], [tab64], [(0, b_qk, 0, DIFF_DIM ** -0.5 * LOG2_E, (BF16,))],
                           tm=tm, period_blocks=period)
            if seq % tm == 0:
                kt, kb = _inproj(xf, wb[j, :, b_qk:2 * b_qk], [tab64], [(0, b_qk, 0, 1.0, (HEAD_MAJOR_T, BF16))],
                                 tm=tm, period_blocks=period, seq=seq)
                k_rows = jnp.transpose(kt, (0, 4, 1, 2, 3))
            else:
                k, kb = _inproj(xf, wb[j, :, b_qk:2 * b_qk], [tab64], [(0, b_qk, 0, 1.0, (F32, BF16))],
                                tm=tm, period_blocks=period)
                k_rows = k.reshape(batch, seq, N_HEADS_B, 2, DIFF_DIM)
            v, vb = _inproj(xf, wb[j, :, 2 * b_qk:], [], [(0, b_qk, None, 1.0, (F32, BF16))],
                            tm=tm, period_blocks=period)
            rows.append((k_rows, v.reshape(batch, seq, N_HEADS_B, 2 * DIFF_DIM)))
            k_all, v_all = kb.reshape(batch, seq, b_qk), vb.reshape(batch, seq, b_qk)
            if past is not None:
                pk, pv = past
                k_all = jnp.concatenate([pk.reshape(batch, -1, b_qk).astype(BF16), k_all], axis=1)
                v_all = jnp.concatenate([pv.reshape(batch, -1, b_qk).astype(BF16), v_all], axis=1)
            tq = min(256, seq)
            tk = KEY_BLOCK
            lk_pad = -(-lk_new // tk) * tk
            o = _diff_attention(q, _pad_keys(k_all, lk_pad), _pad_keys(v_all, lk_pad),
                                w["b_lambda_q1"], w["b_lambda_k1"], w["b_lambda_q2"], w["b_lambda_k2"], w["b_subln"], j,
                                batch=batch, lq=seq, tq=tq, tk=tk, l_valid=lk_new, q_pos0=pos0, lam_init=lam_init)
            xf = _oproj(xf, o, w["b_w_out"], w["ln_g"], w["ln_b"], i, j, alpha=alpha, tm=tm)
        xf = _ffn(xf, w["ffn_w_gate"], w["ffn_w_up"], w["ffn_w_down"], w["ln_g"], w["ln_b"], i, 1, 2,
                  alpha=alpha, tm=tm, tf=tf)
        xf = _ple(xf, pf, w["ple_w_gate"], w["ple_b_gate"], w["ple_w_proj"], w["ln_g"], w["ln_b"], i,
                  alpha=alpha, tm=tm)
    return xf.reshape(batch, seq, d), rows


def kernel(x_prompt, x_sample, cache_l0_k, cache_l0_v, cache_l0_kidx, cache_l1_k, cache_l1_v, cache_l2_k, cache_l2_v, cache_l2_kidx, cache_l3_k, cache_l3_v, p_prompt, p_sample, ln_g, ln_b, ffn_w_gate, ffn_w_up, ffn_w_down, ple_w_gate, ple_b_gate, ple_w_proj, a_w_in, a_w_out, b_w_in, b_w_out, b_lambda_q1, b_lambda_k1, b_lambda_q2, b_lambda_k2, b_subln):
    depth, _, d = ln_g.shape
    a_cols = a_w_in.shape[-1]
    a_pad = -(-a_cols // LANES) * LANES - a_cols
    w = {
        "ln_g": ln_g.reshape(depth, 4, 1, d),
        "ln_b": ln_b.reshape(depth, 4, 1, d),
        "ffn_w_gate": ffn_w_gate.astype(BF16),
        "ffn_w_up": ffn_w_up.astype(BF16),
        "ffn_w_down": ffn_w_down.astype(BF16),
        "ple_w_gate": ple_w_gate.astype(BF16),
        "ple_b_gate": ple_b_gate.reshape(depth, 1, d),
        "ple_w_proj": ple_w_proj.astype(BF16),
        "a_w_in": jnp.pad(a_w_in, ((0, 0), (0, 0), (0, a_pad))).astype(BF16),
        "a_w_out": a_w_out.astype(BF16),
        "b_w_in": b_w_in.astype(BF16),
        "b_w_out": b_w_out.astype(BF16),
        "b_lambda_q1": b_lambda_q1.reshape(-1, 1, DIFF_DIM),
        "b_lambda_k1": b_lambda_k1.reshape(-1, 1, DIFF_DIM),
        "b_lambda_q2": b_lambda_q2.reshape(-1, 1, DIFF_DIM),
        "b_lambda_k2": b_lambda_k2.reshape(-1, 1, DIFF_DIM),
        "b_subln": b_subln.reshape(-1, 1, 2 * DIFF_DIM),
    }
    past_len = cache_l0_k.shape[1]
    pasts_s = [(cache_l0_k, cache_l0_v, cache_l0_kidx), (cache_l1_k, cache_l1_v),
               (cache_l2_k, cache_l2_v, cache_l2_kidx), (cache_l3_k, cache_l3_v)]
    n_p = x_prompt.shape[0] * x_prompt.shape[1]
    n_s = x_sample.shape[0] * x_sample.shape[1]
    y_p, rows_p = _trunk(x_prompt, p_prompt, 0, [None] * depth, w, tm=min(512, n_p))
    y_s, rows_s = _trunk(x_sample, p_sample, past_len, pasts_s, w, tm=min(256, n_s))
    out = [y_p, y_s]
    for rp, rs in zip(rows_p, rows_s):
        out.extend(rp)
        out.extend(rs)
    return tuple(out)
```

```python
import functools
import math

import numpy as np
import jax
import jax.numpy as jnp
from jax import lax
from jax.experimental import pallas as pl
from jax.experimental.pallas import tpu as pltpu

CHUNK = 64
CHUNK_SHIFT = 6
HEAD_DIM = 128
N_HEADS_A = 16
N_KV_HEADS_A = 4
N_IDX_HEADS = 16
IDX_DIM = 64
TOPK_MAX = 256
N_HEADS_B = 16
DIFF_DIM = 64
ROPE_FRACTION = 4
ROPE_THETA = 500000.0
LN_EPS = 1e-5

LANES = 128
KEY_BLOCK = 256
ROW_CHUNK = 128
DIFF_HEADS_PER_STEP = 8
DSA_Q_TILE = 256
ROW_GROUPS = 2
LOG2_E = 1.4426950408889634
VMEM_LIMIT_BYTES = 56 * 1024 * 1024
NEG_BIG = -1e30
INT_MIN = -(2 ** 31)

F32 = jnp.float32
BF16 = jnp.bfloat16
HEAD_MAJOR_T = "f32 (batch, head, map, 64, seq)"
ROW_INTERLEAVED = "f32 (rows * chunks, 128)"
KEY_T = "f32 (batch, 64, seq)"
KEY_TWICE = "bf16 (rows, 128) = [ki | ki]"


def _cparams(sem):
    return pltpu.CompilerParams(dimension_semantics=sem, vmem_limit_bytes=VMEM_LIMIT_BYTES)


def _layer_norm(z, g, b):
    mu = jnp.mean(z, axis=-1, keepdims=True)
    zc = z - mu
    var = jnp.mean(zc * zc, axis=-1, keepdims=True)
    return zc * lax.rsqrt(var + LN_EPS) * g + b


def _dot(a, b):
    return jnp.dot(a, b, preferred_element_type=F32)


def _lane_tile(x, n):
    return x if n == 1 else jnp.concatenate([x] * n, axis=1)


def _dot_nt(a, b):
    return lax.dot_general(a, b, (((1,), (1,)), ((), ())), preferred_element_type=F32)


def _ffn_body(x_ref, wg_ref, wu_ref, wd_ref, g_ref, b_ref, o_ref, xb_ref, *, alpha, nf):
    f = pl.program_id(1)

    @pl.when(f == 0)
    def _():
        xb_ref[...] = x_ref[...].astype(BF16)
        o_ref[...] = jnp.zeros_like(o_ref)

    xb = xb_ref[...]
    gate = _dot(xb, wg_ref[...])
    up = _dot(xb, wu_ref[...])
    h = (gate * jax.nn.sigmoid(gate) * up).astype(BF16)
    o_ref[...] += _dot(h, wd_ref[...])

    @pl.when(f == nf - 1)
    def _():
        z = alpha * x_ref[...] + 0.5 * o_ref[...]
        o_ref[...] = _layer_norm(z, g_ref[...], b_ref[...])


def _ffn(x, wg, wu, wd, ln_g, ln_b, layer, sub, ln_idx, *, alpha, tm, tf):
    n, d = x.shape
    f_dim = wg.shape[-1]
    nf = f_dim // tf
    return pl.pallas_call(
        functools.partial(_ffn_body, alpha=alpha, nf=nf),
        out_shape=jax.ShapeDtypeStruct((n, d), F32),
        grid=(n // tm, nf),
        in_specs=[
            pl.BlockSpec((tm, d), lambda i, f: (i, 0)),
            pl.BlockSpec((None, None, d, tf), lambda i, f: (layer, sub, 0, f)),
            pl.BlockSpec((None, None, d, tf), lambda i, f: (layer, sub, 0, f)),
            pl.BlockSpec((None, None, tf, d), lambda i, f: (layer, sub, f, 0)),
            pl.BlockSpec((None, None, 1, d), lambda i, f: (layer, ln_idx, 0, 0)),
            pl.BlockSpec((None, None, 1, d), lambda i, f: (layer, ln_idx, 0, 0)),
        ],
        out_specs=pl.BlockSpec((tm, d), lambda i, f: (i, 0)),
        scratch_shapes=[pltpu.VMEM((tm, d), BF16)],
        compiler_params=_cparams(("parallel", "arbitrary")),
        name="ffn_postnorm",
    )(x, wg, wu, wd, ln_g, ln_b)


def _ple_body(x_ref, p_ref, wg_ref, bg_ref, wp_ref, g_ref, b_ref, o_ref, *, alpha):
    rows = x_ref.shape[0] // ROW_GROUPS if x_ref.shape[0] % (ROW_GROUPS * 8) == 0 else x_ref.shape[0]
    for r0 in range(0, x_ref.shape[0], rows):
        x = x_ref[r0:r0 + rows, :]
        gate = jax.nn.sigmoid(_dot(x.astype(BF16), wg_ref[...]) + bg_ref[...])
        proj = _dot(p_ref[r0:r0 + rows, :].astype(BF16), wp_ref[...])
        o_ref[r0:r0 + rows, :] = _layer_norm(alpha * x + gate * proj, g_ref[...], b_ref[...])


def _ple(x, p, wg, bg, wp, ln_g, ln_b, layer, *, alpha, tm):
    n, d = x.shape
    pd = p.shape[-1]
    return pl.pallas_call(
        functools.partial(_ple_body, alpha=alpha),
        out_shape=jax.ShapeDtypeStruct((n, d), F32),
        grid=(n // tm,),
        in_specs=[
            pl.BlockSpec((tm, d), lambda i: (i, 0)),
            pl.BlockSpec((None, tm, pd), lambda i: (layer, i, 0)),
            pl.BlockSpec((None, d, d), lambda i: (layer, 0, 0)),
            pl.BlockSpec((None, 1, d), lambda i: (layer, 0, 0)),
            pl.BlockSpec((None, pd, d), lambda i: (layer, 0, 0)),
            pl.BlockSpec((None, None, 1, d), lambda i: (layer, 3, 0, 0)),
            pl.BlockSpec((None, None, 1, d), lambda i: (layer, 3, 0, 0)),
        ],
        out_specs=pl.BlockSpec((tm, d), lambda i: (i, 0)),
        compiler_params=_cparams(("parallel",)),
        name="ple_postnorm",
    )(x, p, wg, bg, wp, ln_g, ln_b)


def _oproj_body(x_ref, a_ref, w_ref, g_ref, b_ref, o_ref, *, alpha):
    rows = x_ref.shape[0] // ROW_GROUPS if x_ref.shape[0] % (ROW_GROUPS * 8) == 0 else x_ref.shape[0]
    for r0 in range(0, x_ref.shape[0], rows):
        mix = _dot(a_ref[r0:r0 + rows, :], w_ref[...])
        o_ref[r0:r0 + rows, :] = _layer_norm(alpha * x_ref[r0:r0 + rows, :] + mix, g_ref[...], b_ref[...])


def _oproj(x, a, w, ln_g, ln_b, layer, j, *, alpha, tm):
    n, d = x.shape
    ad = a.shape[-1]
    return pl.pallas_call(
        functools.partial(_oproj_body, alpha=alpha),
        out_shape=jax.ShapeDtypeStruct((n, d), F32),
        grid=(n // tm,),
        in_specs=[
            pl.BlockSpec((tm, d), lambda i: (i, 0)),
            pl.BlockSpec((tm, ad), lambda i: (i, 0)),
            pl.BlockSpec((None, ad, d), lambda i: (j, 0, 0)),
            pl.BlockSpec((None, None, 1, d), lambda i: (layer, 1, 0, 0)),
            pl.BlockSpec((None, None, 1, d), lambda i: (layer, 1, 0, 0)),
        ],
        out_specs=pl.BlockSpec((tm, d), lambda i: (i, 0)),
        compiler_params=_cparams(("parallel",)),
        name="oproj_postnorm",
    )(x, a, w, ln_g, ln_b)


def _rope_tables(positions, head_dim, kind):
    rd = head_dim // ROPE_FRACTION
    half = rd // 2
    inv_freq = (np.float32(ROPE_THETA) ** (-(np.arange(half, dtype=np.float32) / np.float32(half)))).astype(np.float32)
    ang = positions.astype(np.float32)[:, None] * inv_freq[None, :]
    cos, sin = np.cos(ang).astype(np.float32), np.sin(ang).astype(np.float32)
    npos = positions.shape[0]
    c = np.ones((npos, LANES), np.float32)
    sa = np.zeros((npos, LANES), np.float32)
    sb = np.zeros((npos, LANES), np.float32)
    starts = range(0, LANES, head_dim) if kind == "head" else [0]
    for h0 in starts:
        c[:, h0:h0 + half] = cos
        c[:, h0 + half:h0 + rd] = cos
        sa[:, h0:h0 + half] = -sin
        sb[:, h0 + half:h0 + rd] = sin
    if kind == "kiwi":
        c[:, IDX_DIM:IDX_DIM + N_IDX_HEADS] = (N_IDX_HEADS * IDX_DIM) ** -0.5
    return jnp.asarray(c), jnp.asarray(sa), jnp.asarray(sb), half


def _inproj_body(*refs, segs, n_tab, col_step):
    x_ref, w_ref = refs[0], refs[1]
    tab_refs = refs[2:2 + 3 * n_tab]
    out_refs = refs[2 + 3 * n_tab:]
    xb = x_ref[...].astype(BF16)
    oi = 0
    for (c0, c1, tab, half, scale, dtypes) in segs:
        outs = out_refs[oi:oi + len(dtypes)]
        oi += len(dtypes)
        for j0 in range(c0, c1, col_step):
            wj = min(col_step, c1 - j0)
            y = _dot(xb, w_ref[:, j0:j0 + wj])
            for c in range(wj // LANES):
                yc = y[:, c * LANES:(c + 1) * LANES]
                if tab is not None:
                    cc, sa, sb = (r[...] for r in tab_refs[3 * tab:3 * tab + 3])
                    yc = (yc * cc + pltpu.roll(yc, LANES - half, 1) * sa + pltpu.roll(yc, half, 1) * sb)
                if scale != 1.0:
                    yc = yc * scale
                lo = j0 - c0 + c * LANES
                for o_ref, dt in zip(outs, dtypes):
                    if dt == HEAD_MAJOR_T:
                        yt = yc.T
                        o_ref[lo // LANES, 0] = yt[:DIFF_DIM]
                        o_ref[lo // LANES, 1] = yt[DIFF_DIM:]
                    elif dt == ROW_INTERLEAVED:
                        nchunk = (c1 - c0) // LANES
                        o_ref[pl.ds(lo // LANES, yc.shape[0], stride=nchunk), :] = yc
                    elif dt == KEY_T:
                        o_ref[...] = yc.T[:IDX_DIM]
                    elif dt == KEY_TWICE:
                        lane = lax.broadcasted_iota(jnp.int32, yc.shape, 1)
                        o_ref[...] = jnp.where(lane < IDX_DIM, yc, pltpu.roll(yc, IDX_DIM, 1)).astype(BF16)
                    else:
                        o_ref[:, lo:lo + LANES] = yc.astype(dt)


def _inproj(x, w, tables, segs, *, tm, period_blocks, seq=None):
    n, d = x.shape
    ctot = w.shape[1]
    segs_full = tuple((c0, c1, tab, (tables[tab][3] if tab is not None else 0), scale, tuple(dts))
                      for (c0, c1, tab, scale, dts) in segs)
    in_specs = [pl.BlockSpec((tm, d), lambda i: (i, 0)), pl.BlockSpec((d, ctot), lambda i: (0, 0))]
    args = [x, w]
    for (c, sa, sb, _) in tables:
        for t in (c, sa, sb):
            in_specs.append(pl.BlockSpec((tm, LANES), lambda i: (i % period_blocks, 0)))
            args.append(t)
    out_shape, out_specs = [], []
    for (c0, c1, _, _, dts) in segs:
        for dt in dts:
            if dt == HEAD_MAJOR_T:
                heads = (c1 - c0) // (2 * DIFF_DIM)
                out_shape.append(jax.ShapeDtypeStruct((n // seq, heads, 2, DIFF_DIM, seq), F32))
                out_specs.append(pl.BlockSpec((None, heads, 2, DIFF_DIM, tm),
                                              lambda i: (i // (seq // tm), 0, 0, 0, i % (seq // tm))))
            elif dt == ROW_INTERLEAVED:
                nchunk = (c1 - c0) // LANES
                out_shape.append(jax.ShapeDtypeStruct((n * nchunk, LANES), F32))
                out_specs.append(pl.BlockSpec((tm * nchunk, LANES), lambda i: (i, 0)))
            elif dt == KEY_T:
                out_shape.append(jax.ShapeDtypeStruct((n // seq, IDX_DIM, seq), F32))
                out_specs.append(pl.BlockSpec((None, IDX_DIM, tm), lambda i: (i // (seq // tm), 0, i % (seq // tm))))
            elif dt == KEY_TWICE:
                out_shape.append(jax.ShapeDtypeStruct((n, LANES), BF16))
                out_specs.append(pl.BlockSpec((tm, LANES), lambda i: (i, 0)))
            else:
                out_shape.append(jax.ShapeDtypeStruct((n, c1 - c0), dt))
                out_specs.append(pl.BlockSpec((tm, c1 - c0), lambda i: (i, 0)))
    return pl.pallas_call(
        functools.partial(_inproj_body, segs=segs_full, n_tab=len(tables), col_step=512),
        out_shape=out_shape,
        grid=(n // tm,),
        in_specs=in_specs,
        out_specs=out_specs,
        compiler_params=_cparams(("parallel",)),
        name="inproj_rope",
    )(*args)


def _softmax_step(lhs_ref, m_ref, acc_ref, kblk, vext, bias_fn, *, rows, rc, tk):
    for r in range(rows // rc):
        r0, r1 = r * rc, (r + 1) * rc
        s = _dot_nt(lhs_ref[r0:r1, :], kblk)
        if bias_fn is not None:
            s = s + bias_fn(r)
        m_prev = m_ref[r0:r1, :]
        m_next = jnp.maximum(m_prev, jnp.max(s, axis=1, keepdims=True))
        a = jnp.exp2(m_prev - m_next)
        p = jnp.exp2(s - _lane_tile(m_next, tk // LANES))
        acc_ref[r0:r1, :] = acc_ref[r0:r1, :] * _lane_tile(a, 2) + _dot(p.astype(BF16), vext)
        m_ref[r0:r1, :] = m_next


def _sortable_key(score):
    score = jnp.where(score == 0.0, 0.0, score)
    bits = pltpu.bitcast(score, jnp.int32)
    return bits ^ (lax.shift_right_arithmetic(bits, 31) & 0x7FFFFFFF)


def _dsa_select_key_major(qi_ref, kiwi_ref, kid_ref, tril_ref, key_ref, bias_ref, qlhs_ref, wt_ref, *,
                          tq, tk, nkb, q_first, l_valid, topk):
    rc = LANES
    nrc = tq // rc
    lane128 = lax.broadcasted_iota(jnp.int32, (rc, LANES), 1)
    for rh in range(nrc):
        for c in range(N_IDX_HEADS // 2):
            qc = qi_ref[rh * rc:(rh + 1) * rc, c * LANES:(c + 1) * LANES]
            qlhs_ref[rh, c, :rc, :] = jnp.where(lane128 < IDX_DIM, qc, jnp.zeros_like(qc))
            qlhs_ref[rh, c, rc:, :] = jnp.where(lane128 >= IDX_DIM, qc, jnp.zeros_like(qc))
        kw_t = kiwi_ref[rh * rc:(rh + 1) * rc, :].T
        wt_ref[rh] = kw_t[IDX_DIM:IDX_DIM + N_IDX_HEADS, :]

    def idx_block(kb, carry):
        k0 = pl.multiple_of(kb * tk, tk)
        kblk = kid_ref[pl.ds(k0, tk), :]
        kpos = k0 + lax.broadcasted_iota(jnp.int32, (tk, rc), 0)
        k_ok = kpos < l_valid
        k_chunk = lax.shift_right_arithmetic(kpos, CHUNK_SHIFT)
        for rh in range(nrc):
            score = jnp.zeros((tk, rc), F32)
            for c in range(N_IDX_HEADS // 2):
                rel = jnp.maximum(_dot_nt(kblk, qlhs_ref[rh, c]), 0.0)
                score = (score + wt_ref[rh, 2 * c:2 * c + 1, :] * rel[:, :rc]
                         + wt_ref[rh, 2 * c + 1:2 * c + 2, :] * rel[:, rc:])
            q_chunk = lax.shift_right_arithmetic(
                q_first + rh * rc + lax.broadcasted_iota(jnp.int32, (tk, rc), 1), CHUNK_SHIFT)
            key_ref[kb, rh] = jnp.where((k_chunk <= q_chunk) & k_ok, _sortable_key(score), INT_MIN)
        return carry

    lax.fori_loop(0, nkb, idx_block, 0)
    key_ref[nkb] = jnp.full((nrc, tk, rc), INT_MIN, jnp.int32)

    n_acc = 4

    def count(preds):
        def cnt_pair(kp, accs):
            accs = [list(a) for a in accs]
            for kb in (2 * kp, 2 * kp + 1):
                for rh in range(nrc):
                    hit = jnp.where(preds[rh](key_ref[kb, rh]), 1.0, 0.0)
                    for j in range(tk // 8):
                        accs[rh][j % n_acc] = accs[rh][j % n_acc] + hit[j * 8:(j + 1) * 8, :]
            return tuple(tuple(a) for a in accs)

        zero = tuple(tuple(jnp.zeros((8, rc), F32) for _ in range(n_acc)) for _ in range(nrc))
        accs = lax.fori_loop(0, (nkb + 1) // 2, cnt_pair, zero)
        return [jnp.sum(sum(a[1:], a[0]), axis=0, keepdims=True) for a in accs]

    def bit_step(i, tus):
        bit = lax.shift_left(jnp.int32(1), 31 - i)
        cands_u = [tu | bit for tu in tus]
        totals = count([(lambda key, cs=cu ^ INT_MIN: key >= cs) for cu in cands_u])
        return tuple(jnp.where(tot >= topk, cu, tu) for tot, cu, tu in zip(totals, cands_u, tus))

    tus = lax.fori_loop(0, 32, bit_step, tuple(jnp.zeros((1, rc), jnp.int32) for _ in range(nrc)))
    thr = [tu ^ INT_MIN for tu in tus]
    n_gt = count([(lambda key, th=th: key > th) for th in thr])
    need = [topk - n for n in n_gt]

    def sel_block(kb, seen):
        new_seen = []
        for rh in range(nrc):
            key = key_ref[kb, rh]
            eq = key == thr[rh]
            eqf = jnp.where(eq, 1.0, 0.0)
            pref = _dot(tril_ref[...], eqf.astype(BF16)) + seen[rh]
            keep_tie = jnp.where(eq, jnp.where(pref <= need[rh], 1, 0), 0)
            sel = jnp.where(key > thr[rh], 1, keep_tie)
            sel = jnp.where(key == INT_MIN, 0, sel)
            bias_ref[kb, rh * rc:(rh + 1) * rc, :] = jnp.where(sel == 1, 0.0, NEG_BIG).T
            new_seen.append(pref[tk - 1:tk, :])
        return tuple(new_seen)

    lax.fori_loop(0, nkb, sel_block, tuple(jnp.zeros((1, rc), F32) for _ in range(nrc)))


def _dsa_body(q_ref, qi_ref, kiwi_ref, k_ref, v_ref, kid_ref, tri_ref, o_ref, key_ref, bias_ref,
              lhs_ref, m_ref, acc_ref, qlhs_ref, wb_ref, *, tq, tk, nq, l_valid, q_pos0, topk):
    t = pl.program_id(1)
    q_first = q_pos0 + t * tq
    q_last_chunk = (q_first + tq - 1) // CHUNK
    kmax = jnp.minimum(l_valid, (q_last_chunk + 1) * CHUNK)
    nkb = (kmax + tk - 1) // tk

    if tq % LANES == 0:
        _dsa_select_key_major(qi_ref, kiwi_ref, kid_ref, tri_ref, key_ref, bias_ref, qlhs_ref, wb_ref,
                              tq=tq, tk=tk, nkb=nkb, q_first=q_first, l_valid=l_valid, topk=topk)
    else:
        _dsa_select_query_major(qi_ref, kiwi_ref, kid_ref, tri_ref, key_ref, bias_ref, qlhs_ref, wb_ref,
                                tq=tq, tk=tk, nkb=nkb, q_first=q_first, l_valid=l_valid, topk=topk)
    _dsa_attend(q_ref, k_ref, v_ref, o_ref, bias_ref, lhs_ref, m_ref, acc_ref, tq=tq, tk=tk, nkb=nkb)


def _dsa_select_query_major(qi_ref, kiwi_ref, kid_ref, tri_ref, key_ref, bias_ref, qlhs_ref, wb_ref, *,
                            tq, tk, nkb, q_first, l_valid, topk):
    rc = min(ROW_CHUNK, tq)
    nrc = tq // rc
    nrep = tk // LANES

    lane128 = lax.broadcasted_iota(jnp.int32, (rc, LANES), 1)
    for rh in range(nrc):
        for c in range(N_IDX_HEADS // 2):
            qc = qi_ref[rh * rc:(rh + 1) * rc, c * LANES:(c + 1) * LANES]
            qlhs_ref[rh, c, :rc, :] = jnp.where(lane128 < IDX_DIM, qc, jnp.zeros_like(qc))
            qlhs_ref[rh, c, rc:, :] = jnp.where(lane128 >= IDX_DIM, qc, jnp.zeros_like(qc))
        wi = kiwi_ref[rh * rc:(rh + 1) * rc, IDX_DIM:IDX_DIM + N_IDX_HEADS]
        for h in range(N_IDX_HEADS):
            wb_ref[rh, h] = jnp.broadcast_to(wi[:, h:h + 1], (rc, LANES))

    def idx_block(kb, carry):
        k0 = pl.multiple_of(kb * tk, tk)
        kblk = kid_ref[pl.ds(k0, tk), :]
        kpos = k0 + lax.broadcasted_iota(jnp.int32, (rc, tk), 1)
        for rh in range(nrc):
            score = jnp.zeros((rc, tk), F32)
            for c in range(N_IDX_HEADS // 2):
                rel = jnp.maximum(_dot_nt(qlhs_ref[rh, c], kblk), 0.0)
                score = (score + _lane_tile(wb_ref[rh, 2 * c], nrep) * rel[:rc]
                         + _lane_tile(wb_ref[rh, 2 * c + 1], nrep) * rel[rc:])
            q_chunk = lax.shift_right_arithmetic(
                q_first + rh * rc + lax.broadcasted_iota(jnp.int32, (rc, tk), 0), CHUNK_SHIFT)
            adm = (lax.shift_right_arithmetic(kpos, CHUNK_SHIFT) <= q_chunk) & (kpos < l_valid)
            key_ref[kb, rh * rc:(rh + 1) * rc, :] = jnp.where(adm, _sortable_key(score), INT_MIN)
        return carry

    lax.fori_loop(0, nkb, idx_block, 0)
    key_ref[nkb] = jnp.full((tq, tk), INT_MIN, jnp.int32)

    def lane_counts(rh, pred):
        def cnt_pair(kp, cnt):
            for kb in (2 * kp, 2 * kp + 1):
                for c in range(nrep):
                    cnt = cnt + jnp.where(pred(key_ref[kb, rh * rc:(rh + 1) * rc, c * LANES:(c + 1) * LANES]), 1.0, 0.0)
            return cnt
        return lax.fori_loop(0, (nkb + 1) // 2, cnt_pair, jnp.zeros((rc, LANES), F32))

    def count(preds):
        partial = [lane_counts(rh, preds[rh]) for rh in range(nrc)]
        return [jnp.sum(c, axis=1, keepdims=True) for c in partial]

    def bit_step(i, tus):
        bit = lax.shift_left(jnp.int32(1), 31 - i)
        cands_u = [tu | bit for tu in tus]
        totals = count([(lambda key, cs=cu ^ INT_MIN: key >= cs) for cu in cands_u])
        return tuple(jnp.where(tot >= topk, cu, tu) for tot, cu, tu in zip(totals, cands_u, tus))

    tus = lax.fori_loop(0, 32, bit_step, tuple(jnp.zeros((rc, LANES), jnp.int32) for _ in range(nrc)))
    thr128 = [tu ^ INT_MIN for tu in tus]
    n_gt = count([(lambda key, th=th: key > th) for th in thr128])
    thr = thr128[0][:, :1] if nrc == 1 else jnp.concatenate([th[:, :1] for th in thr128], axis=0)
    need = topk - (n_gt[0] if nrc == 1 else jnp.concatenate(n_gt, axis=0))

    def sel_block(kb, seen):
        key = key_ref[kb]
        eq = key == thr
        eqf = jnp.where(eq, 1.0, 0.0)
        pref = _dot(eqf.astype(BF16), tri_ref[...]) + seen
        keep_tie = jnp.where(eq, jnp.where(pref <= need, 1, 0), 0)
        sel = jnp.where(key > thr, 1, keep_tie)
        sel = jnp.where(key == INT_MIN, 0, sel)
        bias_ref[kb] = jnp.where(sel == 1, 0.0, NEG_BIG)
        return pref[:, tk - 1:tk]

    lax.fori_loop(0, nkb, sel_block, jnp.zeros((tq, 1), F32))


def _dsa_attend(q_ref, k_ref, v_ref, o_ref, bias_ref, lhs_ref, m_ref, acc_ref, *, tq, tk, nkb):
    rep = N_HEADS_A // N_KV_HEADS_A
    rc = min(ROW_CHUNK, tq)
    ones = jnp.ones((tk, LANES), BF16)
    for g in range(N_KV_HEADS_A):
        for r in range(rep):
            lhs_ref[g, r * tq:(r + 1) * tq, :] = q_ref[:, (g * rep + r) * HEAD_DIM:(g * rep + r + 1) * HEAD_DIM]
    m_ref[...] = jnp.full(m_ref.shape, NEG_BIG, F32)
    acc_ref[...] = jnp.zeros(acc_ref.shape, F32)

    def kv_block(kb, c):
        k0 = pl.multiple_of(kb * tk, tk)
        for g in range(N_KV_HEADS_A):
            kblk = k_ref[pl.ds(k0, tk), g * HEAD_DIM:(g + 1) * HEAD_DIM]
            vext = jnp.concatenate([v_ref[pl.ds(k0, tk), g * HEAD_DIM:(g + 1) * HEAD_DIM], ones], axis=1)
            _softmax_step(lhs_ref.at[g], m_ref.at[g], acc_ref.at[g], kblk, vext,
                          lambda r: bias_ref[kb, (r * rc) % tq:(r * rc) % tq + rc, :],
                          rows=rep * tq, rc=rc, tk=tk)
        return c

    lax.fori_loop(0, nkb, kv_block, 0)
    for g in range(N_KV_HEADS_A):
        acc = acc_ref[g]
        o = acc[:, :HEAD_DIM] / acc[:, HEAD_DIM:]
        for r in range(rep):
            h0 = (g * rep + r) * HEAD_DIM
            o_ref[:, h0:h0 + HEAD_DIM] = o[r * tq:(r + 1) * tq].astype(o_ref.dtype)


def _dsa_attention(q, qi, kiwi, k_all, v_all, kid_all, *, batch, lq, tq, tk, l_valid, q_pos0, topk):
    lk_pad = k_all.shape[1]
    nq = lq // tq
    nkb_max = lk_pad // tk
    a_q = N_HEADS_A * HEAD_DIM
    a_kv = N_KV_HEADS_A * HEAD_DIM
    rep = N_HEADS_A // N_KV_HEADS_A
    rc = min(ROW_CHUNK, tq)
    ones_tri = np.triu(np.ones((tk, tk), np.float32))
    if tq % LANES == 0:
        tri = jnp.asarray(ones_tri.T, BF16)
        key_scratch = pltpu.VMEM((nkb_max + 1, tq // LANES, tk, LANES), jnp.int32)
        w_scratch = pltpu.VMEM((tq // LANES, N_IDX_HEADS, LANES), F32)
    else:
        tri = jnp.asarray(ones_tri, BF16)
        key_scratch = pltpu.VMEM((nkb_max + 1, tq, tk), jnp.int32)
        w_scratch = pltpu.VMEM((tq // rc, N_IDX_HEADS, rc, LANES), F32)
    return pl.pallas_call(
        functools.partial(_dsa_body, tq=tq, tk=tk, nq=nq, l_valid=l_valid, q_pos0=q_pos0, topk=topk),
        out_shape=jax.ShapeDtypeStruct((batch * lq, a_q), BF16),
        grid=(batch, nq),
        in_specs=[
            pl.BlockSpec((tq, a_q), lambda b, t: (b * nq + t, 0)),
            pl.BlockSpec((tq, N_IDX_HEADS * IDX_DIM), lambda b, t: (b * nq + t, 0)),
            pl.BlockSpec((tq, LANES), lambda b, t: (b * nq + t, 0)),
            pl.BlockSpec((None, lk_pad, a_kv), lambda b, t: (b, 0, 0)),
            pl.BlockSpec((None, lk_pad, a_kv), lambda b, t: (b, 0, 0)),
            pl.BlockSpec((None, lk_pad, LANES), lambda b, t: (b, 0, 0)),
            pl.BlockSpec((tk, tk), lambda b, t: (0, 0)),
        ],
        out_specs=pl.BlockSpec((tq, a_q), lambda b, t: (b * nq + t, 0)),
        scratch_shapes=[key_scratch, pltpu.VMEM((nkb_max, tq, tk), F32),
                        pltpu.VMEM((N_KV_HEADS_A, rep * tq, HEAD_DIM), BF16),
                        pltpu.VMEM((N_KV_HEADS_A, rep * tq, LANES), F32),
                        pltpu.VMEM((N_KV_HEADS_A, rep * tq, 2 * HEAD_DIM), F32),
                        pltpu.VMEM((tq // rc, N_IDX_HEADS // 2, 2 * rc, LANES), BF16),
                        w_scratch],
        compiler_params=_cparams(("parallel", "arbitrary")),
        name="dsa_attention",
    )(q, qi, kiwi, k_all, v_all, kid_all, tri)


def _diff_body(q_ref, k_ref, v_ref, lq1_ref, lk1_ref, lq2_ref, lk2_ref, g_ref, o_ref, lhs_ref, m_ref, acc_ref, *,
               tq, tk, nq, l_valid, q_pos0, lam_init, rc, heads):
    lam = (jnp.exp(jnp.sum(lq1_ref[...] * lk1_ref[...], axis=1, keepdims=True))
           - jnp.exp(jnp.sum(lq2_ref[...] * lk2_ref[...], axis=1, keepdims=True)) + lam_init)
    lane = lax.broadcasted_iota(jnp.int32, (tq, LANES), 1)
    ones = jnp.ones((tk, LANES), BF16)
    hd = 2 * DIFF_DIM

    def tile(t, carry):
        q0 = pl.multiple_of(t * tq, tq)
        q_first = q_pos0 + t * tq
        for h in range(heads):
            qc = q_ref[pl.ds(q0, tq), h * hd:(h + 1) * hd]
            lhs_ref[h, :tq, :] = jnp.where(lane < DIFF_DIM, qc, jnp.zeros_like(qc))
            lhs_ref[h, tq:, :] = jnp.where(lane >= DIFF_DIM, qc, jnp.zeros_like(qc))
        m_ref[...] = jnp.full(m_ref.shape, NEG_BIG, F32)
        acc_ref[...] = jnp.zeros(acc_ref.shape, F32)
        nfull = jnp.minimum(((q_first // CHUNK + 1) * CHUNK) // tk, l_valid // tk)
        nkb = (jnp.minimum(l_valid, ((q_first + tq - 1) // CHUNK + 1) * CHUNK) + tk - 1) // tk

        def step(kb, c, masked):
            k0 = pl.multiple_of(kb * tk, tk)

            def bias_fn(r):
                row0 = q_first + (r * rc) % tq
                q_chunk = lax.shift_right_arithmetic(row0 + lax.broadcasted_iota(jnp.int32, (rc, tk), 0), CHUNK_SHIFT)
                kpos = k0 + lax.broadcasted_iota(jnp.int32, (rc, tk), 1)
                adm = (lax.shift_right_arithmetic(kpos, CHUNK_SHIFT) <= q_chunk) & (kpos < l_valid)
                return jnp.where(adm, 0.0, NEG_BIG)

            for h in range(heads):
                kblk = k_ref[pl.ds(k0, tk), h * hd:(h + 1) * hd]
                vext = jnp.concatenate([v_ref[pl.ds(k0, tk), h * hd:(h + 1) * hd], ones], axis=1)
                _softmax_step(lhs_ref.at[h], m_ref.at[h], acc_ref.at[h], kblk, vext, bias_fn if masked else None,
                              rows=2 * tq, rc=rc, tk=tk)
            return c

        lax.fori_loop(0, nfull, functools.partial(step, masked=False), 0)
        lax.fori_loop(nfull, nkb, functools.partial(step, masked=True), 0)
        for h in range(heads):
            acc = acc_ref[h]
            o = acc[:, :hd] / acc[:, hd:]
            o = o[:tq] - lam * o[tq:]
            o = o * lax.rsqrt(jnp.mean(o * o, axis=-1, keepdims=True) + LN_EPS) * g_ref[...] * (1.0 - lam_init)
            o_ref[pl.ds(q0, tq), h * hd:(h + 1) * hd] = o.astype(o_ref.dtype)
        return carry

    lax.fori_loop(0, nq, tile, 0)


def _diff_attention(q, k_all, v_all, lq1, lk1, lq2, lk2, g, j, *, batch, lq, tq, tk, l_valid, q_pos0, lam_init):
    lk_pad = k_all.shape[1]
    nq = lq // tq
    hd = 2 * DIFF_DIM
    heads = DIFF_HEADS_PER_STEP
    rc = min(ROW_CHUNK, 2 * tq)
    lam_spec = pl.BlockSpec((None, 1, DIFF_DIM), lambda b, h: (j, 0, 0))
    return pl.pallas_call(
        functools.partial(_diff_body, tq=tq, tk=tk, nq=nq, l_valid=l_valid, q_pos0=q_pos0, lam_init=lam_init,
                          rc=rc, heads=heads),
        out_shape=jax.ShapeDtypeStruct((batch * lq, N_HEADS_B * hd), BF16),
        grid=(batch, N_HEADS_B // heads),
        in_specs=[
            pl.BlockSpec((lq, heads * hd), lambda b, h: (b, h)),
            pl.BlockSpec((None, lk_pad, heads * hd), lambda b, h: (b, 0, h)),
            pl.BlockSpec((None, lk_pad, heads * hd), lambda b, h: (b, 0, h)),
            lam_spec, lam_spec, lam_spec, lam_spec,
            pl.BlockSpec((None, 1, hd), lambda b, h: (j, 0, 0)),
        ],
        out_specs=pl.BlockSpec((lq, heads * hd), lambda b, h: (b, h)),
        scratch_shapes=[pltpu.VMEM((heads, 2 * tq, hd), BF16), pltpu.VMEM((heads, 2 * tq, LANES), F32),
                        pltpu.VMEM((heads, 2 * tq, 2 * hd), F32)],
        compiler_params=_cparams(("parallel", "parallel")),
        name="diff_attention",
    )(q, k_all, v_all, lq1, lk1, lq2, lk2, g)


def _pad_keys(x, lk_pad):
    pad = lk_pad - x.shape[1]
    return x if pad == 0 else jnp.pad(x, ((0, 0), (0, pad), (0, 0)))


def _trunk(x, p, pos0, pasts, w, *, tm):
    batch, seq, d = x.shape
    n = batch * seq
    depth = w["ln_g"].shape[0]
    alpha = float((2 * depth) ** 0.25)
    a_q = N_HEADS_A * HEAD_DIM
    a_kv = N_KV_HEADS_A * HEAD_DIM
    a_qi = N_IDX_HEADS * IDX_DIM
    b_qk = N_HEADS_B * 2 * DIFF_DIM

    period = max(1, seq // tm)
    table_rows = max(seq, tm)
    positions = (pos0 + np.arange(seq)).astype(np.int64)
    positions = np.tile(positions, table_rows // seq)
    tab128 = _rope_tables(positions, HEAD_DIM, "head")
    tab64 = _rope_tables(positions, IDX_DIM, "head")
    tabkw = _rope_tables(positions, IDX_DIM, "kiwi")

    xf = x.reshape(n, d)
    pf = p.reshape(depth, n, p.shape[-1])
    tf = 512 if w["ffn_w_gate"].shape[-1] % 512 == 0 else 128
    rows = []
    for i in range(depth):
        j = i // 2
        xf = _ffn(xf, w["ffn_w_gate"], w["ffn_w_up"], w["ffn_w_down"], w["ln_g"], w["ln_b"], i, 0, 0,
                  alpha=alpha, tm=tm, tf=tf)
        past = pasts[i]
        lk_new = seq if past is None else past[0].shape[1] + seq
        if i % 2 == 0:
            wa = w["a_w_in"]
            (q,) = _inproj(xf, wa[j, :, :a_q], [tab128], [(0, a_q, 0, HEAD_DIM ** -0.5 * LOG2_E, (BF16,))],
                           tm=tm, period_blocks=period)
            k, kb, v, vb = _inproj(xf, wa[j, :, a_q:a_q + 2 * a_kv], [tab128],
                                   [(0, a_kv, 0, 1.0, (ROW_INTERLEAVED, BF16)),
                                    (a_kv, 2 * a_kv, None, 1.0, (ROW_INTERLEAVED, BF16))],
                                   tm=tm, period_blocks=period)
            k_all, v_all = kb.reshape(batch, seq, a_kv), vb.reshape(batch, seq, a_kv)
            if seq % tm == 0:
                qi, kiwi, ki_t, kid = _inproj(xf, wa[j, :, a_q + 2 * a_kv:], [tab64, tabkw],
                                              [(0, a_qi, 0, 1.0, (BF16,)),
                                               (a_qi, a_qi + LANES, 1, 1.0, (F32, KEY_T, KEY_TWICE))],
                                              tm=tm, period_blocks=period, seq=seq)
                ki_rows = jnp.transpose(ki_t, (0, 2, 1))
                kid_all = kid.reshape(batch, seq, LANES)
            else:
                qi, kiwi = _inproj(xf, wa[j, :, a_q + 2 * a_kv:], [tab64, tabkw],
                                   [(0, a_qi, 0, 1.0, (BF16,)), (a_qi, a_qi + LANES, 1, 1.0, (F32,))],
                                   tm=tm, period_blocks=period)
                ki_rows = kiwi[:, :IDX_DIM].reshape(batch, seq, IDX_DIM)
                kid_all = jnp.concatenate([ki_rows, ki_rows], axis=-1).astype(BF16)
            rows.append((k.reshape(batch, seq, N_KV_HEADS_A, HEAD_DIM), v.reshape(batch, seq, N_KV_HEADS_A, HEAD_DIM),
                         ki_rows))
            if past is not None:
                pk, pv, pki = past
                k_all = jnp.concatenate([pk.reshape(batch, -1, a_kv).astype(BF16), k_all], axis=1)
                v_all = jnp.concatenate([pv.reshape(batch, -1, a_kv).astype(BF16), v_all], axis=1)
                kid_all = jnp.concatenate([jnp.concatenate([pki, pki], axis=-1).astype(BF16), kid_all], axis=1)
            tq = min(DSA_Q_TILE, seq)
            tk = KEY_BLOCK
            lk_pad = -(-lk_new // tk) * tk
            topk = min(TOPK_MAX, lk_new // 4)
            o = _dsa_attention(q, qi, kiwi, _pad_keys(k_all, lk_pad), _pad_keys(v_all, lk_pad), _pad_keys(kid_all, lk_pad),
                               batch=batch, lq=seq, tq=tq, tk=tk, l_valid=lk_new, q_pos0=pos0, topk=topk)
            xf = _oproj(xf, o, w["a_w_out"], w["ln_g"], w["ln_b"], i, j, alpha=alpha, tm=tm)
        else:
            wb = w["b_w_in"]
            lam_init = 0.8 - 0.6 * math.exp(-0.3 * i)
            (q,) = _inproj(xf, wb[j, :, :b_qk], [tab64], [(0, b_qk, 0, DIFF_DIM ** -0.5 * LOG2_E, (BF16,))],
                           tm=tm, period_blocks=period)
            if seq % tm == 0:
                kt, kb = _inproj(xf, wb[j, :, b_qk:2 * b_qk], [tab64], [(0, b_qk, 0, 1.0, (HEAD_MAJOR_T, BF16))],
                                 tm=tm, period_blocks=period, seq=seq)
                k_rows = jnp.transpose(kt, (0, 4, 1, 2, 3))
            else:
                k, kb = _inproj(xf, wb[j, :, b_qk:2 * b_qk], [tab64], [(0, b_qk, 0, 1.0, (F32, BF16))],
                                tm=tm, period_blocks=period)
                k_rows = k.reshape(batch, seq, N_HEADS_B, 2, DIFF_DIM)
            v, vb = _inproj(xf, wb[j, :, 2 * b_qk:], [], [(0, b_qk, None, 1.0, (F32, BF16))],
                            tm=tm, period_blocks=period)
            rows.append((k_rows, v.reshape(batch, seq, N_HEADS_B, 2 * DIFF_DIM)))
            k_all, v_all = kb.reshape(batch, seq, b_qk), vb.reshape(batch, seq, b_qk)
            if past is not None:
                pk, pv = past
                k_all = jnp.concatenate([pk.reshape(batch, -1, b_qk).astype(BF16), k_all], axis=1)
                v_all = jnp.concatenate([pv.reshape(batch, -1, b_qk).astype(BF16), v_all], axis=1)
            tq = min(256, seq)
            tk = KEY_BLOCK
            lk_pad = -(-lk_new // tk) * tk
            o = _diff_attention(q, _pad_keys(k_all, lk_pad), _pad_keys(v_all, lk_pad),
                                w["b_lambda_q1"], w["b_lambda_k1"], w["b_lambda_q2"], w["b_lambda_k2"], w["b_subln"], j,
                                batch=batch, lq=seq, tq=tq, tk=tk, l_valid=lk_new, q_pos0=pos0, lam_init=lam_init)
            xf = _oproj(xf, o, w["b_w_out"], w["ln_g"], w["ln_b"], i, j, alpha=alpha, tm=tm)
        xf = _ffn(xf, w["ffn_w_gate"], w["ffn_w_up"], w["ffn_w_down"], w["ln_g"], w["ln_b"], i, 1, 2,
                  alpha=alpha, tm=tm, tf=tf)
        xf = _ple(xf, pf, w["ple_w_gate"], w["ple_b_gate"], w["ple_w_proj"], w["ln_g"], w["ln_b"], i,
                  alpha=alpha, tm=tm)
    return xf.reshape(batch, seq, d), rows


def kernel(x_prompt, x_sample, cache_l0_k, cache_l0_v, cache_l0_kidx, cache_l1_k, cache_l1_v, cache_l2_k, cache_l2_v, cache_l2_kidx, cache_l3_k, cache_l3_v, p_prompt, p_sample, ln_g, ln_b, ffn_w_gate, ffn_w_up, ffn_w_down, ple_w_gate, ple_b_gate, ple_w_proj, a_w_in, a_w_out, b_w_in, b_w_out, b_lambda_q1, b_lambda_k1, b_lambda_q2, b_lambda_k2, b_subln):
    depth, _, d = ln_g.shape
    a_cols = a_w_in.shape[-1]
    a_pad = -(-a_cols // LANES) * LANES - a_cols
    w = {
        "ln_g": ln_g.reshape(depth, 4, 1, d),
        "ln_b": ln_b.reshape(depth, 4, 1, d),
        "ffn_w_gate": ffn_w_gate.astype(BF16),
        "ffn_w_up": ffn_w_up.astype(BF16),
        "ffn_w_down": ffn_w_down.astype(BF16),
        "ple_w_gate": ple_w_gate.astype(BF16),
        "ple_b_gate": ple_b_gate.reshape(depth, 1, d),
        "ple_w_proj": ple_w_proj.astype(BF16),
        "a_w_in": jnp.pad(a_w_in, ((0, 0), (0, 0), (0, a_pad))).astype(BF16),
        "a_w_out": a_w_out.astype(BF16),
        "b_w_in": b_w_in.astype(BF16),
        "b_w_out": b_w_out.astype(BF16),
        "b_lambda_q1": b_lambda_q1.reshape(-1, 1, DIFF_DIM),
        "b_lambda_k1": b_lambda_k1.reshape(-1, 1, DIFF_DIM),
        "b_lambda_q2": b_lambda_q2.reshape(-1, 1, DIFF_DIM),
        "b_lambda_k2": b_lambda_k2.reshape(-1, 1, DIFF_DIM),
        "b_subln": b_subln.reshape(-1, 1, 2 * DIFF_DIM),
    }
    past_len = cache_l0_k.shape[1]
    pasts_s = [(cache_l0_k, cache_l0_v, cache_l0_kidx), (cache_l1_k, cache_l1_v),
               (cache_l2_k, cache_l2_v, cache_l2_kidx), (cache_l3_k, cache_l3_v)]
    n_p = x_prompt.shape[0] * x_prompt.shape[1]
    n_s = x_sample.shape[0] * x_sample.shape[1]
    y_p, rows_p = _trunk(x_prompt, p_prompt, 0, [None] * depth, w, tm=min(512, n_p))
    y_s, rows_s = _trunk(x_sample, p_sample, past_len, pasts_s, w, tm=min(256, n_s))
    out = [y_p, y_s]
    for rp, rs in zip(rows_p, rows_s):
        out.extend(rp)
        out.extend(rs)
    return tuple(out)
```

```python
import functools
import math

import numpy as np
import jax
import jax.numpy as jnp
from jax import lax
from jax.experimental import pallas as pl
from jax.experimental.pallas import tpu as pltpu

CHUNK = 64
CHUNK_SHIFT = 6
HEAD_DIM = 128
N_HEADS_A = 16
N_KV_HEADS_A = 4
N_IDX_HEADS = 16
IDX_DIM = 64
TOPK_MAX = 256
N_HEADS_B = 16
DIFF_DIM = 64
ROPE_FRACTION = 4
ROPE_THETA = 500000.0
LN_EPS = 1e-5

LANES = 128
KEY_BLOCK = 256
ROW_CHUNK = 128
DIFF_HEADS_PER_STEP = 16
DSA_Q_TILE = 256
ROW_GROUPS = 2
LOG2_E = 1.4426950408889634
VMEM_LIMIT_BYTES = 56 * 1024 * 1024
NEG_BIG = -1e30
INT_MIN = -(2 ** 31)

F32 = jnp.float32
BF16 = jnp.bfloat16
HEAD_MAJOR_T = "f32 (batch, head, map, 64, seq)"
ROW_INTERLEAVED = "f32 (rows * chunks, 128)"
KEY_T = "f32 (batch, 64, seq)"
KEY_TWICE = "bf16 (rows, 128) = [ki | ki]"


def _cparams(sem):
    return pltpu.CompilerParams(dimension_semantics=sem, vmem_limit_bytes=VMEM_LIMIT_BYTES)


def _layer_norm(z, g, b):
    mu = jnp.mean(z, axis=-1, keepdims=True)
    zc = z - mu
    var = jnp.mean(zc * zc, axis=-1, keepdims=True)
    return zc * lax.rsqrt(var + LN_EPS) * g + b


def _dot(a, b):
    return jnp.dot(a, b, preferred_element_type=F32)


def _lane_tile(x, n):
    return x if n == 1 else jnp.concatenate([x] * n, axis=1)


def _dot_nt(a, b):
    return lax.dot_general(a, b, (((1,), (1,)), ((), ())), preferred_element_type=F32)


def _ffn_body(x_ref, wg_ref, wu_ref, wd_ref, g_ref, b_ref, o_ref, xb_ref, *, alpha, nf):
    f = pl.program_id(1)

    @pl.when(f == 0)
    def _():
        x = x_ref[...]
        xb_ref[...] = x.astype(BF16)
        o_ref[...] = alpha * x

    xb = xb_ref[...]
    gate = _dot(xb, wg_ref[...])
    up = _dot(xb, wu_ref[...])
    h = (gate * jax.nn.sigmoid(gate) * up).astype(BF16)
    o_ref[...] += _dot(h, wd_ref[...])

    @pl.when(f == nf - 1)
    def _():
        o_ref[...] = _layer_norm(o_ref[...], g_ref[...], b_ref[...])


def _ffn(x, wg, wu, wd, ln_g, ln_b, layer, sub, ln_idx, *, alpha, tm, tf):
    n, d = x.shape
    f_dim = wg.shape[-1]
    nf = f_dim // tf
    return pl.pallas_call(
        functools.partial(_ffn_body, alpha=alpha, nf=nf),
        out_shape=jax.ShapeDtypeStruct((n, d), F32),
        grid=(n // tm, nf),
        in_specs=[
            pl.BlockSpec((tm, d), lambda i, f: (i, 0)),
            pl.BlockSpec((None, None, d, tf), lambda i, f: (layer, sub, 0, f)),
            pl.BlockSpec((None, None, d, tf), lambda i, f: (layer, sub, 0, f)),
            pl.BlockSpec((None, None, tf, d), lambda i, f: (layer, sub, f, 0)),
            pl.BlockSpec((None, None, 1, d), lambda i, f: (layer, ln_idx, 0, 0)),
            pl.BlockSpec((None, None, 1, d), lambda i, f: (layer, ln_idx, 0, 0)),
        ],
        out_specs=pl.BlockSpec((tm, d), lambda i, f: (i, 0)),
        scratch_shapes=[pltpu.VMEM((tm, d), BF16)],
        compiler_params=_cparams(("parallel", "arbitrary")),
        name="ffn_postnorm",
    )(x, wg, wu, wd, ln_g, ln_b)


def _ple_body(x_ref, p_ref, wg_ref, bg_ref, wp_ref, g_ref, b_ref, o_ref, *, alpha):
    rows = x_ref.shape[0] // ROW_GROUPS if x_ref.shape[0] % (ROW_GROUPS * 8) == 0 else x_ref.shape[0]
    for r0 in range(0, x_ref.shape[0], rows):
        x = x_ref[r0:r0 + rows, :]
        gate = jax.nn.sigmoid(_dot(x.astype(BF16), wg_ref[...]) + bg_ref[...])
        proj = _dot(p_ref[r0:r0 + rows, :].astype(BF16), wp_ref[...])
        o_ref[r0:r0 + rows, :] = _layer_norm(alpha * x + gate * proj, g_ref[...], b_ref[...])


def _ple(x, p, wg, bg, wp, ln_g, ln_b, layer, *, alpha, tm):
    n, d = x.shape
    pd = p.shape[-1]
    return pl.pallas_call(
        functools.partial(_ple_body, alpha=alpha),
        out_shape=jax.ShapeDtypeStruct((n, d), F32),
        grid=(n // tm,),
        in_specs=[
            pl.BlockSpec((tm, d), lambda i: (i, 0)),
            pl.BlockSpec((None, tm, pd), lambda i: (layer, i, 0)),
            pl.BlockSpec((None, d, d), lambda i: (layer, 0, 0)),
            pl.BlockSpec((None, 1, d), lambda i: (layer, 0, 0)),
            pl.BlockSpec((None, pd, d), lambda i: (layer, 0, 0)),
            pl.BlockSpec((None, None, 1, d), lambda i: (layer, 3, 0, 0)),
            pl.BlockSpec((None, None, 1, d), lambda i: (layer, 3, 0, 0)),
        ],
        out_specs=pl.BlockSpec((tm, d), lambda i: (i, 0)),
        compiler_params=_cparams(("parallel",)),
        name="ple_postnorm",
    )(x, p, wg, bg, wp, ln_g, ln_b)


def _oproj_body(x_ref, a_ref, w_ref, g_ref, b_ref, o_ref, *, alpha):
    rows = x_ref.shape[0] // ROW_GROUPS if x_ref.shape[0] % (ROW_GROUPS * 8) == 0 else x_ref.shape[0]
    for r0 in range(0, x_ref.shape[0], rows):
        mix = _dot(a_ref[r0:r0 + rows, :], w_ref[...])
        o_ref[r0:r0 + rows, :] = _layer_norm(alpha * x_ref[r0:r0 + rows, :] + mix, g_ref[...], b_ref[...])


def _oproj(x, a, w, ln_g, ln_b, layer, j, *, alpha, tm):
    n, d = x.shape
    ad = a.shape[-1]
    return pl.pallas_call(
        functools.partial(_oproj_body, alpha=alpha),
        out_shape=jax.ShapeDtypeStruct((n, d), F32),
        grid=(n // tm,),
        in_specs=[
            pl.BlockSpec((tm, d), lambda i: (i, 0)),
            pl.BlockSpec((tm, ad), lambda i: (i, 0)),
            pl.BlockSpec((None, ad, d), lambda i: (j, 0, 0)),
            pl.BlockSpec((None, None, 1, d), lambda i: (layer, 1, 0, 0)),
            pl.BlockSpec((None, None, 1, d), lambda i: (layer, 1, 0, 0)),
        ],
        out_specs=pl.BlockSpec((tm, d), lambda i: (i, 0)),
        compiler_params=_cparams(("parallel",)),
        name="oproj_postnorm",
    )(x, a, w, ln_g, ln_b)


def _rope_tables(positions, head_dim, kind):
    rd = head_dim // ROPE_FRACTION
    half = rd // 2
    inv_freq = (np.float32(ROPE_THETA) ** (-(np.arange(half, dtype=np.float32) / np.float32(half)))).astype(np.float32)
    ang = positions.astype(np.float32)[:, None] * inv_freq[None, :]
    cos, sin = np.cos(ang).astype(np.float32), np.sin(ang).astype(np.float32)
    npos = positions.shape[0]
    c = np.ones((npos, LANES), np.float32)
    sa = np.zeros((npos, LANES), np.float32)
    sb = np.zeros((npos, LANES), np.float32)
    starts = range(0, LANES, head_dim) if kind == "head" else [0]
    for h0 in starts:
        c[:, h0:h0 + half] = cos
        c[:, h0 + half:h0 + rd] = cos
        sa[:, h0:h0 + half] = -sin
        sb[:, h0 + half:h0 + rd] = sin
    if kind == "kiwi":
        c[:, IDX_DIM:IDX_DIM + N_IDX_HEADS] = (N_IDX_HEADS * IDX_DIM) ** -0.5
    return jnp.asarray(c), jnp.asarray(sa), jnp.asarray(sb), half


def _inproj_body(*refs, segs, n_tab, col_step):
    x_ref, w_ref = refs[0], refs[1]
    tab_refs = refs[2:2 + 3 * n_tab]
    out_refs = refs[2 + 3 * n_tab:]
    xb = x_ref[...].astype(BF16)
    oi = 0
    for (c0, c1, tab, half, scale, dtypes) in segs:
        outs = out_refs[oi:oi + len(dtypes)]
        oi += len(dtypes)
        for j0 in range(c0, c1, col_step):
            wj = min(col_step, c1 - j0)
            y = _dot(xb, w_ref[:, j0:j0 + wj])
            for c in range(wj // LANES):
                yc = y[:, c * LANES:(c + 1) * LANES]
                if tab is not None:
                    cc, sa, sb = (r[...] for r in tab_refs[3 * tab:3 * tab + 3])
                    yc = (yc * cc + pltpu.roll(yc, LANES - half, 1) * sa + pltpu.roll(yc, half, 1) * sb)
                if scale != 1.0:
                    yc = yc * scale
                lo = j0 - c0 + c * LANES
                for o_ref, dt in zip(outs, dtypes):
                    if dt == HEAD_MAJOR_T:
                        yt = yc.T
                        o_ref[lo // LANES, 0] = yt[:DIFF_DIM]
                        o_ref[lo // LANES, 1] = yt[DIFF_DIM:]
                    elif dt == ROW_INTERLEAVED:
                        nchunk = (c1 - c0) // LANES
                        o_ref[pl.ds(lo // LANES, yc.shape[0], stride=nchunk), :] = yc
                    elif dt == KEY_T:
                        o_ref[...] = yc.T[:IDX_DIM]
                    elif dt == KEY_TWICE:
                        lane = lax.broadcasted_iota(jnp.int32, yc.shape, 1)
                        o_ref[...] = jnp.where(lane < IDX_DIM, yc, pltpu.roll(yc, IDX_DIM, 1)).astype(BF16)
                    else:
                        o_ref[:, lo:lo + LANES] = yc.astype(dt)


def _inproj(x, w, tables, segs, *, tm, period_blocks, seq=None):
    n, d = x.shape
    ctot = w.shape[1]
    segs_full = tuple((c0, c1, tab, (tables[tab][3] if tab is not None else 0), scale, tuple(dts))
                      for (c0, c1, tab, scale, dts) in segs)
    in_specs = [pl.BlockSpec((tm, d), lambda i: (i, 0)), pl.BlockSpec((d, ctot), lambda i: (0, 0))]
    args = [x, w]
    for (c, sa, sb, _) in tables:
        for t in (c, sa, sb):
            in_specs.append(pl.BlockSpec((tm, LANES), lambda i: (i % period_blocks, 0)))
            args.append(t)
    out_shape, out_specs = [], []
    for (c0, c1, _, _, dts) in segs:
        for dt in dts:
            if dt == HEAD_MAJOR_T:
                heads = (c1 - c0) // (2 * DIFF_DIM)
                out_shape.append(jax.ShapeDtypeStruct((n // seq, heads, 2, DIFF_DIM, seq), F32))
                out_specs.append(pl.BlockSpec((None, heads, 2, DIFF_DIM, tm),
                                              lambda i: (i // (seq // tm), 0, 0, 0, i % (seq // tm))))
            elif dt == ROW_INTERLEAVED:
                nchunk = (c1 - c0) // LANES
                out_shape.append(jax.ShapeDtypeStruct((n * nchunk, LANES), F32))
                out_specs.append(pl.BlockSpec((tm * nchunk, LANES), lambda i: (i, 0)))
            elif dt == KEY_T:
                out_shape.append(jax.ShapeDtypeStruct((n // seq, IDX_DIM, seq), F32))
                out_specs.append(pl.BlockSpec((None, IDX_DIM, tm), lambda i: (i // (seq // tm), 0, i % (seq // tm))))
            elif dt == KEY_TWICE:
                out_shape.append(jax.ShapeDtypeStruct((n, LANES), BF16))
                out_specs.append(pl.BlockSpec((tm, LANES), lambda i: (i, 0)))
            else:
                out_shape.append(jax.ShapeDtypeStruct((n, c1 - c0), dt))
                out_specs.append(pl.BlockSpec((tm, c1 - c0), lambda i: (i, 0)))
    return pl.pallas_call(
        functools.partial(_inproj_body, segs=segs_full, n_tab=len(tables), col_step=512),
        out_shape=out_shape,
        grid=(n // tm,),
        in_specs=in_specs,
        out_specs=out_specs,
        compiler_params=_cparams(("parallel",)),
        name="inproj_rope",
    )(*args)


def _softmax_step(lhs_ref, m_ref, acc_ref, kblk, vext, bias_fn, *, rows, rc, tk):
    for r in range(rows // rc):
        r0, r1 = r * rc, (r + 1) * rc
        s = _dot_nt(lhs_ref[r0:r1, :], kblk)
        if bias_fn is not None:
            s = s + bias_fn(r)
        m_prev = m_ref[r0:r1, :]
        m_next = jnp.maximum(m_prev, jnp.max(s, axis=1, keepdims=True))
        a = jnp.exp2(m_prev - m_next)
        p = jnp.exp2(s - _lane_tile(m_next, tk // LANES))
        acc_ref[r0:r1, :] = acc_ref[r0:r1, :] * _lane_tile(a, 2) + _dot(p.astype(BF16), vext)
        m_ref[r0:r1, :] = m_next


def _sortable_key(score):
    score = jnp.where(score == 0.0, 0.0, score)
    bits = pltpu.bitcast(score, jnp.int32)
    return bits ^ (lax.shift_right_arithmetic(bits, 31) & 0x7FFFFFFF)


def _dsa_select_key_major(qi_ref, kiwi_ref, kid_ref, tril_ref, key_ref, bias_ref, qlhs_ref, wt_ref, *,
                          tq, tk, nkb, q_first, l_valid, topk):
    rc = LANES
    nrc = tq // rc
    lane128 = lax.broadcasted_iota(jnp.int32, (rc, LANES), 1)
    for rh in range(nrc):
        for c in range(N_IDX_HEADS // 2):
            qc = qi_ref[rh * rc:(rh + 1) * rc, c * LANES:(c + 1) * LANES]
            qlhs_ref[rh, c, :rc, :] = jnp.where(lane128 < IDX_DIM, qc, jnp.zeros_like(qc))
            qlhs_ref[rh, c, rc:, :] = jnp.where(lane128 >= IDX_DIM, qc, jnp.zeros_like(qc))
        kw_t = kiwi_ref[rh * rc:(rh + 1) * rc, :].T
        wt_ref[rh] = kw_t[IDX_DIM:IDX_DIM + N_IDX_HEADS, :]

    def idx_block(kb, carry):
        k0 = pl.multiple_of(kb * tk, tk)
        kblk = kid_ref[pl.ds(k0, tk), :]
        kpos = k0 + lax.broadcasted_iota(jnp.int32, (tk, rc), 0)
        k_ok = kpos < l_valid
        k_chunk = lax.shift_right_arithmetic(kpos, CHUNK_SHIFT)
        for rh in range(nrc):
            score = jnp.zeros((tk, rc), F32)
            for c in range(N_IDX_HEADS // 2):
                rel = jnp.maximum(_dot_nt(kblk, qlhs_ref[rh, c]), 0.0)
                score = (score + wt_ref[rh, 2 * c:2 * c + 1, :] * rel[:, :rc]
                         + wt_ref[rh, 2 * c + 1:2 * c + 2, :] * rel[:, rc:])
            q_chunk = lax.shift_right_arithmetic(
                q_first + rh * rc + lax.broadcasted_iota(jnp.int32, (tk, rc), 1), CHUNK_SHIFT)
            key_ref[kb, rh] = jnp.where((k_chunk <= q_chunk) & k_ok, _sortable_key(score), INT_MIN)
        return carry

    lax.fori_loop(0, nkb, idx_block, 0)
    key_ref[nkb] = jnp.full((nrc, tk, rc), INT_MIN, jnp.int32)

    n_acc = 4

    def count(preds):
        def cnt_pair(kp, accs):
            accs = [list(a) for a in accs]
            for kb in (2 * kp, 2 * kp + 1):
                for rh in range(nrc):
                    hit = jnp.where(preds[rh](key_ref[kb, rh]), 1.0, 0.0)
                    for j in range(tk // 8):
                        accs[rh][j % n_acc] = accs[rh][j % n_acc] + hit[j * 8:(j + 1) * 8, :]
            return tuple(tuple(a) for a in accs)

        zero = tuple(tuple(jnp.zeros((8, rc), F32) for _ in range(n_acc)) for _ in range(nrc))
        accs = lax.fori_loop(0, (nkb + 1) // 2, cnt_pair, zero)
        return [jnp.sum(sum(a[1:], a[0]), axis=0, keepdims=True) for a in accs]

    def bit_step(i, tus):
        bit = lax.shift_left(jnp.int32(1), 31 - i)
        cands_u = [tu | bit for tu in tus]
        totals = count([(lambda key, cs=cu ^ INT_MIN: key >= cs) for cu in cands_u])
        return tuple(jnp.where(tot >= topk, cu, tu) for tot, cu, tu in zip(totals, cands_u, tus))

    tus = lax.fori_loop(0, 32, bit_step, tuple(jnp.zeros((1, rc), jnp.int32) for _ in range(nrc)))
    thr = [tu ^ INT_MIN for tu in tus]
    n_gt = count([(lambda key, th=th: key > th) for th in thr])
    need = [topk - n for n in n_gt]

    def sel_block(kb, seen):
        new_seen = []
        for rh in range(nrc):
            key = key_ref[kb, rh]
            eq = key == thr[rh]
            eqf = jnp.where(eq, 1.0, 0.0)
            pref = _dot(tril_ref[...], eqf.astype(BF16)) + seen[rh]
            keep_tie = jnp.where(eq, jnp.where(pref <= need[rh], 1, 0), 0)
            sel = jnp.where(key > thr[rh], 1, keep_tie)
            sel = jnp.where(key == INT_MIN, 0, sel)
            bias_ref[kb, rh * rc:(rh + 1) * rc, :] = jnp.where(sel == 1, 0.0, NEG_BIG).T
            new_seen.append(pref[tk - 1:tk, :])
        return tuple(new_seen)

    lax.fori_loop(0, nkb, sel_block, tuple(jnp.zeros((1, rc), F32) for _ in range(nrc)))


def _dsa_body(q_ref, qi_ref, kiwi_ref, k_ref, v_ref, kid_ref, tri_ref, o_ref, key_ref, bias_ref,
              lhs_ref, m_ref, acc_ref, qlhs_ref, wb_ref, *, tq, tk, nq, l_valid, q_pos0, topk):
    t = pl.program_id(1)
    q_first = q_pos0 + t * tq
    q_last_chunk = (q_first + tq - 1) // CHUNK
    kmax = jnp.minimum(l_valid, (q_last_chunk + 1) * CHUNK)
    nkb = (kmax + tk - 1) // tk

    if tq % LANES == 0:
        _dsa_select_key_major(qi_ref, kiwi_ref, kid_ref, tri_ref, key_ref, bias_ref, qlhs_ref, wb_ref,
                              tq=tq, tk=tk, nkb=nkb, q_first=q_first, l_valid=l_valid, topk=topk)
    else:
        _dsa_select_query_major(qi_ref, kiwi_ref, kid_ref, tri_ref, key_ref, bias_ref, qlhs_ref, wb_ref,
                                tq=tq, tk=tk, nkb=nkb, q_first=q_first, l_valid=l_valid, topk=topk)
    _dsa_attend(q_ref, k_ref, v_ref, o_ref, bias_ref, lhs_ref, m_ref, acc_ref, tq=tq, tk=tk, nkb=nkb)


def _dsa_select_query_major(qi_ref, kiwi_ref, kid_ref, tri_ref, key_ref, bias_ref, qlhs_ref, wb_ref, *,
                            tq, tk, nkb, q_first, l_valid, topk):
    rc = min(ROW_CHUNK, tq)
    nrc = tq // rc
    nrep = tk // LANES

    lane128 = lax.broadcasted_iota(jnp.int32, (rc, LANES), 1)
    for rh in range(nrc):
        for c in range(N_IDX_HEADS // 2):
            qc = qi_ref[rh * rc:(rh + 1) * rc, c * LANES:(c + 1) * LANES]
            qlhs_ref[rh, c, :rc, :] = jnp.where(lane128 < IDX_DIM, qc, jnp.zeros_like(qc))
            qlhs_ref[rh, c, rc:, :] = jnp.where(lane128 >= IDX_DIM, qc, jnp.zeros_like(qc))
        wi = kiwi_ref[rh * rc:(rh + 1) * rc, IDX_DIM:IDX_DIM + N_IDX_HEADS]
        for h in range(N_IDX_HEADS):
            wb_ref[rh, h] = jnp.broadcast_to(wi[:, h:h + 1], (rc, LANES))

    def idx_block(kb, carry):
        k0 = pl.multiple_of(kb * tk, tk)
        kblk = kid_ref[pl.ds(k0, tk), :]
        kpos = k0 + lax.broadcasted_iota(jnp.int32, (rc, tk), 1)
        for rh in range(nrc):
            score = jnp.zeros((rc, tk), F32)
            for c in range(N_IDX_HEADS // 2):
                rel = jnp.maximum(_dot_nt(qlhs_ref[rh, c], kblk), 0.0)
                score = (score + _lane_tile(wb_ref[rh, 2 * c], nrep) * rel[:rc]
                         + _lane_tile(wb_ref[rh, 2 * c + 1], nrep) * rel[rc:])
            q_chunk = lax.shift_right_arithmetic(
                q_first + rh * rc + lax.broadcasted_iota(jnp.int32, (rc, tk), 0), CHUNK_SHIFT)
            adm = (lax.shift_right_arithmetic(kpos, CHUNK_SHIFT) <= q_chunk) & (kpos < l_valid)
            key_ref[kb, rh * rc:(rh + 1) * rc, :] = jnp.where(adm, _sortable_key(score), INT_MIN)
        return carry

    lax.fori_loop(0, nkb, idx_block, 0)
    key_ref[nkb] = jnp.full((tq, tk), INT_MIN, jnp.int32)

    def lane_counts(rh, pred):
        def cnt_pair(kp, cnt):
            for kb in (2 * kp, 2 * kp + 1):
                for c in range(nrep):
                    cnt = cnt + jnp.where(pred(key_ref[kb, rh * rc:(rh + 1) * rc, c * LANES:(c + 1) * LANES]), 1.0, 0.0)
            return cnt
        return lax.fori_loop(0, (nkb + 1) // 2, cnt_pair, jnp.zeros((rc, LANES), F32))

    def count(preds):
        partial = [lane_counts(rh, preds[rh]) for rh in range(nrc)]
        return [jnp.sum(c, axis=1, keepdims=True) for c in partial]

    def bit_step(i, tus):
        bit = lax.shift_left(jnp.int32(1), 31 - i)
        cands_u = [tu | bit for tu in tus]
        totals = count([(lambda key, cs=cu ^ INT_MIN: key >= cs) for cu in cands_u])
        return tuple(jnp.where(tot >= topk, cu, tu) for tot, cu, tu in zip(totals, cands_u, tus))

    tus = lax.fori_loop(0, 32, bit_step, tuple(jnp.zeros((rc, LANES), jnp.int32) for _ in range(nrc)))
    thr128 = [tu ^ INT_MIN for tu in tus]
    n_gt = count([(lambda key, th=th: key > th) for th in thr128])
    thr = thr128[0][:, :1] if nrc == 1 else jnp.concatenate([th[:, :1] for th in thr128], axis=0)
    need = topk - (n_gt[0] if nrc == 1 else jnp.concatenate(n_gt, axis=0))

    def sel_block(kb, seen):
        key = key_ref[kb]
        eq = key == thr
        eqf = jnp.where(eq, 1.0, 0.0)
        pref = _dot(eqf.astype(BF16), tri_ref[...]) + seen
        keep_tie = jnp.where(eq, jnp.where(pref <= need, 1, 0), 0)
        sel = jnp.where(key > thr, 1, keep_tie)
        sel = jnp.where(key == INT_MIN, 0, sel)
        bias_ref[kb] = jnp.where(sel == 1, 0.0, NEG_BIG)
        return pref[:, tk - 1:tk]

    lax.fori_loop(0, nkb, sel_block, jnp.zeros((tq, 1), F32))


def _dsa_attend(q_ref, k_ref, v_ref, o_ref, bias_ref, lhs_ref, m_ref, acc_ref, *, tq, tk, nkb):
    rep = N_HEADS_A // N_KV_HEADS_A
    rc = min(ROW_CHUNK, tq)
    ones = jnp.ones((tk, LANES), BF16)
    for g in range(N_KV_HEADS_A):
        for r in range(rep):
            lhs_ref[g, r * tq:(r + 1) * tq, :] = q_ref[:, (g * rep + r) * HEAD_DIM:(g * rep + r + 1) * HEAD_DIM]
    m_ref[...] = jnp.full(m_ref.shape, NEG_BIG, F32)
    acc_ref[...] = jnp.zeros(acc_ref.shape, F32)

    def kv_block(kb, c):
        k0 = pl.multiple_of(kb * tk, tk)
        for g in range(N_KV_HEADS_A):
            kblk = k_ref[pl.ds(k0, tk), g * HEAD_DIM:(g + 1) * HEAD_DIM]
            vext = jnp.concatenate([v_ref[pl.ds(k0, tk), g * HEAD_DIM:(g + 1) * HEAD_DIM], ones], axis=1)
            _softmax_step(lhs_ref.at[g], m_ref.at[g], acc_ref.at[g], kblk, vext,
                          lambda r: bias_ref[kb, (r * rc) % tq:(r * rc) % tq + rc, :],
                          rows=rep * tq, rc=rc, tk=tk)
        return c

    lax.fori_loop(0, nkb, kv_block, 0)
    for g in range(N_KV_HEADS_A):
        acc = acc_ref[g]
        o = acc[:, :HEAD_DIM] / acc[:, HEAD_DIM:]
        for r in range(rep):
            h0 = (g * rep + r) * HEAD_DIM
            o_ref[:, h0:h0 + HEAD_DIM] = o[r * tq:(r + 1) * tq].astype(o_ref.dtype)


def _dsa_attention(q, qi, kiwi, k_all, v_all, kid_all, *, batch, lq, tq, tk, l_valid, q_pos0, topk):
    lk_pad = k_all.shape[1]
    nq = lq // tq
    nkb_max = lk_pad // tk
    a_q = N_HEADS_A * HEAD_DIM
    a_kv = N_KV_HEADS_A * HEAD_DIM
    rep = N_HEADS_A // N_KV_HEADS_A
    rc = min(ROW_CHUNK, tq)
    ones_tri = np.triu(np.ones((tk, tk), np.float32))
    if tq % LANES == 0:
        tri = jnp.asarray(ones_tri.T, BF16)
        key_scratch = pltpu.VMEM((nkb_max + 1, tq // LANES, tk, LANES), jnp.int32)
        w_scratch = pltpu.VMEM((tq // LANES, N_IDX_HEADS, LANES), F32)
    else:
        tri = jnp.asarray(ones_tri, BF16)
        key_scratch = pltpu.VMEM((nkb_max + 1, tq, tk), jnp.int32)
        w_scratch = pltpu.VMEM((tq // rc, N_IDX_HEADS, rc, LANES), F32)
    return pl.pallas_call(
        functools.partial(_dsa_body, tq=tq, tk=tk, nq=nq, l_valid=l_valid, q_pos0=q_pos0, topk=topk),
        out_shape=jax.ShapeDtypeStruct((batch * lq, a_q), BF16),
        grid=(batch, nq),
        in_specs=[
            pl.BlockSpec((tq, a_q), lambda b, t: (b * nq + t, 0)),
            pl.BlockSpec((tq, N_IDX_HEADS * IDX_DIM), lambda b, t: (b * nq + t, 0)),
            pl.BlockSpec((tq, LANES), lambda b, t: (b * nq + t, 0)),
            pl.BlockSpec((None, lk_pad, a_kv), lambda b, t: (b, 0, 0)),
            pl.BlockSpec((None, lk_pad, a_kv), lambda b, t: (b, 0, 0)),
            pl.BlockSpec((None, lk_pad, LANES), lambda b, t: (b, 0, 0)),
            pl.BlockSpec((tk, tk), lambda b, t: (0, 0)),
        ],
        out_specs=pl.BlockSpec((tq, a_q), lambda b, t: (b * nq + t, 0)),
        scratch_shapes=[key_scratch, pltpu.VMEM((nkb_max, tq, tk), F32),
                        pltpu.VMEM((N_KV_HEADS_A, rep * tq, HEAD_DIM), BF16),
                        pltpu.VMEM((N_KV_HEADS_A, rep * tq, LANES), F32),
                        pltpu.VMEM((N_KV_HEADS_A, rep * tq, 2 * HEAD_DIM), F32),
                        pltpu.VMEM((tq // rc, N_IDX_HEADS // 2, 2 * rc, LANES), BF16),
                        w_scratch],
        compiler_params=_cparams(("parallel", "arbitrary")),
        name="dsa_attention",
    )(q, qi, kiwi, k_all, v_all, kid_all, tri)


def _diff_body(q_ref, k_ref, v_ref, lq1_ref, lk1_ref, lq2_ref, lk2_ref, g_ref, o_ref, lhs_ref, m_ref, acc_ref, *,
               tq, tk, nq, l_valid, q_pos0, lam_init, rc, heads):
    lam = (jnp.exp(jnp.sum(lq1_ref[...] * lk1_ref[...], axis=1, keepdims=True))
           - jnp.exp(jnp.sum(lq2_ref[...] * lk2_ref[...], axis=1, keepdims=True)) + lam_init)
    lane = lax.broadcasted_iota(jnp.int32, (tq, LANES), 1)
    ones = jnp.ones((tk, LANES), BF16)
    hd = 2 * DIFF_DIM

    def tile(t, carry):
        q0 = pl.multiple_of(t * tq, tq)
        q_first = q_pos0 + t * tq
        for h in range(heads):
            qc = q_ref[pl.ds(q0, tq), h * hd:(h + 1) * hd]
            lhs_ref[h, :tq, :] = jnp.where(lane < DIFF_DIM, qc, jnp.zeros_like(qc))
            lhs_ref[h, tq:, :] = jnp.where(lane >= DIFF_DIM, qc, jnp.zeros_like(qc))
        m_ref[...] = jnp.full(m_ref.shape, NEG_BIG, F32)
        acc_ref[...] = jnp.zeros(acc_ref.shape, F32)
        nfull = jnp.minimum(((q_first // CHUNK + 1) * CHUNK) // tk, l_valid // tk)
        nkb = (jnp.minimum(l_valid, ((q_first + tq - 1) // CHUNK + 1) * CHUNK) + tk - 1) // tk

        def step(kb, c, masked):
            k0 = pl.multiple_of(kb * tk, tk)

            def bias_fn(r):
                row0 = q_first + (r * rc) % tq
                q_chunk = lax.shift_right_arithmetic(row0 + lax.broadcasted_iota(jnp.int32, (rc, tk), 0), CHUNK_SHIFT)
                kpos = k0 + lax.broadcasted_iota(jnp.int32, (rc, tk), 1)
                adm = (lax.shift_right_arithmetic(kpos, CHUNK_SHIFT) <= q_chunk) & (kpos < l_valid)
                return jnp.where(adm, 0.0, NEG_BIG)

            for h in range(heads):
                kblk = k_ref[pl.ds(k0, tk), h * hd:(h + 1) * hd]
                vext = jnp.concatenate([v_ref[pl.ds(k0, tk), h * hd:(h + 1) * hd], ones], axis=1)
                _softmax_step(lhs_ref.at[h], m_ref.at[h], acc_ref.at[h], kblk, vext, bias_fn if masked else None,
                              rows=2 * tq, rc=rc, tk=tk)
            return c

        lax.fori_loop(0, nfull, functools.partial(step, masked=False), 0)
        lax.fori_loop(nfull, nkb, functools.partial(step, masked=True), 0)
        for h in range(heads):
            acc = acc_ref[h]
            o = acc[:, :hd] / acc[:, hd:]
            o = o[:tq] - lam * o[tq:]
            o = o * lax.rsqrt(jnp.mean(o * o, axis=-1, keepdims=True) + LN_EPS) * g_ref[...] * (1.0 - lam_init)
            o_ref[pl.ds(q0, tq), h * hd:(h + 1) * hd] = o.astype(o_ref.dtype)
        return carry

    lax.fori_loop(0, nq, tile, 0)


def _diff_attention(q, k_all, v_all, lq1, lk1, lq2, lk2, g, j, *, batch, lq, tq, tk, l_valid, q_pos0, lam_init):
    lk_pad = k_all.shape[1]
    nq = lq // tq
    hd = 2 * DIFF_DIM
    heads = DIFF_HEADS_PER_STEP
    rc = min(ROW_CHUNK, 2 * tq)
    single = pl.Buffered(1)
    lam_spec = pl.BlockSpec((None, 1, DIFF_DIM), lambda b, h: (j, 0, 0))
    return pl.pallas_call(
        functools.partial(_diff_body, tq=tq, tk=tk, nq=nq, l_valid=l_valid, q_pos0=q_pos0, lam_init=lam_init,
                          rc=rc, heads=heads),
        out_shape=jax.ShapeDtypeStruct((batch * lq, N_HEADS_B * hd), BF16),
        grid=(batch, N_HEADS_B // heads),
        in_specs=[
            pl.BlockSpec((lq, heads * hd), lambda b, h: (b, h), pipeline_mode=single),
            pl.BlockSpec((None, lk_pad, heads * hd), lambda b, h: (b, 0, h), pipeline_mode=single),
            pl.BlockSpec((None, lk_pad, heads * hd), lambda b, h: (b, 0, h), pipeline_mode=single),
            lam_spec, lam_spec, lam_spec, lam_spec,
            pl.BlockSpec((None, 1, hd), lambda b, h: (j, 0, 0)),
        ],
        out_specs=pl.BlockSpec((lq, heads * hd), lambda b, h: (b, h), pipeline_mode=single),
        scratch_shapes=[pltpu.VMEM((heads, 2 * tq, hd), BF16), pltpu.VMEM((heads, 2 * tq, LANES), F32),
                        pltpu.VMEM((heads, 2 * tq, 2 * hd), F32)],
        compiler_params=_cparams(("parallel", "parallel")),
        name="diff_attention",
    )(q, k_all, v_all, lq1, lk1, lq2, lk2, g)


def _pad_keys(x, lk_pad):
    pad = lk_pad - x.shape[1]
    return x if pad == 0 else jnp.pad(x, ((0, 0), (0, pad), (0, 0)))


def _trunk(x, p, pos0, pasts, w, *, tm):
    batch, seq, d = x.shape
    n = batch * seq
    depth = w["ln_g"].shape[0]
    alpha = float((2 * depth) ** 0.25)
    a_q = N_HEADS_A * HEAD_DIM
    a_kv = N_KV_HEADS_A * HEAD_DIM
    a_qi = N_IDX_HEADS * IDX_DIM
    b_qk = N_HEADS_B * 2 * DIFF_DIM

    period = max(1, seq // tm)
    table_rows = max(seq, tm)
    positions = (pos0 + np.arange(seq)).astype(np.int64)
    positions = np.tile(positions, table_rows // seq)
    tab128 = _rope_tables(positions, HEAD_DIM, "head")
    tab64 = _rope_tables(positions, IDX_DIM, "head")
    tabkw = _rope_tables(positions, IDX_DIM, "kiwi")

    xf = x.reshape(n, d)
    pf = p.reshape(depth, n, p.shape[-1])
    tf = 512 if w["ffn_w_gate"].shape[-1] % 512 == 0 else 128
    rows = []
    for i in range(depth):
        j = i // 2
        xf = _ffn(xf, w["ffn_w_gate"], w["ffn_w_up"], w["ffn_w_down"], w["ln_g"], w["ln_b"], i, 0, 0,
                  alpha=alpha, tm=tm, tf=tf)
        past = pasts[i]
        lk_new = seq if past is None else past[0].shape[1] + seq
        if i % 2 == 0:
            wa = w["a_w_in"]
            (q,) = _inproj(xf, wa[j, :, :a_q], [tab128], [(0, a_q, 0, HEAD_DIM ** -0.5 * LOG2_E, (BF16,))],
                           tm=tm, period_blocks=period)
            k, kb, v, vb = _inproj(xf, wa[j, :, a_q:a_q + 2 * a_kv], [tab128],
                                   [(0, a_kv, 0, 1.0, (ROW_INTERLEAVED, BF16)),
                                    (a_kv, 2 * a_kv, None, 1.0, (ROW_INTERLEAVED, BF16))],
                                   tm=tm, period_blocks=period)
            k_all, v_all = kb.reshape(batch, seq, a_kv), vb.reshape(batch, seq, a_kv)
            if seq % tm == 0:
                qi, kiwi, ki_t, kid = _inproj(xf, wa[j, :, a_q + 2 * a_kv:], [tab64, tabkw],
                                              [(0, a_qi, 0, 1.0, (BF16,)),
                                               (a_qi, a_qi + LANES, 1, 1.0, (F32, KEY_T, KEY_TWICE))],
                                              tm=tm, period_blocks=period, seq=seq)
                ki_rows = jnp.transpose(ki_t, (0, 2, 1))
                kid_all = kid.reshape(batch, seq, LANES)
            else:
                qi, kiwi = _inproj(xf, wa[j, :, a_q + 2 * a_kv:], [tab64, tabkw],
                                   [(0, a_qi, 0, 1.0, (BF16,)), (a_qi, a_qi + LANES, 1, 1.0, (F32,))],
                                   tm=tm, period_blocks=period)
                ki_rows = kiwi[:, :IDX_DIM].reshape(batch, seq, IDX_DIM)
                kid_all = jnp.concatenate([ki_rows, ki_rows], axis=-1).astype(BF16)
            rows.append((k.reshape(batch, seq, N_KV_HEADS_A, HEAD_DIM), v.reshape(batch, seq, N_KV_HEADS_A, HEAD_DIM),
                         ki_rows))
            if past is not None:
                pk, pv, pki = past
                k_all = jnp.concatenate([pk.reshape(batch, -1, a_kv).astype(BF16), k_all], axis=1)
                v_all = jnp.concatenate([pv.reshape(batch, -1, a_kv).astype(BF16), v_all], axis=1)
                kid_all = jnp.concatenate([jnp.concatenate([pki, pki], axis=-1).astype(BF16), kid_all], axis=1)
            tq = min(DSA_Q_TILE, seq)
            tk = KEY_BLOCK
            lk_pad = -(-lk_new // tk) * tk
            topk = min(TOPK_MAX, lk_new // 4)
            o = _dsa_attention(q, qi, kiwi, _pad_keys(k_all, lk_pad), _pad_keys(v_all, lk_pad), _pad_keys(kid_all, lk_pad),
                               batch=batch, lq=seq, tq=tq, tk=tk, l_valid=lk_new, q_pos0=pos0, topk=topk)
            xf = _oproj(xf, o, w["a_w_out"], w["ln_g"], w["ln_b"], i, j, alpha=alpha, tm=tm)
        else:
            wb = w["b_w_in"]
            lam_init = 0.8 - 0.6 * math.exp(-0.3 * i)
            (q,) = _inproj(xf, wb[j, :, :b_qk], [tab64], [(0, b_qk, 0, DIFF_DIM ** -0.5 * LOG2_E, (BF16,))],
                           tm=tm, period_blocks=period)
            if seq % tm == 0:
                kt, kb = _inproj(xf, wb[j, :, b_qk:2 * b_qk], [tab64], [(0, b_qk, 0, 1.0, (HEAD_MAJOR_T, BF16))],
                                 tm=tm, period_blocks=period, seq=seq)
                k_rows = jnp.transpose(kt, (0, 4, 1, 2, 3))
            else:
                k, kb = _inproj(xf, wb[j, :, b_qk:2 * b_qk], [tab64], [(0, b_qk, 0, 1.0, (F32, BF16))],
                                tm=tm, period_blocks=period)
                k_rows = k.reshape(batch, seq, N_HEADS_B, 2, DIFF_DIM)
            v, vb = _inproj(xf, wb[j, :, 2 * b_qk:], [], [(0, b_qk, None, 1.0, (F32, BF16))],
                            tm=tm, period_blocks=period)
            rows.append((k_rows, v.reshape(batch, seq, N_HEADS_B, 2 * DIFF_DIM)))
            k_all, v_all = kb.reshape(batch, seq, b_qk), vb.reshape(batch, seq, b_qk)
            if past is not None:
                pk, pv = past
                k_all = jnp.concatenate([pk.reshape(batch, -1, b_qk).astype(BF16), k_all], axis=1)
                v_all = jnp.concatenate([pv.reshape(batch, -1, b_qk).astype(BF16), v_all], axis=1)
            tq = min(256, seq)
            tk = KEY_BLOCK
            lk_pad = -(-lk_new // tk) * tk
            o = _diff_attention(q, _pad_keys(k_all, lk_pad), _pad_keys(v_all, lk_pad),
                                w["b_lambda_q1"], w["b_lambda_k1"], w["b_lambda_q2"], w["b_lambda_k2"], w["b_subln"], j,
                                batch=batch, lq=seq, tq=tq, tk=tk, l_valid=lk_new, q_pos0=pos0, lam_init=lam_init)
            xf = _oproj(xf, o, w["b_w_out"], w["ln_g"], w["ln_b"], i, j, alpha=alpha, tm=tm)
        xf = _ffn(xf, w["ffn_w_gate"], w["ffn_w_up"], w["ffn_w_down"], w["ln_g"], w["ln_b"], i, 1, 2,
                  alpha=alpha, tm=tm, tf=tf)
        xf = _ple(xf, pf, w["ple_w_gate"], w["ple_b_gate"], w["ple_w_proj"], w["ln_g"], w["ln_b"], i,
                  alpha=alpha, tm=tm)
    return xf.reshape(batch, seq, d), rows


def kernel(x_prompt, x_sample, cache_l0_k, cache_l0_v, cache_l0_kidx, cache_l1_k, cache_l1_v, cache_l2_k, cache_l2_v, cache_l2_kidx, cache_l3_k, cache_l3_v, p_prompt, p_sample, ln_g, ln_b, ffn_w_gate, ffn_w_up, ffn_w_down, ple_w_gate, ple_b_gate, ple_w_proj, a_w_in, a_w_out, b_w_in, b_w_out, b_lambda_q1, b_lambda_k1, b_lambda_q2, b_lambda_k2, b_subln):
    depth, _, d = ln_g.shape
    a_cols = a_w_in.shape[-1]
    a_pad = -(-a_cols // LANES) * LANES - a_cols
    w = {
        "ln_g": ln_g.reshape(depth, 4, 1, d),
        "ln_b": ln_b.reshape(depth, 4, 1, d),
        "ffn_w_gate": ffn_w_gate.astype(BF16),
        "ffn_w_up": ffn_w_up.astype(BF16),
        "ffn_w_down": (0.5 * ffn_w_down).astype(BF16),
        "ple_w_gate": ple_w_gate.astype(BF16),
        "ple_b_gate": ple_b_gate.reshape(depth, 1, d),
        "ple_w_proj": ple_w_proj.astype(BF16),
        "a_w_in": jnp.pad(a_w_in, ((0, 0), (0, 0), (0, a_pad))).astype(BF16),
        "a_w_out": a_w_out.astype(BF16),
        "b_w_in": b_w_in.astype(BF16),
        "b_w_out": b_w_out.astype(BF16),
        "b_lambda_q1": b_lambda_q1.reshape(-1, 1, DIFF_DIM),
        "b_lambda_k1": b_lambda_k1.reshape(-1, 1, DIFF_DIM),
        "b_lambda_q2": b_lambda_q2.reshape(-1, 1, DIFF_DIM),
        "b_lambda_k2": b_lambda_k2.reshape(-1, 1, DIFF_DIM),
        "b_subln": b_subln.reshape(-1, 1, 2 * DIFF_DIM),
    }
    past_len = cache_l0_k.shape[1]
    pasts_s = [(cache_l0_k, cache_l0_v, cache_l0_kidx), (cache_l1_k, cache_l1_v),
               (cache_l2_k, cache_l2_v, cache_l2_kidx), (cache_l3_k, cache_l3_v)]
    n_p = x_prompt.shape[0] * x_prompt.shape[1]
    n_s = x_sample.shape[0] * x_sample.shape[1]
    y_p, rows_p = _trunk(x_prompt, p_prompt, 0, [None] * depth, w, tm=min(512, n_p))
    y_s, rows_s = _trunk(x_sample, p_sample, past_len, pasts_s, w, tm=min(256, n_s))
    out = [y_p, y_s]
    for rp, rs in zip(rows_p, rows_s):
        out.extend(rp)
        out.extend(rs)
    return tuple(out)
```

```python
import functools
import math

import numpy as np
import jax
import jax.numpy as jnp
from jax import lax
from jax.experimental import pallas as pl
from jax.experimental.pallas import tpu as pltpu

CHUNK = 64
CHUNK_SHIFT = 6
HEAD_DIM = 128
N_HEADS_A = 16
N_KV_HEADS_A = 4
N_IDX_HEADS = 16
IDX_DIM = 64
TOPK_MAX = 256
N_HEADS_B = 16
DIFF_DIM = 64
ROPE_FRACTION = 4
ROPE_THETA = 500000.0
LN_EPS = 1e-5

LANES = 128
KEY_BLOCK = 256
ROW_CHUNK = 128
DIFF_HEADS_PER_STEP = 16
DSA_Q_TILE = 256
ROW_GROUPS = 2
LOG2_E = 1.4426950408889634
VMEM_LIMIT_BYTES = 56 * 1024 * 1024
NEG_BIG = -1e30
INT_MIN = -(2 ** 31)

F32 = jnp.float32
BF16 = jnp.bfloat16
HEAD_MAJOR_T = "f32 (batch, head, map, 64, seq)"
ROW_INTERLEAVED = "f32 (rows * chunks, 128)"
KEY_T = "f32 (batch, 64, seq)"
KEY_TWICE = "bf16 (rows, 128) = [ki | ki]"


def _cparams(sem):
    return pltpu.CompilerParams(dimension_semantics=sem, vmem_limit_bytes=VMEM_LIMIT_BYTES)


def _layer_norm(z, g, b):
    mu = jnp.mean(z, axis=-1, keepdims=True)
    zc = z - mu
    var = jnp.mean(zc * zc, axis=-1, keepdims=True)
    return zc * lax.rsqrt(var + LN_EPS) * g + b


def _dot(a, b):
    return jnp.dot(a, b, preferred_element_type=F32)


def _lane_tile(x, n):
    return x if n == 1 else jnp.concatenate([x] * n, axis=1)


def _dot_nt(a, b):
    return lax.dot_general(a, b, (((1,), (1,)), ((), ())), preferred_element_type=F32)


def _ffn_body(x_ref, wg_ref, wu_ref, wd_ref, g_ref, b_ref, o_ref, xb_ref, *, alpha, nf):
    f = pl.program_id(1)

    @pl.when(f == 0)
    def _():
        x = x_ref[...]
        xb_ref[...] = x.astype(BF16)
        o_ref[...] = alpha * x

    xb = xb_ref[...]
    gate = _dot(xb, wg_ref[...])
    up = _dot(xb, wu_ref[...])
    h = (gate * jax.nn.sigmoid(gate) * up).astype(BF16)
    o_ref[...] += _dot(h, wd_ref[...])

    @pl.when(f == nf - 1)
    def _():
        o_ref[...] = _layer_norm(o_ref[...], g_ref[...], b_ref[...])


def _ffn(x, wg, wu, wd, ln_g, ln_b, layer, sub, ln_idx, *, alpha, tm, tf):
    n, d = x.shape
    f_dim = wg.shape[-1]
    nf = f_dim // tf
    return pl.pallas_call(
        functools.partial(_ffn_body, alpha=alpha, nf=nf),
        out_shape=jax.ShapeDtypeStruct((n, d), F32),
        grid=(n // tm, nf),
        in_specs=[
            pl.BlockSpec((tm, d), lambda i, f: (i, 0)),
            pl.BlockSpec((None, None, d, tf), lambda i, f: (layer, sub, 0, f)),
            pl.BlockSpec((None, None, d, tf), lambda i, f: (layer, sub, 0, f)),
            pl.BlockSpec((None, None, tf, d), lambda i, f: (layer, sub, f, 0)),
            pl.BlockSpec((None, None, 1, d), lambda i, f: (layer, ln_idx, 0, 0)),
            pl.BlockSpec((None, None, 1, d), lambda i, f: (layer, ln_idx, 0, 0)),
        ],
        out_specs=pl.BlockSpec((tm, d), lambda i, f: (i, 0)),
        scratch_shapes=[pltpu.VMEM((tm, d), BF16)],
        compiler_params=_cparams(("parallel", "arbitrary")),
        name="ffn_postnorm",
    )(x, wg, wu, wd, ln_g, ln_b)


def _ple_body(x_ref, p_ref, wg_ref, bg_ref, wp_ref, g_ref, b_ref, o_ref, *, alpha):
    rows = x_ref.shape[0] // ROW_GROUPS if x_ref.shape[0] % (ROW_GROUPS * 8) == 0 else x_ref.shape[0]
    for r0 in range(0, x_ref.shape[0], rows):
        x = x_ref[r0:r0 + rows, :]
        gate = jax.nn.sigmoid(_dot(x.astype(BF16), wg_ref[...]) + bg_ref[...])
        proj = _dot(p_ref[r0:r0 + rows, :].astype(BF16), wp_ref[...])
        o_ref[r0:r0 + rows, :] = _layer_norm(alpha * x + gate * proj, g_ref[...], b_ref[...])


def _ple(x, p, wg, bg, wp, ln_g, ln_b, layer, *, alpha, tm):
    n, d = x.shape
    pd = p.shape[-1]
    return pl.pallas_call(
        functools.partial(_ple_body, alpha=alpha),
        out_shape=jax.ShapeDtypeStruct((n, d), F32),
        grid=(n // tm,),
        in_specs=[
            pl.BlockSpec((tm, d), lambda i: (i, 0)),
            pl.BlockSpec((None, tm, pd), lambda i: (layer, i, 0)),
            pl.BlockSpec((None, d, d), lambda i: (layer, 0, 0)),
            pl.BlockSpec((None, 1, d), lambda i: (layer, 0, 0)),
            pl.BlockSpec((None, pd, d), lambda i: (layer, 0, 0)),
            pl.BlockSpec((None, None, 1, d), lambda i: (layer, 3, 0, 0)),
            pl.BlockSpec((None, None, 1, d), lambda i: (layer, 3, 0, 0)),
        ],
        out_specs=pl.BlockSpec((tm, d), lambda i: (i, 0)),
        compiler_params=_cparams(("parallel",)),
        name="ple_postnorm",
    )(x, p, wg, bg, wp, ln_g, ln_b)


def _oproj_body(x_ref, a_ref, w_ref, g_ref, b_ref, o_ref, *, alpha):
    rows = x_ref.shape[0] // ROW_GROUPS if x_ref.shape[0] % (ROW_GROUPS * 8) == 0 else x_ref.shape[0]
    for r0 in range(0, x_ref.shape[0], rows):
        mix = _dot(a_ref[r0:r0 + rows, :], w_ref[...])
        o_ref[r0:r0 + rows, :] = _layer_norm(alpha * x_ref[r0:r0 + rows, :] + mix, g_ref[...], b_ref[...])


def _oproj(x, a, w, ln_g, ln_b, layer, j, *, alpha, tm):
    n, d = x.shape
    ad = a.shape[-1]
    return pl.pallas_call(
        functools.partial(_oproj_body, alpha=alpha),
        out_shape=jax.ShapeDtypeStruct((n, d), F32),
        grid=(n // tm,),
        in_specs=[
            pl.BlockSpec((tm, d), lambda i: (i, 0)),
            pl.BlockSpec((tm, ad), lambda i: (i, 0)),
            pl.BlockSpec((None, ad, d), lambda i: (j, 0, 0)),
            pl.BlockSpec((None, None, 1, d), lambda i: (layer, 1, 0, 0)),
            pl.BlockSpec((None, None, 1, d), lambda i: (layer, 1, 0, 0)),
        ],
        out_specs=pl.BlockSpec((tm, d), lambda i: (i, 0)),
        compiler_params=_cparams(("parallel",)),
        name="oproj_postnorm",
    )(x, a, w, ln_g, ln_b)


def _rope_tables(positions, head_dim, kind):
    rd = head_dim // ROPE_FRACTION
    half = rd // 2
    inv_freq = (np.float32(ROPE_THETA) ** (-(np.arange(half, dtype=np.float32) / np.float32(half)))).astype(np.float32)
    ang = positions.astype(np.float32)[:, None] * inv_freq[None, :]
    cos, sin = np.cos(ang).astype(np.float32), np.sin(ang).astype(np.float32)
    npos = positions.shape[0]
    c = np.ones((npos, LANES), np.float32)
    sa = np.zeros((npos, LANES), np.float32)
    sb = np.zeros((npos, LANES), np.float32)
    starts = range(0, LANES, head_dim) if kind == "head" else [0]
    for h0 in starts:
        c[:, h0:h0 + half] = cos
        c[:, h0 + half:h0 + rd] = cos
        sa[:, h0:h0 + half] = -sin
        sb[:, h0 + half:h0 + rd] = sin
    if kind == "kiwi":
        c[:, IDX_DIM:IDX_DIM + N_IDX_HEADS] = (N_IDX_HEADS * IDX_DIM) ** -0.5
    return jnp.asarray(c), jnp.asarray(sa), jnp.asarray(sb), half


def _inproj_body(*refs, segs, n_tab, col_step):
    x_ref, w_ref = refs[0], refs[1]
    tab_refs = refs[2:2 + 3 * n_tab]
    out_refs = refs[2 + 3 * n_tab:]
    xb = x_ref[...].astype(BF16)
    oi = 0
    for (c0, c1, tab, half, scale, dtypes) in segs:
        outs = out_refs[oi:oi + len(dtypes)]
        oi += len(dtypes)
        for j0 in range(c0, c1, col_step):
            wj = min(col_step, c1 - j0)
            y = _dot(xb, w_ref[:, j0:j0 + wj])
            for c in range(wj // LANES):
                yc = y[:, c * LANES:(c + 1) * LANES]
                if tab is not None:
                    cc, sa, sb = (r[...] for r in tab_refs[3 * tab:3 * tab + 3])
                    yc = (yc * cc + pltpu.roll(yc, LANES - half, 1) * sa + pltpu.roll(yc, half, 1) * sb)
                if scale != 1.0:
                    yc = yc * scale
                lo = j0 - c0 + c * LANES
                for o_ref, dt in zip(outs, dtypes):
                    if dt == HEAD_MAJOR_T:
                        yt = yc.T
                        o_ref[lo // LANES, 0] = yt[:DIFF_DIM]
                        o_ref[lo // LANES, 1] = yt[DIFF_DIM:]
                    elif dt == ROW_INTERLEAVED:
                        nchunk = (c1 - c0) // LANES
                        o_ref[pl.ds(lo // LANES, yc.shape[0], stride=nchunk), :] = yc
                    elif dt == KEY_T:
                        o_ref[...] = yc.T[:IDX_DIM]
                    elif dt == KEY_TWICE:
                        lane = lax.broadcasted_iota(jnp.int32, yc.shape, 1)
                        o_ref[...] = jnp.where(lane < IDX_DIM, yc, pltpu.roll(yc, IDX_DIM, 1)).astype(BF16)
                    else:
                        o_ref[:, lo:lo + LANES] = yc.astype(dt)


def _inproj(x, w, tables, segs, *, tm, period_blocks, seq=None, w_layer=0):
    n, d = x.shape
    ctot = w.shape[-1]
    segs_full = tuple((c0, c1, tab, (tables[tab][3] if tab is not None else 0), scale, tuple(dts))
                      for (c0, c1, tab, scale, dts) in segs)
    w_spec = (pl.BlockSpec((d, ctot), lambda i: (0, 0), pipeline_mode=pl.Buffered(1)) if w.ndim == 2 else
              pl.BlockSpec((None, d, ctot), lambda i: (w_layer, 0, 0), pipeline_mode=pl.Buffered(1)))
    in_specs = [pl.BlockSpec((tm, d), lambda i: (i, 0)), w_spec]
    args = [x, w]
    for (c, sa, sb, _) in tables:
        for t in (c, sa, sb):
            in_specs.append(pl.BlockSpec((tm, LANES), lambda i: (i % period_blocks, 0)))
            args.append(t)
    out_shape, out_specs = [], []
    for (c0, c1, _, _, dts) in segs:
        for dt in dts:
            if dt == HEAD_MAJOR_T:
                heads = (c1 - c0) // (2 * DIFF_DIM)
                out_shape.append(jax.ShapeDtypeStruct((n // seq, heads, 2, DIFF_DIM, seq), F32))
                out_specs.append(pl.BlockSpec((None, heads, 2, DIFF_DIM, tm),
                                              lambda i: (i // (seq // tm), 0, 0, 0, i % (seq // tm))))
            elif dt == ROW_INTERLEAVED:
                nchunk = (c1 - c0) // LANES
                out_shape.append(jax.ShapeDtypeStruct((n * nchunk, LANES), F32))
                out_specs.append(pl.BlockSpec((tm * nchunk, LANES), lambda i: (i, 0)))
            elif dt == KEY_T:
                out_shape.append(jax.ShapeDtypeStruct((n // seq, IDX_DIM, seq), F32))
                out_specs.append(pl.BlockSpec((None, IDX_DIM, tm), lambda i: (i // (seq // tm), 0, i % (seq // tm))))
            elif dt == KEY_TWICE:
                out_shape.append(jax.ShapeDtypeStruct((n, LANES), BF16))
                out_specs.append(pl.BlockSpec((tm, LANES), lambda i: (i, 0)))
            else:
                out_shape.append(jax.ShapeDtypeStruct((n, c1 - c0), dt))
                out_specs.append(pl.BlockSpec((tm, c1 - c0), lambda i: (i, 0)))
    return pl.pallas_call(
        functools.partial(_inproj_body, segs=segs_full, n_tab=len(tables), col_step=512),
        out_shape=out_shape,
        grid=(n // tm,),
        in_specs=in_specs,
        out_specs=out_specs,
        compiler_params=_cparams(("parallel",)),
        name="inproj_rope",
    )(*args)


def _softmax_step(lhs_ref, m_ref, acc_ref, kblk, vext, bias_fn, *, rows, rc, tk):
    for r in range(rows // rc):
        r0, r1 = r * rc, (r + 1) * rc
        s = _dot_nt(lhs_ref[r0:r1, :], kblk)
        if bias_fn is not None:
            s = s + bias_fn(r)
        m_prev = m_ref[r0:r1, :]
        m_next = jnp.maximum(m_prev, jnp.max(s, axis=1, keepdims=True))
        a = jnp.exp2(m_prev - m_next)
        p = jnp.exp2(s - _lane_tile(m_next, tk // LANES))
        acc_ref[r0:r1, :] = acc_ref[r0:r1, :] * _lane_tile(a, 2) + _dot(p.astype(BF16), vext)
        m_ref[r0:r1, :] = m_next


def _sortable_key(score):
    score = jnp.where(score == 0.0, 0.0, score)
    bits = pltpu.bitcast(score, jnp.int32)
    return bits ^ (lax.shift_right_arithmetic(bits, 31) & 0x7FFFFFFF)


def _dsa_select_key_major(qi_ref, kiwi_ref, kid_ref, tril_ref, key_ref, bias_ref, qlhs_ref, wt_ref, *,
                          tq, tk, nkb, q_first, l_valid, topk):
    rc = LANES
    nrc = tq // rc
    lane128 = lax.broadcasted_iota(jnp.int32, (rc, LANES), 1)
    for rh in range(nrc):
        for c in range(N_IDX_HEADS // 2):
            qc = qi_ref[rh * rc:(rh + 1) * rc, c * LANES:(c + 1) * LANES]
            qlhs_ref[rh, c, :rc, :] = jnp.where(lane128 < IDX_DIM, qc, jnp.zeros_like(qc))
            qlhs_ref[rh, c, rc:, :] = jnp.where(lane128 >= IDX_DIM, qc, jnp.zeros_like(qc))
        kw_t = kiwi_ref[rh * rc:(rh + 1) * rc, :].T
        wt_ref[rh] = kw_t[IDX_DIM:IDX_DIM + N_IDX_HEADS, :]

    def idx_block(kb, carry):
        k0 = pl.multiple_of(kb * tk, tk)
        kblk = kid_ref[pl.ds(k0, tk), :]
        kpos = k0 + lax.broadcasted_iota(jnp.int32, (tk, rc), 0)
        k_ok = kpos < l_valid
        k_chunk = lax.shift_right_arithmetic(kpos, CHUNK_SHIFT)
        for rh in range(nrc):
            score = jnp.zeros((tk, rc), F32)
            for c in range(N_IDX_HEADS // 2):
                rel = jnp.maximum(_dot_nt(kblk, qlhs_ref[rh, c]), 0.0)
                score = (score + wt_ref[rh, 2 * c:2 * c + 1, :] * rel[:, :rc]
                         + wt_ref[rh, 2 * c + 1:2 * c + 2, :] * rel[:, rc:])
            q_chunk = lax.shift_right_arithmetic(
                q_first + rh * rc + lax.broadcasted_iota(jnp.int32, (tk, rc), 1), CHUNK_SHIFT)
            key_ref[kb, rh] = jnp.where((k_chunk <= q_chunk) & k_ok, _sortable_key(score), INT_MIN)
        return carry

    lax.fori_loop(0, nkb, idx_block, 0)
    key_ref[nkb] = jnp.full((nrc, tk, rc), INT_MIN, jnp.int32)

    n_acc = 4

    def count(preds):
        def cnt_pair(kp, accs):
            accs = [list(a) for a in accs]
            for kb in (2 * kp, 2 * kp + 1):
                for rh in range(nrc):
                    hit = jnp.where(preds[rh](key_ref[kb, rh]), 1.0, 0.0)
                    for j in range(tk // 8):
                        accs[rh][j % n_acc] = accs[rh][j % n_acc] + hit[j * 8:(j + 1) * 8, :]
            return tuple(tuple(a) for a in accs)

        zero = tuple(tuple(jnp.zeros((8, rc), F32) for _ in range(n_acc)) for _ in range(nrc))
        accs = lax.fori_loop(0, (nkb + 1) // 2, cnt_pair, zero)
        return [jnp.sum(sum(a[1:], a[0]), axis=0, keepdims=True) for a in accs]

    def bit_step(i, tus):
        bit = lax.shift_left(jnp.int32(1), 31 - i)
        cands_u = [tu | bit for tu in tus]
        totals = count([(lambda key, cs=cu ^ INT_MIN: key >= cs) for cu in cands_u])
        return tuple(jnp.where(tot >= topk, cu, tu) for tot, cu, tu in zip(totals, cands_u, tus))

    tus = lax.fori_loop(0, 32, bit_step, tuple(jnp.zeros((1, rc), jnp.int32) for _ in range(nrc)))
    thr = [tu ^ INT_MIN for tu in tus]
    n_gt = count([(lambda key, th=th: key > th) for th in thr])
    need = [topk - n for n in n_gt]

    def sel_block(kb, seen):
        new_seen = []
        for rh in range(nrc):
            key = key_ref[kb, rh]
            eq = key == thr[rh]
            eqf = jnp.where(eq, 1.0, 0.0)
            pref = _dot(tril_ref[...], eqf.astype(BF16)) + seen[rh]
            keep_tie = jnp.where(eq, jnp.where(pref <= need[rh], 1, 0), 0)
            sel = jnp.where(key > thr[rh], 1, keep_tie)
            sel = jnp.where(key == INT_MIN, 0, sel)
            bias_ref[kb, rh * rc:(rh + 1) * rc, :] = jnp.where(sel == 1, 0.0, NEG_BIG).T
            new_seen.append(pref[tk - 1:tk, :])
        return tuple(new_seen)

    lax.fori_loop(0, nkb, sel_block, tuple(jnp.zeros((1, rc), F32) for _ in range(nrc)))


def _dsa_body(q_ref, qi_ref, kiwi_ref, k_ref, v_ref, kid_ref, tri_ref, o_ref, key_ref, bias_ref,
              lhs_ref, m_ref, acc_ref, qlhs_ref, wb_ref, *, tq, tk, nq, l_valid, q_pos0, topk):
    t = pl.program_id(1)
    q_first = q_pos0 + t * tq
    q_last_chunk = (q_first + tq - 1) // CHUNK
    kmax = jnp.minimum(l_valid, (q_last_chunk + 1) * CHUNK)
    nkb = (kmax + tk - 1) // tk

    if tq % LANES == 0:
        _dsa_select_key_major(qi_ref, kiwi_ref, kid_ref, tri_ref, key_ref, bias_ref, qlhs_ref, wb_ref,
                              tq=tq, tk=tk, nkb=nkb, q_first=q_first, l_valid=l_valid, topk=topk)
    else:
        _dsa_select_query_major(qi_ref, kiwi_ref, kid_ref, tri_ref, key_ref, bias_ref, qlhs_ref, wb_ref,
                                tq=tq, tk=tk, nkb=nkb, q_first=q_first, l_valid=l_valid, topk=topk)
    _dsa_attend(q_ref, k_ref, v_ref, o_ref, bias_ref, lhs_ref, m_ref, acc_ref, tq=tq, tk=tk, nkb=nkb)


def _dsa_select_query_major(qi_ref, kiwi_ref, kid_ref, tri_ref, key_ref, bias_ref, qlhs_ref, wb_ref, *,
                            tq, tk, nkb, q_first, l_valid, topk):
    rc = min(ROW_CHUNK, tq)
    nrc = tq // rc
    nrep = tk // LANES

    lane128 = lax.broadcasted_iota(jnp.int32, (rc, LANES), 1)
    for rh in range(nrc):
        for c in range(N_IDX_HEADS // 2):
            qc = qi_ref[rh * rc:(rh + 1) * rc, c * LANES:(c + 1) * LANES]
            qlhs_ref[rh, c, :rc, :] = jnp.where(lane128 < IDX_DIM, qc, jnp.zeros_like(qc))
            qlhs_ref[rh, c, rc:, :] = jnp.where(lane128 >= IDX_DIM, qc, jnp.zeros_like(qc))
        wi = kiwi_ref[rh * rc:(rh + 1) * rc, IDX_DIM:IDX_DIM + N_IDX_HEADS]
        for h in range(N_IDX_HEADS):
            wb_ref[rh, h] = jnp.broadcast_to(wi[:, h:h + 1], (rc, LANES))

    def idx_block(kb, carry):
        k0 = pl.multiple_of(kb * tk, tk)
        kblk = kid_ref[pl.ds(k0, tk), :]
        kpos = k0 + lax.broadcasted_iota(jnp.int32, (rc, tk), 1)
        for rh in range(nrc):
            score = jnp.zeros((rc, tk), F32)
            for c in range(N_IDX_HEADS // 2):
                rel = jnp.maximum(_dot_nt(qlhs_ref[rh, c], kblk), 0.0)
                score = (score + _lane_tile(wb_ref[rh, 2 * c], nrep) * rel[:rc]
                         + _lane_tile(wb_ref[rh, 2 * c + 1], nrep) * rel[rc:])
            q_chunk = lax.shift_right_arithmetic(
                q_first + rh * rc + lax.broadcasted_iota(jnp.int32, (rc, tk), 0), CHUNK_SHIFT)
            adm = (lax.shift_right_arithmetic(kpos, CHUNK_SHIFT) <= q_chunk) & (kpos < l_valid)
            key_ref[kb, rh * rc:(rh + 1) * rc, :] = jnp.where(adm, _sortable_key(score), INT_MIN)
        return carry

    lax.fori_loop(0, nkb, idx_block, 0)
    key_ref[nkb] = jnp.full((tq, tk), INT_MIN, jnp.int32)

    def lane_counts(rh, pred):
        def cnt_pair(kp, cnt):
            for kb in (2 * kp, 2 * kp + 1):
                for c in range(nrep):
                    cnt = cnt + jnp.where(pred(key_ref[kb, rh * rc:(rh + 1) * rc, c * LANES:(c + 1) * LANES]), 1.0, 0.0)
            return cnt
        return lax.fori_loop(0, (nkb + 1) // 2, cnt_pair, jnp.zeros((rc, LANES), F32))

    def count(preds):
        partial = [lane_counts(rh, preds[rh]) for rh in range(nrc)]
        return [jnp.sum(c, axis=1, keepdims=True) for c in partial]

    def bit_step(i, tus):
        bit = lax.shift_left(jnp.int32(1), 31 - i)
        cands_u = [tu | bit for tu in tus]
        totals = count([(lambda key, cs=cu ^ INT_MIN: key >= cs) for cu in cands_u])
        return tuple(jnp.where(tot >= topk, cu, tu) for tot, cu, tu in zip(totals, cands_u, tus))

    tus = lax.fori_loop(0, 32, bit_step, tuple(jnp.zeros((rc, LANES), jnp.int32) for _ in range(nrc)))
    thr128 = [tu ^ INT_MIN for tu in tus]
    n_gt = count([(lambda key, th=th: key > th) for th in thr128])
    thr = thr128[0][:, :1] if nrc == 1 else jnp.concatenate([th[:, :1] for th in thr128], axis=0)
    need = topk - (n_gt[0] if nrc == 1 else jnp.concatenate(n_gt, axis=0))

    def sel_block(kb, seen):
        key = key_ref[kb]
        eq = key == thr
        eqf = jnp.where(eq, 1.0, 0.0)
        pref = _dot(eqf.astype(BF16), tri_ref[...]) + seen
        keep_tie = jnp.where(eq, jnp.where(pref <= need, 1, 0), 0)
        sel = jnp.where(key > thr, 1, keep_tie)
        sel = jnp.where(key == INT_MIN, 0, sel)
        bias_ref[kb] = jnp.where(sel == 1, 0.0, NEG_BIG)
        return pref[:, tk - 1:tk]

    lax.fori_loop(0, nkb, sel_block, jnp.zeros((tq, 1), F32))


def _dsa_attend(q_ref, k_ref, v_ref, o_ref, bias_ref, lhs_ref, m_ref, acc_ref, *, tq, tk, nkb):
    rep = N_HEADS_A // N_KV_HEADS_A
    rc = min(ROW_CHUNK, tq)
    ones = jnp.ones((tk, LANES), BF16)
    for g in range(N_KV_HEADS_A):
        for r in range(rep):
            lhs_ref[g, r * tq:(r + 1) * tq, :] = q_ref[:, (g * rep + r) * HEAD_DIM:(g * rep + r + 1) * HEAD_DIM]
    m_ref[...] = jnp.full(m_ref.shape, NEG_BIG, F32)
    acc_ref[...] = jnp.zeros(acc_ref.shape, F32)

    def kv_block(kb, c):
        k0 = pl.multiple_of(kb * tk, tk)
        for g in range(N_KV_HEADS_A):
            kblk = k_ref[pl.ds(k0, tk), g * HEAD_DIM:(g + 1) * HEAD_DIM]
            vext = jnp.concatenate([v_ref[pl.ds(k0, tk), g * HEAD_DIM:(g + 1) * HEAD_DIM], ones], axis=1)
            _softmax_step(lhs_ref.at[g], m_ref.at[g], acc_ref.at[g], kblk, vext,
                          lambda r: bias_ref[kb, (r * rc) % tq:(r * rc) % tq + rc, :],
                          rows=rep * tq, rc=rc, tk=tk)
        return c

    lax.fori_loop(0, nkb, kv_block, 0)
    for g in range(N_KV_HEADS_A):
        acc = acc_ref[g]
        o = acc[:, :HEAD_DIM] / acc[:, HEAD_DIM:]
        for r in range(rep):
            h0 = (g * rep + r) * HEAD_DIM
            o_ref[:, h0:h0 + HEAD_DIM] = o[r * tq:(r + 1) * tq].astype(o_ref.dtype)


def _dsa_attention(q, qi, kiwi, k_all, v_all, kid_all, *, batch, lq, tq, tk, l_valid, q_pos0, topk):
    lk_pad = k_all.shape[1]
    nq = lq // tq
    nkb_max = lk_pad // tk
    a_q = N_HEADS_A * HEAD_DIM
    a_kv = N_KV_HEADS_A * HEAD_DIM
    rep = N_HEADS_A // N_KV_HEADS_A
    rc = min(ROW_CHUNK, tq)
    ones_tri = np.triu(np.ones((tk, tk), np.float32))
    if tq % LANES == 0:
        tri = jnp.asarray(ones_tri.T, BF16)
        key_scratch = pltpu.VMEM((nkb_max + 1, tq // LANES, tk, LANES), jnp.int32)
        w_scratch = pltpu.VMEM((tq // LANES, N_IDX_HEADS, LANES), F32)
    else:
        tri = jnp.asarray(ones_tri, BF16)
        key_scratch = pltpu.VMEM((nkb_max + 1, tq, tk), jnp.int32)
        w_scratch = pltpu.VMEM((tq // rc, N_IDX_HEADS, rc, LANES), F32)
    return pl.pallas_call(
        functools.partial(_dsa_body, tq=tq, tk=tk, nq=nq, l_valid=l_valid, q_pos0=q_pos0, topk=topk),
        out_shape=jax.ShapeDtypeStruct((batch * lq, a_q), BF16),
        grid=(batch, nq),
        in_specs=[
            pl.BlockSpec((tq, a_q), lambda b, t: (b * nq + t, 0)),
            pl.BlockSpec((tq, N_IDX_HEADS * IDX_DIM), lambda b, t: (b * nq + t, 0)),
            pl.BlockSpec((tq, LANES), lambda b, t: (b * nq + t, 0)),
            pl.BlockSpec((None, lk_pad, a_kv), lambda b, t: (b, 0, 0)),
            pl.BlockSpec((None, lk_pad, a_kv), lambda b, t: (b, 0, 0)),
            pl.BlockSpec((None, lk_pad, LANES), lambda b, t: (b, 0, 0)),
            pl.BlockSpec((tk, tk), lambda b, t: (0, 0)),
        ],
        out_specs=pl.BlockSpec((tq, a_q), lambda b, t: (b * nq + t, 0)),
        scratch_shapes=[key_scratch, pltpu.VMEM((nkb_max, tq, tk), F32),
                        pltpu.VMEM((N_KV_HEADS_A, rep * tq, HEAD_DIM), BF16),
                        pltpu.VMEM((N_KV_HEADS_A, rep * tq, LANES), F32),
                        pltpu.VMEM((N_KV_HEADS_A, rep * tq, 2 * HEAD_DIM), F32),
                        pltpu.VMEM((tq // rc, N_IDX_HEADS // 2, 2 * rc, LANES), BF16),
                        w_scratch],
        compiler_params=_cparams(("parallel", "arbitrary")),
        name="dsa_attention",
    )(q, qi, kiwi, k_all, v_all, kid_all, tri)


def _diff_body(q_ref, k_ref, v_ref, lq1_ref, lk1_ref, lq2_ref, lk2_ref, g_ref, o_ref, lhs_ref, m_ref, acc_ref, *,
               tq, tk, nq, l_valid, q_pos0, lam_init, rc, heads):
    lam = (jnp.exp(jnp.sum(lq1_ref[...] * lk1_ref[...], axis=1, keepdims=True))
           - jnp.exp(jnp.sum(lq2_ref[...] * lk2_ref[...], axis=1, keepdims=True)) + lam_init)
    lane = lax.broadcasted_iota(jnp.int32, (tq, LANES), 1)
    ones = jnp.ones((tk, LANES), BF16)
    hd = 2 * DIFF_DIM

    def tile(t, carry):
        q0 = pl.multiple_of(t * tq, tq)
        q_first = q_pos0 + t * tq
        for h in range(heads):
            qc = q_ref[pl.ds(q0, tq), h * hd:(h + 1) * hd]
            lhs_ref[h, :tq, :] = jnp.where(lane < DIFF_DIM, qc, jnp.zeros_like(qc))
            lhs_ref[h, tq:, :] = jnp.where(lane >= DIFF_DIM, qc, jnp.zeros_like(qc))
        m_ref[...] = jnp.full(m_ref.shape, NEG_BIG, F32)
        acc_ref[...] = jnp.zeros(acc_ref.shape, F32)
        nfull = jnp.minimum(((q_first // CHUNK + 1) * CHUNK) // tk, l_valid // tk)
        nkb = (jnp.minimum(l_valid, ((q_first + tq - 1) // CHUNK + 1) * CHUNK) + tk - 1) // tk

        def step(kb, c, masked):
            k0 = pl.multiple_of(kb * tk, tk)

            def bias_fn(r):
                row0 = q_first + (r * rc) % tq
                q_chunk = lax.shift_right_arithmetic(row0 + lax.broadcasted_iota(jnp.int32, (rc, tk), 0), CHUNK_SHIFT)
                kpos = k0 + lax.broadcasted_iota(jnp.int32, (rc, tk), 1)
                adm = (lax.shift_right_arithmetic(kpos, CHUNK_SHIFT) <= q_chunk) & (kpos < l_valid)
                return jnp.where(adm, 0.0, NEG_BIG)

            for h in range(heads):
                kblk = k_ref[pl.ds(k0, tk), h * hd:(h + 1) * hd]
                vext = jnp.concatenate([v_ref[pl.ds(k0, tk), h * hd:(h + 1) * hd], ones], axis=1)
                _softmax_step(lhs_ref.at[h], m_ref.at[h], acc_ref.at[h], kblk, vext, bias_fn if masked else None,
                              rows=2 * tq, rc=rc, tk=tk)
            return c

        lax.fori_loop(0, nfull, functools.partial(step, masked=False), 0)
        lax.fori_loop(nfull, nkb, functools.partial(step, masked=True), 0)
        for h in range(heads):
            acc = acc_ref[h]
            o = acc[:, :hd] / acc[:, hd:]
            o = o[:tq] - lam * o[tq:]
            o = o * lax.rsqrt(jnp.mean(o * o, axis=-1, keepdims=True) + LN_EPS) * g_ref[...] * (1.0 - lam_init)
            o_ref[pl.ds(q0, tq), h * hd:(h + 1) * hd] = o.astype(o_ref.dtype)
        return carry

    lax.fori_loop(0, nq, tile, 0)


def _diff_attention(q, k_all, v_all, lq1, lk1, lq2, lk2, g, j, *, batch, lq, tq, tk, l_valid, q_pos0, lam_init):
    lk_pad = k_all.shape[1]
    nq = lq // tq
    hd = 2 * DIFF_DIM
    heads = DIFF_HEADS_PER_STEP
    rc = min(ROW_CHUNK, 2 * tq)
    single = pl.Buffered(1)
    lam_spec = pl.BlockSpec((None, 1, DIFF_DIM), lambda b, h: (j, 0, 0))
    return pl.pallas_call(
        functools.partial(_diff_body, tq=tq, tk=tk, nq=nq, l_valid=l_valid, q_pos0=q_pos0, lam_init=lam_init,
                          rc=rc, heads=heads),
        out_shape=jax.ShapeDtypeStruct((batch * lq, N_HEADS_B * hd), BF16),
        grid=(batch, N_HEADS_B // heads),
        in_specs=[
            pl.BlockSpec((lq, heads * hd), lambda b, h: (b, h), pipeline_mode=single),
            pl.BlockSpec((None, lk_pad, heads * hd), lambda b, h: (b, 0, h), pipeline_mode=single),
            pl.BlockSpec((None, lk_pad, heads * hd), lambda b, h: (b, 0, h), pipeline_mode=single),
            lam_spec, lam_spec, lam_spec, lam_spec,
            pl.BlockSpec((None, 1, hd), lambda b, h: (j, 0, 0)),
        ],
        out_specs=pl.BlockSpec((lq, heads * hd), lambda b, h: (b, h), pipeline_mode=single),
        scratch_shapes=[pltpu.VMEM((heads, 2 * tq, hd), BF16), pltpu.VMEM((heads, 2 * tq, LANES), F32),
                        pltpu.VMEM((heads, 2 * tq, 2 * hd), F32)],
        compiler_params=_cparams(("parallel", "parallel")),
        name="diff_attention",
    )(q, k_all, v_all, lq1, lk1, lq2, lk2, g)


def _pad_keys(x, lk_pad):
    pad = lk_pad - x.shape[1]
    return x if pad == 0 else jnp.pad(x, ((0, 0), (0, pad), (0, 0)))


def _trunk(x, p, pos0, pasts, w, *, tm):
    batch, seq, d = x.shape
    n = batch * seq
    depth = w["ln_g"].shape[0]
    alpha = float((2 * depth) ** 0.25)
    a_q = N_HEADS_A * HEAD_DIM
    a_kv = N_KV_HEADS_A * HEAD_DIM
    a_qi = N_IDX_HEADS * IDX_DIM
    b_qk = N_HEADS_B * 2 * DIFF_DIM

    period = max(1, seq // tm)
    table_rows = max(seq, tm)
    positions = (pos0 + np.arange(seq)).astype(np.int64)
    positions = np.tile(positions, table_rows // seq)
    tab128 = _rope_tables(positions, HEAD_DIM, "head")
    tab64 = _rope_tables(positions, IDX_DIM, "head")
    tabkw = _rope_tables(positions, IDX_DIM, "kiwi")

    xf = x.reshape(n, d)
    pf = p.reshape(depth, n, p.shape[-1])
    tf = 512 if w["ffn_w_gate"].shape[-1] % 512 == 0 else 128
    rows = []
    for i in range(depth):
        j = i // 2
        xf = _ffn(xf, w["ffn_w_gate"], w["ffn_w_up"], w["ffn_w_down"], w["ln_g"], w["ln_b"], i, 0, 0,
                  alpha=alpha, tm=tm, tf=tf)
        past = pasts[i]
        lk_new = seq if past is None else past[0].shape[1] + seq
        if i % 2 == 0:
            wa = w["a_w_in"]
            c_k, c_v, c_qi, c_kw = a_q, a_q + a_kv, a_q + 2 * a_kv, a_q + 2 * a_kv + a_qi
            kw_kinds = (F32, KEY_T, KEY_TWICE) if seq % tm == 0 else (F32,)
            outs = _inproj(xf, wa, [tab128, tab64, tabkw],
                           [(0, c_k, 0, HEAD_DIM ** -0.5 * LOG2_E, (BF16,)),
                            (c_k, c_v, 0, 1.0, (ROW_INTERLEAVED, BF16)),
                            (c_v, c_qi, None, 1.0, (ROW_INTERLEAVED, BF16)),
                            (c_qi, c_kw, 1, 1.0, (BF16,)),
                            (c_kw, c_kw + LANES, 2, 1.0, kw_kinds)],
                           tm=tm, period_blocks=period, seq=seq, w_layer=j)
            q, k, kb, v, vb, qi, kiwi = outs[:7]
            k_all, v_all = kb.reshape(batch, seq, a_kv), vb.reshape(batch, seq, a_kv)
            if seq % tm == 0:
                ki_rows = jnp.transpose(outs[7], (0, 2, 1))
                kid_all = outs[8].reshape(batch, seq, LANES)
            else:
                ki_rows = kiwi[:, :IDX_DIM].reshape(batch, seq, IDX_DIM)
                kid_all = jnp.concatenate([ki_rows, ki_rows], axis=-1).astype(BF16)
            rows.append((k.reshape(batch, seq, N_KV_HEADS_A, HEAD_DIM), v.reshape(batch, seq, N_KV_HEADS_A, HEAD_DIM),
                         ki_rows))
            if past is not None:
                pk, pv, pki = past
                k_all = jnp.concatenate([pk.reshape(batch, -1, a_kv).astype(BF16), k_all], axis=1)
                v_all = jnp.concatenate([pv.reshape(batch, -1, a_kv).astype(BF16), v_all], axis=1)
                kid_all = jnp.concatenate([jnp.concatenate([pki, pki], axis=-1).astype(BF16), kid_all], axis=1)
            tq = min(DSA_Q_TILE, seq)
            tk = KEY_BLOCK
            lk_pad = -(-lk_new // tk) * tk
            topk = min(TOPK_MAX, lk_new // 4)
            o = _dsa_attention(q, qi, kiwi, _pad_keys(k_all, lk_pad), _pad_keys(v_all, lk_pad), _pad_keys(kid_all, lk_pad),
                               batch=batch, lq=seq, tq=tq, tk=tk, l_valid=lk_new, q_pos0=pos0, topk=topk)
            xf = _oproj(xf, o, w["a_w_out"], w["ln_g"], w["ln_b"], i, j, alpha=alpha, tm=tm)
        else:
            wb = w["b_w_in"]
            lam_init = 0.8 - 0.6 * math.exp(-0.3 * i)
            k_kinds = (HEAD_MAJOR_T, BF16) if seq % tm == 0 else (F32, BF16)
            q, k, kb = _inproj(xf, wb[j, :, :2 * b_qk], [tab64],
                               [(0, b_qk, 0, DIFF_DIM ** -0.5 * LOG2_E, (BF16,)), (b_qk, 2 * b_qk, 0, 1.0, k_kinds)],
                               tm=tm, period_blocks=period, seq=seq)
            k_rows = (jnp.transpose(k, (0, 4, 1, 2, 3)) if seq % tm == 0
                      else k.reshape(batch, seq, N_HEADS_B, 2, DIFF_DIM))
            v, vb = _inproj(xf, wb[j, :, 2 * b_qk:], [], [(0, b_qk, None, 1.0, (F32, BF16))],
                            tm=tm, period_blocks=period)
            rows.append((k_rows, v.reshape(batch, seq, N_HEADS_B, 2 * DIFF_DIM)))
            k_all, v_all = kb.reshape(batch, seq, b_qk), vb.reshape(batch, seq, b_qk)
            if past is not None:
                pk, pv = past
                k_all = jnp.concatenate([pk.reshape(batch, -1, b_qk).astype(BF16), k_all], axis=1)
                v_all = jnp.concatenate([pv.reshape(batch, -1, b_qk).astype(BF16), v_all], axis=1)
            tq = min(256, seq)
            tk = KEY_BLOCK
            lk_pad = -(-lk_new // tk) * tk
            o = _diff_attention(q, _pad_keys(k_all, lk_pad), _pad_keys(v_all, lk_pad),
                                w["b_lambda_q1"], w["b_lambda_k1"], w["b_lambda_q2"], w["b_lambda_k2"], w["b_subln"], j,
                                batch=batch, lq=seq, tq=tq, tk=tk, l_valid=lk_new, q_pos0=pos0, lam_init=lam_init)
            xf = _oproj(xf, o, w["b_w_out"], w["ln_g"], w["ln_b"], i, j, alpha=alpha, tm=tm)
        xf = _ffn(xf, w["ffn_w_gate"], w["ffn_w_up"], w["ffn_w_down"], w["ln_g"], w["ln_b"], i, 1, 2,
                  alpha=alpha, tm=tm, tf=tf)
        xf = _ple(xf, pf, w["ple_w_gate"], w["ple_b_gate"], w["ple_w_proj"], w["ln_g"], w["ln_b"], i,
                  alpha=alpha, tm=tm)
    return xf.reshape(batch, seq, d), rows


def kernel(x_prompt, x_sample, cache_l0_k, cache_l0_v, cache_l0_kidx, cache_l1_k, cache_l1_v, cache_l2_k, cache_l2_v, cache_l2_kidx, cache_l3_k, cache_l3_v, p_prompt, p_sample, ln_g, ln_b, ffn_w_gate, ffn_w_up, ffn_w_down, ple_w_gate, ple_b_gate, ple_w_proj, a_w_in, a_w_out, b_w_in, b_w_out, b_lambda_q1, b_lambda_k1, b_lambda_q2, b_lambda_k2, b_subln):
    depth, _, d = ln_g.shape
    a_cols = a_w_in.shape[-1]
    a_pad = -(-a_cols // LANES) * LANES - a_cols
    w = {
        "ln_g": ln_g.reshape(depth, 4, 1, d),
        "ln_b": ln_b.reshape(depth, 4, 1, d),
        "ffn_w_gate": ffn_w_gate.astype(BF16),
        "ffn_w_up": ffn_w_up.astype(BF16),
        "ffn_w_down": (0.5 * ffn_w_down).astype(BF16),
        "ple_w_gate": ple_w_gate.astype(BF16),
        "ple_b_gate": ple_b_gate.reshape(depth, 1, d),
        "ple_w_proj": ple_w_proj.astype(BF16),
        "a_w_in": jnp.pad(a_w_in, ((0, 0), (0, 0), (0, a_pad))).astype(BF16),
        "a_w_out": a_w_out.astype(BF16),
        "b_w_in": b_w_in.astype(BF16),
        "b_w_out": b_w_out.astype(BF16),
        "b_lambda_q1": b_lambda_q1.reshape(-1, 1, DIFF_DIM),
        "b_lambda_k1": b_lambda_k1.reshape(-1, 1, DIFF_DIM),
        "b_lambda_q2": b_lambda_q2.reshape(-1, 1, DIFF_DIM),
        "b_lambda_k2": b_lambda_k2.reshape(-1, 1, DIFF_DIM),
        "b_subln": b_subln.reshape(-1, 1, 2 * DIFF_DIM),
    }
    past_len = cache_l0_k.shape[1]
    pasts_s = [(cache_l0_k, cache_l0_v, cache_l0_kidx), (cache_l1_k, cache_l1_v),
               (cache_l2_k, cache_l2_v, cache_l2_kidx), (cache_l3_k, cache_l3_v)]
    n_p = x_prompt.shape[0] * x_prompt.shape[1]
    n_s = x_sample.shape[0] * x_sample.shape[1]
    y_p, rows_p = _trunk(x_prompt, p_prompt, 0, [None] * depth, w, tm=min(512, n_p))
    y_s, rows_s = _trunk(x_sample, p_sample, past_len, pasts_s, w, tm=min(256, n_s))
    out = [y_p, y_s]
    for rp, rs in zip(rows_p, rows_s):
        out.extend(rp)
        out.extend(rs)
    return tuple(out)
```

```python
import functools
import math

import numpy as np
import jax
import jax.numpy as jnp
from jax import lax
from jax.experimental import pallas as pl
from jax.experimental.pallas import tpu as pltpu

CHUNK = 64
CHUNK_SHIFT = 6
HEAD_DIM = 128
N_HEADS_A = 16
N_KV_HEADS_A = 4
N_IDX_HEADS = 16
IDX_DIM = 64
TOPK_MAX = 256
N_HEADS_B = 16
DIFF_DIM = 64
ROPE_FRACTION = 4
ROPE_THETA = 500000.0
LN_EPS = 1e-5

LANES = 128
KEY_BLOCK = 256
ROW_CHUNK = 128
DIFF_HEADS_PER_STEP = 16
DIFF_PAST_HEADS_PER_STEP = 8
DSA_Q_TILE = 256
ROW_GROUPS = 2
LOG2_E = 1.4426950408889634
VMEM_LIMIT_BYTES = 56 * 1024 * 1024
NEG_BIG = -1e30
INT_MIN = -(2 ** 31)

F32 = jnp.float32
BF16 = jnp.bfloat16
HEAD_MAJOR_T = "f32 (batch, head, map, 64, seq)"
ROW_INTERLEAVED = "f32 (rows * chunks, 128)"
KEY_T = "f32 (batch, 64, seq)"
KEY_TWICE = "bf16 (rows, 128) = [ki | ki]"


def _cparams(sem):
    return pltpu.CompilerParams(dimension_semantics=sem, vmem_limit_bytes=VMEM_LIMIT_BYTES)


def _layer_norm(z, g, b):
    mu = jnp.mean(z, axis=-1, keepdims=True)
    zc = z - mu
    var = jnp.mean(zc * zc, axis=-1, keepdims=True)
    return zc * lax.rsqrt(var + LN_EPS) * g + b


def _dot(a, b):
    return jnp.dot(a, b, preferred_element_type=F32)


def _lane_tile(x, n):
    return x if n == 1 else jnp.concatenate([x] * n, axis=1)


def _dot_nt(a, b):
    return lax.dot_general(a, b, (((1,), (1,)), ((), ())), preferred_element_type=F32)


def _ffn_body(x_ref, wg_ref, wu_ref, wd_ref, g_ref, b_ref, o_ref, xb_ref, *, alpha, nf):
    f = pl.program_id(1)

    @pl.when(f == 0)
    def _():
        x = x_ref[...]
        xb_ref[...] = x.astype(BF16)
        o_ref[...] = alpha * x

    xb = xb_ref[...]
    gate = _dot(xb, wg_ref[...])
    up = _dot(xb, wu_ref[...])
    h = (gate * jax.nn.sigmoid(gate) * up).astype(BF16)
    o_ref[...] += _dot(h, wd_ref[...])

    @pl.when(f == nf - 1)
    def _():
        o_ref[...] = _layer_norm(o_ref[...], g_ref[...], b_ref[...])


def _ffn(x, wg, wu, wd, ln_g, ln_b, layer, sub, ln_idx, *, alpha, tm, tf):
    n, d = x.shape
    f_dim = wg.shape[-1]
    nf = f_dim // tf
    return pl.pallas_call(
        functools.partial(_ffn_body, alpha=alpha, nf=nf),
        out_shape=jax.ShapeDtypeStruct((n, d), F32),
        grid=(n // tm, nf),
        in_specs=[
            pl.BlockSpec((tm, d), lambda i, f: (i, 0)),
            pl.BlockSpec((None, None, d, tf), lambda i, f: (layer, sub, 0, f)),
            pl.BlockSpec((None, None, d, tf), lambda i, f: (layer, sub, 0, f)),
            pl.BlockSpec((None, None, tf, d), lambda i, f: (layer, sub, f, 0)),
            pl.BlockSpec((None, None, 1, d), lambda i, f: (layer, ln_idx, 0, 0)),
            pl.BlockSpec((None, None, 1, d), lambda i, f: (layer, ln_idx, 0, 0)),
        ],
        out_specs=pl.BlockSpec((tm, d), lambda i, f: (i, 0)),
        scratch_shapes=[pltpu.VMEM((tm, d), BF16)],
        compiler_params=_cparams(("parallel", "arbitrary")),
        name="ffn_postnorm",
    )(x, wg, wu, wd, ln_g, ln_b)


def _ple_body(x_ref, p_ref, wg_ref, bg_ref, wp_ref, g_ref, b_ref, o_ref, *, alpha):
    rows = x_ref.shape[0] // ROW_GROUPS if x_ref.shape[0] % (ROW_GROUPS * 8) == 0 else x_ref.shape[0]
    for r0 in range(0, x_ref.shape[0], rows):
        x = x_ref[r0:r0 + rows, :]
        gate = jax.nn.sigmoid(_dot(x.astype(BF16), wg_ref[...]) + bg_ref[...])
        proj = _dot(p_ref[r0:r0 + rows, :].astype(BF16), wp_ref[...])
        o_ref[r0:r0 + rows, :] = _layer_norm(alpha * x + gate * proj, g_ref[...], b_ref[...])


def _ple(x, p, wg, bg, wp, ln_g, ln_b, layer, *, alpha, tm):
    n, d = x.shape
    pd = p.shape[-1]
    return pl.pallas_call(
        functools.partial(_ple_body, alpha=alpha),
        out_shape=jax.ShapeDtypeStruct((n, d), F32),
        grid=(n // tm,),
        in_specs=[
            pl.BlockSpec((tm, d), lambda i: (i, 0)),
            pl.BlockSpec((None, tm, pd), lambda i: (layer, i, 0)),
            pl.BlockSpec((None, d, d), lambda i: (layer, 0, 0)),
            pl.BlockSpec((None, 1, d), lambda i: (layer, 0, 0)),
            pl.BlockSpec((None, pd, d), lambda i: (layer, 0, 0)),
            pl.BlockSpec((None, None, 1, d), lambda i: (layer, 3, 0, 0)),
            pl.BlockSpec((None, None, 1, d), lambda i: (layer, 3, 0, 0)),
        ],
        out_specs=pl.BlockSpec((tm, d), lambda i: (i, 0)),
        compiler_params=_cparams(("parallel",)),
        name="ple_postnorm",
    )(x, p, wg, bg, wp, ln_g, ln_b)


def _oproj_body(x_ref, a_ref, w_ref, g_ref, b_ref, o_ref, *, alpha):
    rows = x_ref.shape[0] // ROW_GROUPS if x_ref.shape[0] % (ROW_GROUPS * 8) == 0 else x_ref.shape[0]
    for r0 in range(0, x_ref.shape[0], rows):
        mix = _dot(a_ref[r0:r0 + rows, :], w_ref[...])
        o_ref[r0:r0 + rows, :] = _layer_norm(alpha * x_ref[r0:r0 + rows, :] + mix, g_ref[...], b_ref[...])


def _oproj(x, a, w, ln_g, ln_b, layer, j, *, alpha, tm):
    n, d = x.shape
    ad = a.shape[-1]
    return pl.pallas_call(
        functools.partial(_oproj_body, alpha=alpha),
        out_shape=jax.ShapeDtypeStruct((n, d), F32),
        grid=(n // tm,),
        in_specs=[
            pl.BlockSpec((tm, d), lambda i: (i, 0)),
            pl.BlockSpec((tm, ad), lambda i: (i, 0)),
            pl.BlockSpec((None, ad, d), lambda i: (j, 0, 0)),
            pl.BlockSpec((None, None, 1, d), lambda i: (layer, 1, 0, 0)),
            pl.BlockSpec((None, None, 1, d), lambda i: (layer, 1, 0, 0)),
        ],
        out_specs=pl.BlockSpec((tm, d), lambda i: (i, 0)),
        compiler_params=_cparams(("parallel",)),
        name="oproj_postnorm",
    )(x, a, w, ln_g, ln_b)


def _rope_tables(positions, head_dim, kind):
    rd = head_dim // ROPE_FRACTION
    half = rd // 2
    inv_freq = (np.float32(ROPE_THETA) ** (-(np.arange(half, dtype=np.float32) / np.float32(half)))).astype(np.float32)
    ang = positions.astype(np.float32)[:, None] * inv_freq[None, :]
    cos, sin = np.cos(ang).astype(np.float32), np.sin(ang).astype(np.float32)
    npos = positions.shape[0]
    c = np.ones((npos, LANES), np.float32)
    sa = np.zeros((npos, LANES), np.float32)
    sb = np.zeros((npos, LANES), np.float32)
    starts = range(0, LANES, head_dim) if kind == "head" else [0]
    for h0 in starts:
        c[:, h0:h0 + half] = cos
        c[:, h0 + half:h0 + rd] = cos
        sa[:, h0:h0 + half] = -sin
        sb[:, h0 + half:h0 + rd] = sin
    if kind == "kiwi":
        c[:, IDX_DIM:IDX_DIM + N_IDX_HEADS] = (N_IDX_HEADS * IDX_DIM) ** -0.5
    return jnp.asarray(c), jnp.asarray(sa), jnp.asarray(sb), half


def _inproj_body(*refs, segs, n_tab, col_step):
    x_ref, w_ref = refs[0], refs[1]
    tab_refs = refs[2:2 + 3 * n_tab]
    out_refs = refs[2 + 3 * n_tab:]
    xb = x_ref[...].astype(BF16)
    oi = 0
    for (c0, c1, tab, half, scale, dtypes) in segs:
        outs = out_refs[oi:oi + len(dtypes)]
        oi += len(dtypes)
        for j0 in range(c0, c1, col_step):
            wj = min(col_step, c1 - j0)
            y = _dot(xb, w_ref[:, j0:j0 + wj])
            for c in range(wj // LANES):
                yc = y[:, c * LANES:(c + 1) * LANES]
                if tab is not None:
                    cc, sa, sb = (r[...] for r in tab_refs[3 * tab:3 * tab + 3])
                    yc = (yc * cc + pltpu.roll(yc, LANES - half, 1) * sa + pltpu.roll(yc, half, 1) * sb)
                if scale != 1.0:
                    yc = yc * scale
                lo = j0 - c0 + c * LANES
                for o_ref, dt in zip(outs, dtypes):
                    if dt == HEAD_MAJOR_T:
                        yt = yc.T
                        o_ref[lo // LANES, 0] = yt[:DIFF_DIM]
                        o_ref[lo // LANES, 1] = yt[DIFF_DIM:]
                    elif dt == ROW_INTERLEAVED:
                        nchunk = (c1 - c0) // LANES
                        o_ref[pl.ds(lo // LANES, yc.shape[0], stride=nchunk), :] = yc
                    elif dt == KEY_T:
                        o_ref[...] = yc.T[:IDX_DIM]
                    elif dt == KEY_TWICE:
                        lane = lax.broadcasted_iota(jnp.int32, yc.shape, 1)
                        o_ref[...] = jnp.where(lane < IDX_DIM, yc, pltpu.roll(yc, IDX_DIM, 1)).astype(BF16)
                    else:
                        o_ref[:, lo:lo + LANES] = yc.astype(dt)


def _inproj(x, w, tables, segs, *, tm, period_blocks, seq=None, w_layer=0):
    n, d = x.shape
    ctot = w.shape[-1]
    segs_full = tuple((c0, c1, tab, (tables[tab][3] if tab is not None else 0), scale, tuple(dts))
                      for (c0, c1, tab, scale, dts) in segs)
    w_spec = (pl.BlockSpec((d, ctot), lambda i: (0, 0), pipeline_mode=pl.Buffered(1)) if w.ndim == 2 else
              pl.BlockSpec((None, d, ctot), lambda i: (w_layer, 0, 0), pipeline_mode=pl.Buffered(1)))
    in_specs = [pl.BlockSpec((tm, d), lambda i: (i, 0)), w_spec]
    args = [x, w]
    for (c, sa, sb, _) in tables:
        for t in (c, sa, sb):
            in_specs.append(pl.BlockSpec((tm, LANES), lambda i: (i % period_blocks, 0)))
            args.append(t)
    out_shape, out_specs = [], []
    for (c0, c1, _, _, dts) in segs:
        for dt in dts:
            if dt == HEAD_MAJOR_T:
                heads = (c1 - c0) // (2 * DIFF_DIM)
                out_shape.append(jax.ShapeDtypeStruct((n // seq, heads, 2, DIFF_DIM, seq), F32))
                out_specs.append(pl.BlockSpec((None, heads, 2, DIFF_DIM, tm),
                                              lambda i: (i // (seq // tm), 0, 0, 0, i % (seq // tm))))
            elif dt == ROW_INTERLEAVED:
                nchunk = (c1 - c0) // LANES
                out_shape.append(jax.ShapeDtypeStruct((n * nchunk, LANES), F32))
                out_specs.append(pl.BlockSpec((tm * nchunk, LANES), lambda i: (i, 0)))
            elif dt == KEY_T:
                out_shape.append(jax.ShapeDtypeStruct((n // seq, IDX_DIM, seq), F32))
                out_specs.append(pl.BlockSpec((None, IDX_DIM, tm), lambda i: (i // (seq // tm), 0, i % (seq // tm))))
            elif dt == KEY_TWICE:
                out_shape.append(jax.ShapeDtypeStruct((n, LANES), BF16))
                out_specs.append(pl.BlockSpec((tm, LANES), lambda i: (i, 0)))
            else:
                out_shape.append(jax.ShapeDtypeStruct((n, c1 - c0), dt))
                out_specs.append(pl.BlockSpec((tm, c1 - c0), lambda i: (i, 0)))
    return pl.pallas_call(
        functools.partial(_inproj_body, segs=segs_full, n_tab=len(tables), col_step=512),
        out_shape=out_shape,
        grid=(n // tm,),
        in_specs=in_specs,
        out_specs=out_specs,
        compiler_params=_cparams(("parallel",)),
        name="inproj_rope",
    )(*args)


def _softmax_step(lhs_ref, m_ref, acc_ref, kblk, vext, bias_fn, *, rows, rc, tk, k_transposed=False):
    for r in range(rows // rc):
        r0, r1 = r * rc, (r + 1) * rc
        s = (_dot if k_transposed else _dot_nt)(lhs_ref[r0:r1, :], kblk)
        if bias_fn is not None:
            s = s + bias_fn(r)
        m_prev = m_ref[r0:r1, :]
        m_next = jnp.maximum(m_prev, jnp.max(s, axis=1, keepdims=True))
        a = jnp.exp2(m_prev - m_next)
        p = jnp.exp2(s - _lane_tile(m_next, tk // LANES))
        acc_ref[r0:r1, :] = acc_ref[r0:r1, :] * _lane_tile(a, 2) + _dot(p.astype(BF16), vext)
        m_ref[r0:r1, :] = m_next


def _sortable_key(score):
    score = jnp.where(score == 0.0, 0.0, score)
    bits = pltpu.bitcast(score, jnp.int32)
    return bits ^ (lax.shift_right_arithmetic(bits, 31) & 0x7FFFFFFF)


def _dsa_select_key_major(qi_ref, kiwi_ref, kid_ref, tril_ref, key_ref, bias_ref, qlhs_ref, wt_ref, *,
                          tq, tk, nkb, q_first, l_valid, topk):
    rc = LANES
    nrc = tq // rc
    lane128 = lax.broadcasted_iota(jnp.int32, (rc, LANES), 1)
    for rh in range(nrc):
        for c in range(N_IDX_HEADS // 2):
            qc = qi_ref[rh * rc:(rh + 1) * rc, c * LANES:(c + 1) * LANES]
            qlhs_ref[rh, c, :rc, :] = jnp.where(lane128 < IDX_DIM, qc, jnp.zeros_like(qc))
            qlhs_ref[rh, c, rc:, :] = jnp.where(lane128 >= IDX_DIM, qc, jnp.zeros_like(qc))
        kw_t = kiwi_ref[rh * rc:(rh + 1) * rc, :].T
        wt_ref[rh] = kw_t[IDX_DIM:IDX_DIM + N_IDX_HEADS, :]

    def idx_block(kb, carry):
        k0 = pl.multiple_of(kb * tk, tk)
        kblk = kid_ref[pl.ds(k0, tk), :]
        kpos = k0 + lax.broadcasted_iota(jnp.int32, (tk, rc), 0)
        k_ok = kpos < l_valid
        k_chunk = lax.shift_right_arithmetic(kpos, CHUNK_SHIFT)
        for rh in range(nrc):
            score = jnp.zeros((tk, rc), F32)
            for c in range(N_IDX_HEADS // 2):
                rel = jnp.maximum(_dot_nt(kblk, qlhs_ref[rh, c]), 0.0)
                score = (score + wt_ref[rh, 2 * c:2 * c + 1, :] * rel[:, :rc]
                         + wt_ref[rh, 2 * c + 1:2 * c + 2, :] * rel[:, rc:])
            q_chunk = lax.shift_right_arithmetic(
                q_first + rh * rc + lax.broadcasted_iota(jnp.int32, (tk, rc), 1), CHUNK_SHIFT)
            key_ref[kb, rh] = jnp.where((k_chunk <= q_chunk) & k_ok, _sortable_key(score), INT_MIN)
        return carry

    lax.fori_loop(0, nkb, idx_block, 0)
    key_ref[nkb] = jnp.full((nrc, tk, rc), INT_MIN, jnp.int32)

    n_acc = 4

    def count(preds):
        def cnt_pair(kp, accs):
            accs = [list(a) for a in accs]
            for kb in (2 * kp, 2 * kp + 1):
                for rh in range(nrc):
                    hit = jnp.where(preds[rh](key_ref[kb, rh]), 1.0, 0.0)
                    for j in range(tk // 8):
                        accs[rh][j % n_acc] = accs[rh][j % n_acc] + hit[j * 8:(j + 1) * 8, :]
            return tuple(tuple(a) for a in accs)

        zero = tuple(tuple(jnp.zeros((8, rc), F32) for _ in range(n_acc)) for _ in range(nrc))
        accs = lax.fori_loop(0, (nkb + 1) // 2, cnt_pair, zero)
        return [jnp.sum(sum(a[1:], a[0]), axis=0, keepdims=True) for a in accs]

    def bit_step(i, tus):
        bit = lax.shift_left(jnp.int32(1), 31 - i)
        cands_u = [tu | bit for tu in tus]
        totals = count([(lambda key, cs=cu ^ INT_MIN: key >= cs) for cu in cands_u])
        return tuple(jnp.where(tot >= topk, cu, tu) for tot, cu, tu in zip(totals, cands_u, tus))

    tus = lax.fori_loop(0, 32, bit_step, tuple(jnp.zeros((1, rc), jnp.int32) for _ in range(nrc)))
    thr = [tu ^ INT_MIN for tu in tus]
    n_gt = count([(lambda key, th=th: key > th) for th in thr])
    need = [topk - n for n in n_gt]

    def sel_block(kb, seen):
        new_seen = []
        for rh in range(nrc):
            key = key_ref[kb, rh]
            eq = key == thr[rh]
            eqf = jnp.where(eq, 1.0, 0.0)
            pref = _dot(tril_ref[...], eqf.astype(BF16)) + seen[rh]
            keep_tie = jnp.where(eq, jnp.where(pref <= need[rh], 1, 0), 0)
            sel = jnp.where(key > thr[rh], 1, keep_tie)
            sel = jnp.where(key == INT_MIN, 0, sel)
            bias_ref[kb, rh * rc:(rh + 1) * rc, :] = jnp.where(sel == 1, 0.0, NEG_BIG).T
            new_seen.append(pref[tk - 1:tk, :])
        return tuple(new_seen)

    lax.fori_loop(0, nkb, sel_block, tuple(jnp.zeros((1, rc), F32) for _ in range(nrc)))


def _dsa_body(q_ref, qi_ref, kiwi_ref, k_ref, v_ref, kid_ref, tri_ref, o_ref, key_ref, bias_ref,
              lhs_ref, m_ref, acc_ref, qlhs_ref, wb_ref, *, tq, tk, nq, l_valid, q_pos0, topk):
    t = pl.program_id(1)
    q_first = q_pos0 + t * tq
    q_last_chunk = (q_first + tq - 1) // CHUNK
    kmax = jnp.minimum(l_valid, (q_last_chunk + 1) * CHUNK)
    nkb = (kmax + tk - 1) // tk

    if tq % LANES == 0:
        _dsa_select_key_major(qi_ref, kiwi_ref, kid_ref, tri_ref, key_ref, bias_ref, qlhs_ref, wb_ref,
                              tq=tq, tk=tk, nkb=nkb, q_first=q_first, l_valid=l_valid, topk=topk)
    else:
        _dsa_select_query_major(qi_ref, kiwi_ref, kid_ref, tri_ref, key_ref, bias_ref, qlhs_ref, wb_ref,
                                tq=tq, tk=tk, nkb=nkb, q_first=q_first, l_valid=l_valid, topk=topk)
    _dsa_attend(q_ref, k_ref, v_ref, o_ref, bias_ref, lhs_ref, m_ref, acc_ref, tq=tq, tk=tk, nkb=nkb)


def _dsa_select_query_major(qi_ref, kiwi_ref, kid_ref, tri_ref, key_ref, bias_ref, qlhs_ref, wb_ref, *,
                            tq, tk, nkb, q_first, l_valid, topk):
    rc = min(ROW_CHUNK, tq)
    nrc = tq // rc
    nrep = tk // LANES

    lane128 = lax.broadcasted_iota(jnp.int32, (rc, LANES), 1)
    for rh in range(nrc):
        for c in range(N_IDX_HEADS // 2):
            qc = qi_ref[rh * rc:(rh + 1) * rc, c * LANES:(c + 1) * LANES]
            qlhs_ref[rh, c, :rc, :] = jnp.where(lane128 < IDX_DIM, qc, jnp.zeros_like(qc))
            qlhs_ref[rh, c, rc:, :] = jnp.where(lane128 >= IDX_DIM, qc, jnp.zeros_like(qc))
        wi = kiwi_ref[rh * rc:(rh + 1) * rc, IDX_DIM:IDX_DIM + N_IDX_HEADS]
        for h in range(N_IDX_HEADS):
            wb_ref[rh, h] = jnp.broadcast_to(wi[:, h:h + 1], (rc, LANES))

    def idx_block(kb, carry):
        k0 = pl.multiple_of(kb * tk, tk)
        kblk = kid_ref[pl.ds(k0, tk), :]
        kpos = k0 + lax.broadcasted_iota(jnp.int32, (rc, tk), 1)
        for rh in range(nrc):
            score = jnp.zeros((rc, tk), F32)
            for c in range(N_IDX_HEADS // 2):
                rel = jnp.maximum(_dot_nt(qlhs_ref[rh, c], kblk), 0.0)
                score = (score + _lane_tile(wb_ref[rh, 2 * c], nrep) * rel[:rc]
                         + _lane_tile(wb_ref[rh, 2 * c + 1], nrep) * rel[rc:])
            q_chunk = lax.shift_right_arithmetic(
                q_first + rh * rc + lax.broadcasted_iota(jnp.int32, (rc, tk), 0), CHUNK_SHIFT)
            adm = (lax.shift_right_arithmetic(kpos, CHUNK_SHIFT) <= q_chunk) & (kpos < l_valid)
            key_ref[kb, rh * rc:(rh + 1) * rc, :] = jnp.where(adm, _sortable_key(score), INT_MIN)
        return carry

    lax.fori_loop(0, nkb, idx_block, 0)
    key_ref[nkb] = jnp.full((tq, tk), INT_MIN, jnp.int32)

    def lane_counts(rh, pred):
        def cnt_pair(kp, cnt):
            for kb in (2 * kp, 2 * kp + 1):
                for c in range(nrep):
                    cnt = cnt + jnp.where(pred(key_ref[kb, rh * rc:(rh + 1) * rc, c * LANES:(c + 1) * LANES]), 1.0, 0.0)
            return cnt
        return lax.fori_loop(0, (nkb + 1) // 2, cnt_pair, jnp.zeros((rc, LANES), F32))

    def count(preds):
        partial = [lane_counts(rh, preds[rh]) for rh in range(nrc)]
        return [jnp.sum(c, axis=1, keepdims=True) for c in partial]

    def bit_step(i, tus):
        bit = lax.shift_left(jnp.int32(1), 31 - i)
        cands_u = [tu | bit for tu in tus]
        totals = count([(lambda key, cs=cu ^ INT_MIN: key >= cs) for cu in cands_u])
        return tuple(jnp.where(tot >= topk, cu, tu) for tot, cu, tu in zip(totals, cands_u, tus))

    tus = lax.fori_loop(0, 32, bit_step, tuple(jnp.zeros((rc, LANES), jnp.int32) for _ in range(nrc)))
    thr128 = [tu ^ INT_MIN for tu in tus]
    n_gt = count([(lambda key, th=th: key > th) for th in thr128])
    thr = thr128[0][:, :1] if nrc == 1 else jnp.concatenate([th[:, :1] for th in thr128], axis=0)
    need = topk - (n_gt[0] if nrc == 1 else jnp.concatenate(n_gt, axis=0))

    def sel_block(kb, seen):
        key = key_ref[kb]
        eq = key == thr
        eqf = jnp.where(eq, 1.0, 0.0)
        pref = _dot(eqf.astype(BF16), tri_ref[...]) + seen
        keep_tie = jnp.where(eq, jnp.where(pref <= need, 1, 0), 0)
        sel = jnp.where(key > thr, 1, keep_tie)
        sel = jnp.where(key == INT_MIN, 0, sel)
        bias_ref[kb] = jnp.where(sel == 1, 0.0, NEG_BIG)
        return pref[:, tk - 1:tk]

    lax.fori_loop(0, nkb, sel_block, jnp.zeros((tq, 1), F32))


def _dsa_attend(q_ref, k_ref, v_ref, o_ref, bias_ref, lhs_ref, m_ref, acc_ref, *, tq, tk, nkb):
    rep = N_HEADS_A // N_KV_HEADS_A
    rc = min(ROW_CHUNK, tq)
    ones = jnp.ones((tk, LANES), BF16)
    for g in range(N_KV_HEADS_A):
        for r in range(rep):
            lhs_ref[g, r * tq:(r + 1) * tq, :] = q_ref[:, (g * rep + r) * HEAD_DIM:(g * rep + r + 1) * HEAD_DIM]
    m_ref[...] = jnp.full(m_ref.shape, NEG_BIG, F32)
    acc_ref[...] = jnp.zeros(acc_ref.shape, F32)

    def kv_block(kb, c):
        k0 = pl.multiple_of(kb * tk, tk)
        for g in range(N_KV_HEADS_A):
            kblk = k_ref[pl.ds(k0, tk), g * HEAD_DIM:(g + 1) * HEAD_DIM]
            vext = jnp.concatenate([v_ref[pl.ds(k0, tk), g * HEAD_DIM:(g + 1) * HEAD_DIM], ones], axis=1)
            _softmax_step(lhs_ref.at[g], m_ref.at[g], acc_ref.at[g], kblk, vext,
                          lambda r: bias_ref[kb, (r * rc) % tq:(r * rc) % tq + rc, :],
                          rows=rep * tq, rc=rc, tk=tk)
        return c

    lax.fori_loop(0, nkb, kv_block, 0)
    for g in range(N_KV_HEADS_A):
        acc = acc_ref[g]
        o = acc[:, :HEAD_DIM] / acc[:, HEAD_DIM:]
        for r in range(rep):
            h0 = (g * rep + r) * HEAD_DIM
            o_ref[:, h0:h0 + HEAD_DIM] = o[r * tq:(r + 1) * tq].astype(o_ref.dtype)


def _dsa_attention(q, qi, kiwi, k_all, v_all, kid_all, *, batch, lq, tq, tk, l_valid, q_pos0, topk):
    lk_pad = k_all.shape[1]
    nq = lq // tq
    nkb_max = lk_pad // tk
    a_q = N_HEADS_A * HEAD_DIM
    a_kv = N_KV_HEADS_A * HEAD_DIM
    rep = N_HEADS_A // N_KV_HEADS_A
    rc = min(ROW_CHUNK, tq)
    ones_tri = np.triu(np.ones((tk, tk), np.float32))
    if tq % LANES == 0:
        tri = jnp.asarray(ones_tri.T, BF16)
        key_scratch = pltpu.VMEM((nkb_max + 1, tq // LANES, tk, LANES), jnp.int32)
        w_scratch = pltpu.VMEM((tq // LANES, N_IDX_HEADS, LANES), F32)
    else:
        tri = jnp.asarray(ones_tri, BF16)
        key_scratch = pltpu.VMEM((nkb_max + 1, tq, tk), jnp.int32)
        w_scratch = pltpu.VMEM((tq // rc, N_IDX_HEADS, rc, LANES), F32)
    return pl.pallas_call(
        functools.partial(_dsa_body, tq=tq, tk=tk, nq=nq, l_valid=l_valid, q_pos0=q_pos0, topk=topk),
        out_shape=jax.ShapeDtypeStruct((batch * lq, a_q), BF16),
        grid=(batch, nq),
        in_specs=[
            pl.BlockSpec((tq, a_q), lambda b, t: (b * nq + t, 0)),
            pl.BlockSpec((tq, N_IDX_HEADS * IDX_DIM), lambda b, t: (b * nq + t, 0)),
            pl.BlockSpec((tq, LANES), lambda b, t: (b * nq + t, 0)),
            pl.BlockSpec((None, lk_pad, a_kv), lambda b, t: (b, 0, 0)),
            pl.BlockSpec((None, lk_pad, a_kv), lambda b, t: (b, 0, 0)),
            pl.BlockSpec((None, lk_pad, LANES), lambda b, t: (b, 0, 0)),
            pl.BlockSpec((tk, tk), lambda b, t: (0, 0)),
        ],
        out_specs=pl.BlockSpec((tq, a_q), lambda b, t: (b * nq + t, 0)),
        scratch_shapes=[key_scratch, pltpu.VMEM((nkb_max, tq, tk), F32),
                        pltpu.VMEM((N_KV_HEADS_A, rep * tq, HEAD_DIM), BF16),
                        pltpu.VMEM((N_KV_HEADS_A, rep * tq, LANES), F32),
                        pltpu.VMEM((N_KV_HEADS_A, rep * tq, 2 * HEAD_DIM), F32),
                        pltpu.VMEM((tq // rc, N_IDX_HEADS // 2, 2 * rc, LANES), BF16),
                        w_scratch],
        compiler_params=_cparams(("parallel", "arbitrary")),
        name="dsa_attention",
    )(q, qi, kiwi, k_all, v_all, kid_all, tri)


def _diff_body(q_ref, k_ref, v_ref, lq1_ref, lk1_ref, lq2_ref, lk2_ref, g_ref, o_ref, lhs_ref, m_ref, acc_ref, *,
               tq, tk, nq, l_valid, q_pos0, lam_init, rc, heads):
    lam = (jnp.exp(jnp.sum(lq1_ref[...] * lk1_ref[...], axis=1, keepdims=True))
           - jnp.exp(jnp.sum(lq2_ref[...] * lk2_ref[...], axis=1, keepdims=True)) + lam_init)
    lane = lax.broadcasted_iota(jnp.int32, (tq, LANES), 1)
    ones = jnp.ones((tk, LANES), BF16)
    hd = 2 * DIFF_DIM

    def tile(t, carry):
        q0 = pl.multiple_of(t * tq, tq)
        q_first = q_pos0 + t * tq
        for h in range(heads):
            qc = q_ref[pl.ds(q0, tq), h * hd:(h + 1) * hd]
            lhs_ref[h, :tq, :] = jnp.where(lane < DIFF_DIM, qc, jnp.zeros_like(qc))
            lhs_ref[h, tq:, :] = jnp.where(lane >= DIFF_DIM, qc, jnp.zeros_like(qc))
        m_ref[...] = jnp.full(m_ref.shape, NEG_BIG, F32)
        acc_ref[...] = jnp.zeros(acc_ref.shape, F32)
        nfull = jnp.minimum(((q_first // CHUNK + 1) * CHUNK) // tk, l_valid // tk)
        nkb = (jnp.minimum(l_valid, ((q_first + tq - 1) // CHUNK + 1) * CHUNK) + tk - 1) // tk

        def step(kb, c, masked):
            k0 = pl.multiple_of(kb * tk, tk)

            def bias_fn(r):
                row0 = q_first + (r * rc) % tq
                q_chunk = lax.shift_right_arithmetic(row0 + lax.broadcasted_iota(jnp.int32, (rc, tk), 0), CHUNK_SHIFT)
                kpos = k0 + lax.broadcasted_iota(jnp.int32, (rc, tk), 1)
                adm = (lax.shift_right_arithmetic(kpos, CHUNK_SHIFT) <= q_chunk) & (kpos < l_valid)
                return jnp.where(adm, 0.0, NEG_BIG)

            for h in range(heads):
                kblk = k_ref[pl.ds(k0, tk), h * hd:(h + 1) * hd]
                vext = jnp.concatenate([v_ref[pl.ds(k0, tk), h * hd:(h + 1) * hd], ones], axis=1)
                _softmax_step(lhs_ref.at[h], m_ref.at[h], acc_ref.at[h], kblk, vext, bias_fn if masked else None,
                              rows=2 * tq, rc=rc, tk=tk)
            return c

        lax.fori_loop(0, nfull, functools.partial(step, masked=False), 0)
        lax.fori_loop(nfull, nkb, functools.partial(step, masked=True), 0)
        for h in range(heads):
            acc = acc_ref[h]
            o = acc[:, :hd] / acc[:, hd:]
            o = o[:tq] - lam * o[tq:]
            o = o * lax.rsqrt(jnp.mean(o * o, axis=-1, keepdims=True) + LN_EPS) * g_ref[...] * (1.0 - lam_init)
            o_ref[pl.ds(q0, tq), h * hd:(h + 1) * hd] = o.astype(o_ref.dtype)
        return carry

    lax.fori_loop(0, nq, tile, 0)


def _diff_attention(q, k_all, v_all, lq1, lk1, lq2, lk2, g, j, *, batch, lq, tq, tk, l_valid, q_pos0, lam_init):
    lk_pad = k_all.shape[1]
    nq = lq // tq
    hd = 2 * DIFF_DIM
    heads = DIFF_HEADS_PER_STEP
    rc = min(ROW_CHUNK, 2 * tq)
    single = pl.Buffered(1)
    lam_spec = pl.BlockSpec((None, 1, DIFF_DIM), lambda b, h: (j, 0, 0))
    return pl.pallas_call(
        functools.partial(_diff_body, tq=tq, tk=tk, nq=nq, l_valid=l_valid, q_pos0=q_pos0, lam_init=lam_init,
                          rc=rc, heads=heads),
        out_shape=jax.ShapeDtypeStruct((batch * lq, N_HEADS_B * hd), BF16),
        grid=(batch, N_HEADS_B // heads),
        in_specs=[
            pl.BlockSpec((lq, heads * hd), lambda b, h: (b, h), pipeline_mode=single),
            pl.BlockSpec((None, lk_pad, heads * hd), lambda b, h: (b, 0, h), pipeline_mode=single),
            pl.BlockSpec((None, lk_pad, heads * hd), lambda b, h: (b, 0, h), pipeline_mode=single),
            lam_spec, lam_spec, lam_spec, lam_spec,
            pl.BlockSpec((None, 1, hd), lambda b, h: (j, 0, 0)),
        ],
        out_specs=pl.BlockSpec((lq, heads * hd), lambda b, h: (b, h), pipeline_mode=single),
        scratch_shapes=[pltpu.VMEM((heads, 2 * tq, hd), BF16), pltpu.VMEM((heads, 2 * tq, LANES), F32),
                        pltpu.VMEM((heads, 2 * tq, 2 * hd), F32)],
        compiler_params=_cparams(("parallel", "parallel")),
        name="diff_attention",
    )(q, k_all, v_all, lq1, lk1, lq2, lk2, g)


def _diff_past_body(q_ref, pkt_ref, pv_ref, kn_ref, vn_ref, lq1_ref, lk1_ref, lq2_ref, lk2_ref, g_ref, o_ref,
                    lhs_ref, m_ref, acc_ref, *, tq, tk, l_past, l_new, q_pos0, lam_init, rc, heads):
    lam = (jnp.exp(jnp.sum(lq1_ref[...] * lk1_ref[...], axis=1, keepdims=True))
           - jnp.exp(jnp.sum(lq2_ref[...] * lk2_ref[...], axis=1, keepdims=True)) + lam_init)
    lane = lax.broadcasted_iota(jnp.int32, (tq, LANES), 1)
    ones = jnp.ones((tk, LANES), BF16)
    hd = 2 * DIFF_DIM
    for h in range(heads):
        qc = q_ref[:, h * hd:(h + 1) * hd]
        lhs_ref[h, :tq, :] = jnp.where(lane < DIFF_DIM, qc, jnp.zeros_like(qc))
        lhs_ref[h, tq:, :] = jnp.where(lane >= DIFF_DIM, qc, jnp.zeros_like(qc))
    m_ref[...] = jnp.full(m_ref.shape, NEG_BIG, F32)
    acc_ref[...] = jnp.zeros(acc_ref.shape, F32)

    for kb in range(l_past // tk):
        for h in range(heads):
            kt = pkt_ref[h, :, kb * tk:(kb + 1) * tk].astype(BF16)
            vext = jnp.concatenate([pv_ref[kb * tk:(kb + 1) * tk, h * hd:(h + 1) * hd].astype(BF16), ones], axis=1)
            _softmax_step(lhs_ref.at[h], m_ref.at[h], acc_ref.at[h], kt, vext, None,
                          rows=2 * tq, rc=rc, tk=tk, k_transposed=True)

    def bias_fn(r):
        row0 = q_pos0 + (r * rc) % tq
        q_chunk = lax.shift_right_arithmetic(row0 + lax.broadcasted_iota(jnp.int32, (rc, tk), 0), CHUNK_SHIFT)
        kpos = l_past + lax.broadcasted_iota(jnp.int32, (rc, tk), 1)
        adm = (lax.shift_right_arithmetic(kpos, CHUNK_SHIFT) <= q_chunk) & (kpos < l_past + l_new)
        return jnp.where(adm, 0.0, NEG_BIG)

    for h in range(heads):
        vext = jnp.concatenate([vn_ref[:, h * hd:(h + 1) * hd], ones], axis=1)
        _softmax_step(lhs_ref.at[h], m_ref.at[h], acc_ref.at[h], kn_ref[:, h * hd:(h + 1) * hd], vext, bias_fn,
                      rows=2 * tq, rc=rc, tk=tk)
    for h in range(heads):
        acc = acc_ref[h]
        o = acc[:, :hd] / acc[:, hd:]
        o = o[:tq] - lam * o[tq:]
        o = o * lax.rsqrt(jnp.mean(o * o, axis=-1, keepdims=True) + LN_EPS) * g_ref[...] * (1.0 - lam_init)
        o_ref[:, h * hd:(h + 1) * hd] = o.astype(o_ref.dtype)


def _diff_attention_past(q, pk_t, pv, k_new, v_new, lq1, lk1, lq2, lk2, g, j, *, batch, lq, tk, l_new, q_pos0, lam_init):
    l_past = pk_t.shape[-1]
    hd = 2 * DIFF_DIM
    heads = DIFF_PAST_HEADS_PER_STEP
    rc = min(ROW_CHUNK, 2 * lq)
    lam_spec = pl.BlockSpec((None, 1, DIFF_DIM), lambda b, h: (j, 0, 0))
    return pl.pallas_call(
        functools.partial(_diff_past_body, tq=lq, tk=tk, l_past=l_past, l_new=l_new, q_pos0=q_pos0,
                          lam_init=lam_init, rc=rc, heads=heads),
        out_shape=jax.ShapeDtypeStruct((batch * lq, N_HEADS_B * hd), BF16),
        grid=(batch, N_HEADS_B // heads),
        in_specs=[
            pl.BlockSpec((lq, heads * hd), lambda b, h: (b, h)),
            pl.BlockSpec((None, heads, hd, l_past), lambda b, h: (b, h, 0, 0)),
            pl.BlockSpec((None, l_past, heads * hd), lambda b, h: (b, 0, h)),
            pl.BlockSpec((None, tk, heads * hd), lambda b, h: (b, 0, h)),
            pl.BlockSpec((None, tk, heads * hd), lambda b, h: (b, 0, h)),
            lam_spec, lam_spec, lam_spec, lam_spec,
            pl.BlockSpec((None, 1, hd), lambda b, h: (j, 0, 0)),
        ],
        out_specs=pl.BlockSpec((lq, heads * hd), lambda b, h: (b, h)),
        scratch_shapes=[pltpu.VMEM((heads, 2 * lq, hd), BF16), pltpu.VMEM((heads, 2 * lq, LANES), F32),
                        pltpu.VMEM((heads, 2 * lq, 2 * hd), F32)],
        compiler_params=_cparams(("parallel", "parallel")),
        name="diff_attention_past",
    )(q, pk_t, pv, k_new, v_new, lq1, lk1, lq2, lk2, g)


def _pad_keys(x, lk_pad):
    pad = lk_pad - x.shape[1]
    return x if pad == 0 else jnp.pad(x, ((0, 0), (0, pad), (0, 0)))


def _trunk(x, p, pos0, pasts, w, *, tm):
    batch, seq, d = x.shape
    n = batch * seq
    depth = w["ln_g"].shape[0]
    alpha = float((2 * depth) ** 0.25)
    a_q = N_HEADS_A * HEAD_DIM
    a_kv = N_KV_HEADS_A * HEAD_DIM
    a_qi = N_IDX_HEADS * IDX_DIM
    b_qk = N_HEADS_B * 2 * DIFF_DIM

    period = max(1, seq // tm)
    table_rows = max(seq, tm)
    positions = (pos0 + np.arange(seq)).astype(np.int64)
    positions = np.tile(positions, table_rows // seq)
    tab128 = _rope_tables(positions, HEAD_DIM, "head")
    tab64 = _rope_tables(positions, IDX_DIM, "head")
    tabkw = _rope_tables(positions, IDX_DIM, "kiwi")

    xf = x.reshape(n, d)
    pf = p.reshape(depth, n, p.shape[-1])
    tf = 512 if w["ffn_w_gate"].shape[-1] % 512 == 0 else 128
    rows = []
    for i in range(depth):
        j = i // 2
        xf = _ffn(xf, w["ffn_w_gate"], w["ffn_w_up"], w["ffn_w_down"], w["ln_g"], w["ln_b"], i, 0, 0,
                  alpha=alpha, tm=tm, tf=tf)
        past = pasts[i]
        lk_new = seq if past is None else past[0].shape[1] + seq
        if i % 2 == 0:
            wa = w["a_w_in"]
            c_k, c_v, c_qi, c_kw = a_q, a_q + a_kv, a_q + 2 * a_kv, a_q + 2 * a_kv + a_qi
            kw_kinds = (F32, KEY_T, KEY_TWICE) if seq % tm == 0 else (F32,)
            outs = _inproj(xf, wa, [tab128, tab64, tabkw],
                           [(0, c_k, 0, HEAD_DIM ** -0.5 * LOG2_E, (BF16,)),
                            (c_k, c_v, 0, 1.0, (ROW_INTERLEAVED, BF16)),
                            (c_v, c_qi, None, 1.0, (ROW_INTERLEAVED, BF16)),
                            (c_qi, c_kw, 1, 1.0, (BF16,)),
                            (c_kw, c_kw + LANES, 2, 1.0, kw_kinds)],
                           tm=tm, period_blocks=period, seq=seq, w_layer=j)
            q, k, kb, v, vb, qi, kiwi = outs[:7]
            k_all, v_all = kb.reshape(batch, seq, a_kv), vb.reshape(batch, seq, a_kv)
            if seq % tm == 0:
                ki_rows = jnp.transpose(outs[7], (0, 2, 1))
                kid_all = outs[8].reshape(batch, seq, LANES)
            else:
                ki_rows = kiwi[:, :IDX_DIM].reshape(batch, seq, IDX_DIM)
                kid_all = jnp.concatenate([ki_rows, ki_rows], axis=-1).astype(BF16)
            rows.append((k.reshape(batch, seq, N_KV_HEADS_A, HEAD_DIM), v.reshape(batch, seq, N_KV_HEADS_A, HEAD_DIM),
                         ki_rows))
            if past is not None:
                pk, pv, pki = past
                k_all = jnp.concatenate([pk.reshape(batch, -1, a_kv).astype(BF16), k_all], axis=1)
                v_all = jnp.concatenate([pv.reshape(batch, -1, a_kv).astype(BF16), v_all], axis=1)
                kid_all = jnp.concatenate([jnp.concatenate([pki, pki], axis=-1).astype(BF16), kid_all], axis=1)
            tq = min(DSA_Q_TILE, seq)
            tk = KEY_BLOCK
            lk_pad = -(-lk_new // tk) * tk
            topk = min(TOPK_MAX, lk_new // 4)
            o = _dsa_attention(q, qi, kiwi, _pad_keys(k_all, lk_pad), _pad_keys(v_all, lk_pad), _pad_keys(kid_all, lk_pad),
                               batch=batch, lq=seq, tq=tq, tk=tk, l_valid=lk_new, q_pos0=pos0, topk=topk)
            xf = _oproj(xf, o, w["a_w_out"], w["ln_g"], w["ln_b"], i, j, alpha=alpha, tm=tm)
        else:
            wb = w["b_w_in"]
            lam_init = 0.8 - 0.6 * math.exp(-0.3 * i)
            k_kinds = (HEAD_MAJOR_T, BF16) if seq % tm == 0 else (F32, BF16)
            q, k, kb = _inproj(xf, wb[j, :, :2 * b_qk], [tab64],
                               [(0, b_qk, 0, DIFF_DIM ** -0.5 * LOG2_E, (BF16,)), (b_qk, 2 * b_qk, 0, 1.0, k_kinds)],
                               tm=tm, period_blocks=period, seq=seq)
            k_rows = (jnp.transpose(k, (0, 4, 1, 2, 3)) if seq % tm == 0
                      else k.reshape(batch, seq, N_HEADS_B, 2, DIFF_DIM))
            v, vb = _inproj(xf, wb[j, :, 2 * b_qk:], [], [(0, b_qk, None, 1.0, (F32, BF16))],
                            tm=tm, period_blocks=period)
            rows.append((k_rows, v.reshape(batch, seq, N_HEADS_B, 2 * DIFF_DIM)))
            k_all, v_all = kb.reshape(batch, seq, b_qk), vb.reshape(batch, seq, b_qk)
            tq = min(256, seq)
            tk = KEY_BLOCK
            lam_args = (w["b_lambda_q1"], w["b_lambda_k1"], w["b_lambda_q2"], w["b_lambda_k2"], w["b_subln"], j)
            if past is not None and past[0].shape[1] % tk == 0 and seq <= tk and pos0 >= past[0].shape[1]:
                pk, pv = past
                l_past = pk.shape[1]
                pk_t = jnp.transpose(pk, (0, 2, 3, 4, 1)).reshape(batch, N_HEADS_B, 2 * DIFF_DIM, l_past)
                o = _diff_attention_past(q, pk_t, pv.reshape(batch, l_past, b_qk), _pad_keys(k_all, tk),
                                         _pad_keys(v_all, tk), *lam_args, batch=batch, lq=seq, tk=tk, l_new=seq,
                                         q_pos0=pos0, lam_init=lam_init)
            else:
                if past is not None:
                    pk, pv = past
                    k_all = jnp.concatenate([pk.reshape(batch, -1, b_qk).astype(BF16), k_all], axis=1)
                    v_all = jnp.concatenate([pv.reshape(batch, -1, b_qk).astype(BF16), v_all], axis=1)
                lk_pad = -(-lk_new // tk) * tk
                o = _diff_attention(q, _pad_keys(k_all, lk_pad), _pad_keys(v_all, lk_pad), *lam_args,
                                    batch=batch, lq=seq, tq=tq, tk=tk, l_valid=lk_new, q_pos0=pos0, lam_init=lam_init)
            xf = _oproj(xf, o, w["b_w_out"], w["ln_g"], w["ln_b"], i, j, alpha=alpha, tm=tm)
        xf = _ffn(xf, w["ffn_w_gate"], w["ffn_w_up"], w["ffn_w_down"], w["ln_g"], w["ln_b"], i, 1, 2,
                  alpha=alpha, tm=tm, tf=tf)
        xf = _ple(xf, pf, w["ple_w_gate"], w["ple_b_gate"], w["ple_w_proj"], w["ln_g"], w["ln_b"], i,
                  alpha=alpha, tm=tm)
    return xf.reshape(batch, seq, d), rows


def kernel(x_prompt, x_sample, cache_l0_k, cache_l0_v, cache_l0_kidx, cache_l1_k, cache_l1_v, cache_l2_k, cache_l2_v, cache_l2_kidx, cache_l3_k, cache_l3_v, p_prompt, p_sample, ln_g, ln_b, ffn_w_gate, ffn_w_up, ffn_w_down, ple_w_gate, ple_b_gate, ple_w_proj, a_w_in, a_w_out, b_w_in, b_w_out, b_lambda_q1, b_lambda_k1, b_lambda_q2, b_lambda_k2, b_subln):
    depth, _, d = ln_g.shape
    a_cols = a_w_in.shape[-1]
    a_pad = -(-a_cols // LANES) * LANES - a_cols
    w = {
        "ln_g": ln_g.reshape(depth, 4, 1, d),
        "ln_b": ln_b.reshape(depth, 4, 1, d),
        "ffn_w_gate": ffn_w_gate.astype(BF16),
        "ffn_w_up": ffn_w_up.astype(BF16),
        "ffn_w_down": (0.5 * ffn_w_down).astype(BF16),
        "ple_w_gate": ple_w_gate.astype(BF16),
        "ple_b_gate": ple_b_gate.reshape(depth, 1, d),
        "ple_w_proj": ple_w_proj.astype(BF16),
        "a_w_in": jnp.pad(a_w_in, ((0, 0), (0, 0), (0, a_pad))).astype(BF16),
        "a_w_out": a_w_out.astype(BF16),
        "b_w_in": b_w_in.astype(BF16),
        "b_w_out": b_w_out.astype(BF16),
        "b_lambda_q1": b_lambda_q1.reshape(-1, 1, DIFF_DIM),
        "b_lambda_k1": b_lambda_k1.reshape(-1, 1, DIFF_DIM),
        "b_lambda_q2": b_lambda_q2.reshape(-1, 1, DIFF_DIM),
        "b_lambda_k2": b_lambda_k2.reshape(-1, 1, DIFF_DIM),
        "b_subln": b_subln.reshape(-1, 1, 2 * DIFF_DIM),
    }
    past_len = cache_l0_k.shape[1]
    pasts_s = [(cache_l0_k, cache_l0_v, cache_l0_kidx), (cache_l1_k, cache_l1_v),
               (cache_l2_k, cache_l2_v, cache_l2_kidx), (cache_l3_k, cache_l3_v)]
    n_p = x_prompt.shape[0] * x_prompt.shape[1]
    n_s = x_sample.shape[0] * x_sample.shape[1]
    y_p, rows_p = _trunk(x_prompt, p_prompt, 0, [None] * depth, w, tm=min(512, n_p))
    y_s, rows_s = _trunk(x_sample, p_sample, past_len, pasts_s, w, tm=min(256, n_s))
    out = [y_p, y_s]
    for rp, rs in zip(rows_p, rows_s):
        out.extend(rp)
        out.extend(rs)
    return tuple(out)
```

```python
import functools
import math

import numpy as np
import jax
import jax.numpy as jnp
from jax import lax
from jax.experimental import pallas as pl
from jax.experimental.pallas import tpu as pltpu

CHUNK = 64
CHUNK_SHIFT = 6
HEAD_DIM = 128
N_HEADS_A = 16
N_KV_HEADS_A = 4
N_IDX_HEADS = 16
IDX_DIM = 64
TOPK_MAX = 256
N_HEADS_B = 16
DIFF_DIM = 64
ROPE_FRACTION = 4
ROPE_THETA = 500000.0
LN_EPS = 1e-5

LANES = 128
KEY_BLOCK = 256
ROW_CHUNK = 128
DIFF_HEADS_PER_STEP = 16
DIFF_PAST_HEADS_PER_STEP = 8
DSA_Q_TILE = 256
ROW_GROUPS = 2
LOG2_E = 1.4426950408889634
VMEM_LIMIT_BYTES = 56 * 1024 * 1024
NEG_BIG = -1e30
INT_MIN = -(2 ** 31)

F32 = jnp.float32
BF16 = jnp.bfloat16
HEAD_MAJOR_T = "f32 (batch, head, map, 64, seq)"
ROW_INTERLEAVED = "f32 (rows * chunks, 128)"
KEY_T = "f32 (batch, 64, seq)"
KEY_TWICE = "bf16 (rows, 128) = [ki | ki]"


def _cparams(sem):
    return pltpu.CompilerParams(dimension_semantics=sem, vmem_limit_bytes=VMEM_LIMIT_BYTES)


def _layer_norm(z, g, b):
    mu = jnp.mean(z, axis=-1, keepdims=True)
    zc = z - mu
    var = jnp.mean(zc * zc, axis=-1, keepdims=True)
    return zc * lax.rsqrt(var + LN_EPS) * g + b


def _dot(a, b):
    return jnp.dot(a, b, preferred_element_type=F32)


def _lane_tile(x, n):
    return x if n == 1 else jnp.concatenate([x] * n, axis=1)


def _dot_nt(a, b):
    return lax.dot_general(a, b, (((1,), (1,)), ((), ())), preferred_element_type=F32)


def _ffn_body(x_ref, wg_ref, wu_ref, wd_ref, g_ref, b_ref, o_ref, xb_ref, *, alpha, nf):
    f = pl.program_id(1)

    @pl.when(f == 0)
    def _():
        x = x_ref[...]
        xb_ref[...] = x.astype(BF16)
        o_ref[...] = alpha * x

    xb = xb_ref[...]
    gate = _dot(xb, wg_ref[...])
    up = _dot(xb, wu_ref[...])
    h = (gate * jax.nn.sigmoid(gate) * up).astype(BF16)
    o_ref[...] += _dot(h, wd_ref[...])

    @pl.when(f == nf - 1)
    def _():
        o_ref[...] = _layer_norm(o_ref[...], g_ref[...], b_ref[...])


def _ffn(x, wg, wu, wd, ln_g, ln_b, layer, sub, ln_idx, *, alpha, tm, tf):
    n, d = x.shape
    f_dim = wg.shape[-1]
    nf = f_dim // tf
    return pl.pallas_call(
        functools.partial(_ffn_body, alpha=alpha, nf=nf),
        out_shape=jax.ShapeDtypeStruct((n, d), F32),
        grid=(n // tm, nf),
        in_specs=[
            pl.BlockSpec((tm, d), lambda i, f: (i, 0)),
            pl.BlockSpec((None, None, d, tf), lambda i, f: (layer, sub, 0, f)),
            pl.BlockSpec((None, None, d, tf), lambda i, f: (layer, sub, 0, f)),
            pl.BlockSpec((None, None, tf, d), lambda i, f: (layer, sub, f, 0)),
            pl.BlockSpec((None, None, 1, d), lambda i, f: (layer, ln_idx, 0, 0)),
            pl.BlockSpec((None, None, 1, d), lambda i, f: (layer, ln_idx, 0, 0)),
        ],
        out_specs=pl.BlockSpec((tm, d), lambda i, f: (i, 0)),
        scratch_shapes=[pltpu.VMEM((tm, d), BF16)],
        compiler_params=_cparams(("parallel", "arbitrary")),
        name="ffn_postnorm",
    )(x, wg, wu, wd, ln_g, ln_b)


def _ple_body(x_ref, p_ref, wg_ref, bg_ref, wp_ref, g_ref, b_ref, o_ref, *, alpha):
    rows = x_ref.shape[0] // ROW_GROUPS if x_ref.shape[0] % (ROW_GROUPS * 8) == 0 else x_ref.shape[0]
    for r0 in range(0, x_ref.shape[0], rows):
        x = x_ref[r0:r0 + rows, :]
        gate = jax.nn.sigmoid(_dot(x.astype(BF16), wg_ref[...]) + bg_ref[...])
        proj = _dot(p_ref[r0:r0 + rows, :].astype(BF16), wp_ref[...])
        o_ref[r0:r0 + rows, :] = _layer_norm(alpha * x + gate * proj, g_ref[...], b_ref[...])


def _ple(x, p, wg, bg, wp, ln_g, ln_b, layer, *, alpha, tm):
    n, d = x.shape
    pd = p.shape[-1]
    return pl.pallas_call(
        functools.partial(_ple_body, alpha=alpha),
        out_shape=jax.ShapeDtypeStruct((n, d), F32),
        grid=(n // tm,),
        in_specs=[
            pl.BlockSpec((tm, d), lambda i: (i, 0)),
            pl.BlockSpec((None, tm, pd), lambda i: (layer, i, 0)),
            pl.BlockSpec((None, d, d), lambda i: (layer, 0, 0)),
            pl.BlockSpec((None, 1, d), lambda i: (layer, 0, 0)),
            pl.BlockSpec((None, pd, d), lambda i: (layer, 0, 0)),
            pl.BlockSpec((None, None, 1, d), lambda i: (layer, 3, 0, 0)),
            pl.BlockSpec((None, None, 1, d), lambda i: (layer, 3, 0, 0)),
        ],
        out_specs=pl.BlockSpec((tm, d), lambda i: (i, 0)),
        compiler_params=_cparams(("parallel",)),
        name="ple_postnorm",
    )(x, p, wg, bg, wp, ln_g, ln_b)


def _oproj_body(x_ref, a_ref, w_ref, g_ref, b_ref, o_ref, *, alpha):
    rows = x_ref.shape[0] // ROW_GROUPS if x_ref.shape[0] % (ROW_GROUPS * 8) == 0 else x_ref.shape[0]
    for r0 in range(0, x_ref.shape[0], rows):
        mix = _dot(a_ref[r0:r0 + rows, :], w_ref[...])
        o_ref[r0:r0 + rows, :] = _layer_norm(alpha * x_ref[r0:r0 + rows, :] + mix, g_ref[...], b_ref[...])


def _oproj(x, a, w, ln_g, ln_b, layer, j, *, alpha, tm):
    n, d = x.shape
    ad = a.shape[-1]
    return pl.pallas_call(
        functools.partial(_oproj_body, alpha=alpha),
        out_shape=jax.ShapeDtypeStruct((n, d), F32),
        grid=(n // tm,),
        in_specs=[
            pl.BlockSpec((tm, d), lambda i: (i, 0)),
            pl.BlockSpec((tm, ad), lambda i: (i, 0)),
            pl.BlockSpec((None, ad, d), lambda i: (j, 0, 0)),
            pl.BlockSpec((None, None, 1, d), lambda i: (layer, 1, 0, 0)),
            pl.BlockSpec((None, None, 1, d), lambda i: (layer, 1, 0, 0)),
        ],
        out_specs=pl.BlockSpec((tm, d), lambda i: (i, 0)),
        compiler_params=_cparams(("parallel",)),
        name="oproj_postnorm",
    )(x, a, w, ln_g, ln_b)


def _rope_tables(positions, head_dim, kind):
    rd = head_dim // ROPE_FRACTION
    half = rd // 2
    inv_freq = (np.float32(ROPE_THETA) ** (-(np.arange(half, dtype=np.float32) / np.float32(half)))).astype(np.float32)
    ang = positions.astype(np.float32)[:, None] * inv_freq[None, :]
    cos, sin = np.cos(ang).astype(np.float32), np.sin(ang).astype(np.float32)
    npos = positions.shape[0]
    c = np.ones((npos, LANES), np.float32)
    sa = np.zeros((npos, LANES), np.float32)
    sb = np.zeros((npos, LANES), np.float32)
    starts = range(0, LANES, head_dim) if kind == "head" else [0]
    for h0 in starts:
        c[:, h0:h0 + half] = cos
        c[:, h0 + half:h0 + rd] = cos
        sa[:, h0:h0 + half] = -sin
        sb[:, h0 + half:h0 + rd] = sin
    if kind == "kiwi":
        c[:, IDX_DIM:IDX_DIM + N_IDX_HEADS] = (N_IDX_HEADS * IDX_DIM) ** -0.5
    return jnp.asarray(c), jnp.asarray(sa), jnp.asarray(sb), half


def _inproj_body(*refs, segs, n_tab, col_step):
    x_ref, w_ref = refs[0], refs[1]
    tab_refs = refs[2:2 + 3 * n_tab]
    out_refs = refs[2 + 3 * n_tab:]
    xb = x_ref[...].astype(BF16)
    oi = 0
    for (c0, c1, tab, half, scale, dtypes) in segs:
        outs = out_refs[oi:oi + len(dtypes)]
        oi += len(dtypes)
        for j0 in range(c0, c1, col_step):
            wj = min(col_step, c1 - j0)
            y = _dot(xb, w_ref[:, j0:j0 + wj])
            for c in range(wj // LANES):
                yc = y[:, c * LANES:(c + 1) * LANES]
                if tab is not None:
                    cc, sa, sb = (r[...] for r in tab_refs[3 * tab:3 * tab + 3])
                    yc = (yc * cc + pltpu.roll(yc, LANES - half, 1) * sa + pltpu.roll(yc, half, 1) * sb)
                if scale != 1.0:
                    yc = yc * scale
                lo = j0 - c0 + c * LANES
                for o_ref, dt in zip(outs, dtypes):
                    if dt == HEAD_MAJOR_T:
                        yt = yc.T
                        o_ref[lo // LANES, 0] = yt[:DIFF_DIM]
                        o_ref[lo // LANES, 1] = yt[DIFF_DIM:]
                    elif dt == ROW_INTERLEAVED:
                        nchunk = (c1 - c0) // LANES
                        o_ref[pl.ds(lo // LANES, yc.shape[0], stride=nchunk), :] = yc
                    elif dt == KEY_T:
                        o_ref[...] = yc.T[:IDX_DIM]
                    elif dt == KEY_TWICE:
                        lane = lax.broadcasted_iota(jnp.int32, yc.shape, 1)
                        o_ref[...] = jnp.where(lane < IDX_DIM, yc, pltpu.roll(yc, IDX_DIM, 1)).astype(BF16)
                    else:
                        o_ref[:, lo:lo + LANES] = yc.astype(dt)


def _inproj(x, w, tables, segs, *, tm, period_blocks, seq=None, w_layer=0):
    n, d = x.shape
    ctot = w.shape[-1]
    segs_full = tuple((c0, c1, tab, (tables[tab][3] if tab is not None else 0), scale, tuple(dts))
                      for (c0, c1, tab, scale, dts) in segs)
    w_spec = (pl.BlockSpec((d, ctot), lambda i: (0, 0), pipeline_mode=pl.Buffered(1)) if w.ndim == 2 else
              pl.BlockSpec((None, d, ctot), lambda i: (w_layer, 0, 0), pipeline_mode=pl.Buffered(1)))
    in_specs = [pl.BlockSpec((tm, d), lambda i: (i, 0)), w_spec]
    args = [x, w]
    for (c, sa, sb, _) in tables:
        for t in (c, sa, sb):
            in_specs.append(pl.BlockSpec((tm, LANES), lambda i: (i % period_blocks, 0)))
            args.append(t)
    out_shape, out_specs = [], []
    for (c0, c1, _, _, dts) in segs:
        for dt in dts:
            if dt == HEAD_MAJOR_T:
                heads = (c1 - c0) // (2 * DIFF_DIM)
                out_shape.append(jax.ShapeDtypeStruct((n // seq, heads, 2, DIFF_DIM, seq), F32))
                out_specs.append(pl.BlockSpec((None, heads, 2, DIFF_DIM, tm),
                                              lambda i: (i // (seq // tm), 0, 0, 0, i % (seq // tm))))
            elif dt == ROW_INTERLEAVED:
                nchunk = (c1 - c0) // LANES
                out_shape.append(jax.ShapeDtypeStruct((n * nchunk, LANES), F32))
                out_specs.append(pl.BlockSpec((tm * nchunk, LANES), lambda i: (i, 0)))
            elif dt == KEY_T:
                out_shape.append(jax.ShapeDtypeStruct((n // seq, IDX_DIM, seq), F32))
                out_specs.append(pl.BlockSpec((None, IDX_DIM, tm), lambda i: (i // (seq // tm), 0, i % (seq // tm))))
            elif dt == KEY_TWICE:
                out_shape.append(jax.ShapeDtypeStruct((n, LANES), BF16))
                out_specs.append(pl.BlockSpec((tm, LANES), lambda i: (i, 0)))
            else:
                out_shape.append(jax.ShapeDtypeStruct((n, c1 - c0), dt))
                out_specs.append(pl.BlockSpec((tm, c1 - c0), lambda i: (i, 0)))
    return pl.pallas_call(
        functools.partial(_inproj_body, segs=segs_full, n_tab=len(tables), col_step=512),
        out_shape=out_shape,
        grid=(n // tm,),
        in_specs=in_specs,
        out_specs=out_specs,
        compiler_params=_cparams(("parallel",)),
        name="inproj_rope",
    )(*args)


def _softmax_step(lhs_ref, m_ref, acc_ref, kblk, vext, bias_fn, *, rows, rc, tk, k_transposed=False):
    for r in range(rows // rc):
        r0, r1 = r * rc, (r + 1) * rc
        s = (_dot if k_transposed else _dot_nt)(lhs_ref[r0:r1, :], kblk)
        if bias_fn is not None:
            s = s + bias_fn(r)
        m_prev = m_ref[r0:r1, :]
        m_next = jnp.maximum(m_prev, jnp.max(s, axis=1, keepdims=True))
        a = jnp.exp2(m_prev - m_next)
        p = jnp.exp2(s - _lane_tile(m_next, tk // LANES))
        acc_ref[r0:r1, :] = acc_ref[r0:r1, :] * _lane_tile(a, 2) + _dot(p.astype(BF16), vext)
        m_ref[r0:r1, :] = m_next


def _sortable_key(score):
    score = jnp.where(score == 0.0, 0.0, score)
    bits = pltpu.bitcast(score, jnp.int32)
    return bits ^ (lax.shift_right_arithmetic(bits, 31) & 0x7FFFFFFF)


def _dsa_select_key_major(qi_ref, kiwi_ref, kid_ref, tril_ref, key_ref, bias_ref, qlhs_ref, wt_ref, *,
                          tq, tk, nkb, q_first, l_valid, topk):
    rc = LANES
    nrc = tq // rc
    lane128 = lax.broadcasted_iota(jnp.int32, (rc, LANES), 1)
    for rh in range(nrc):
        for c in range(N_IDX_HEADS // 2):
            qc = qi_ref[rh * rc:(rh + 1) * rc, c * LANES:(c + 1) * LANES]
            qlhs_ref[rh, c, :rc, :] = jnp.where(lane128 < IDX_DIM, qc, jnp.zeros_like(qc))
            qlhs_ref[rh, c, rc:, :] = jnp.where(lane128 >= IDX_DIM, qc, jnp.zeros_like(qc))
        kw_t = kiwi_ref[rh * rc:(rh + 1) * rc, :].T
        wt_ref[rh] = kw_t[IDX_DIM:IDX_DIM + N_IDX_HEADS, :]

    def idx_block(kb, carry):
        k0 = pl.multiple_of(kb * tk, tk)
        kblk = kid_ref[pl.ds(k0, tk), :]
        kpos = k0 + lax.broadcasted_iota(jnp.int32, (tk, rc), 0)
        k_ok = kpos < l_valid
        k_chunk = lax.shift_right_arithmetic(kpos, CHUNK_SHIFT)
        for rh in range(nrc):
            score = jnp.zeros((tk, rc), F32)
            for c in range(N_IDX_HEADS // 2):
                rel = jnp.maximum(_dot_nt(kblk, qlhs_ref[rh, c]), 0.0)
                score = (score + wt_ref[rh, 2 * c:2 * c + 1, :] * rel[:, :rc]
                         + wt_ref[rh, 2 * c + 1:2 * c + 2, :] * rel[:, rc:])
            q_chunk = lax.shift_right_arithmetic(
                q_first + rh * rc + lax.broadcasted_iota(jnp.int32, (tk, rc), 1), CHUNK_SHIFT)
            key_ref[kb, rh] = jnp.where((k_chunk <= q_chunk) & k_ok, _sortable_key(score), INT_MIN)
        return carry

    lax.fori_loop(0, nkb, idx_block, 0)
    key_ref[nkb] = jnp.full((nrc, tk, rc), INT_MIN, jnp.int32)

    n_acc = 4

    def count(preds):
        def cnt_pair(kp, accs):
            accs = [list(a) for a in accs]
            for kb in (2 * kp, 2 * kp + 1):
                for rh in range(nrc):
                    hit = jnp.where(preds[rh](key_ref[kb, rh]), 1.0, 0.0)
                    for j in range(tk // 8):
                        accs[rh][j % n_acc] = accs[rh][j % n_acc] + hit[j * 8:(j + 1) * 8, :]
            return tuple(tuple(a) for a in accs)

        zero = tuple(tuple(jnp.zeros((8, rc), F32) for _ in range(n_acc)) for _ in range(nrc))
        accs = lax.fori_loop(0, (nkb + 1) // 2, cnt_pair, zero)
        return [jnp.sum(sum(a[1:], a[0]), axis=0, keepdims=True) for a in accs]

    def bit_step(i, tus):
        bit = lax.shift_left(jnp.int32(1), 31 - i)
        cands_u = [tu | bit for tu in tus]
        totals = count([(lambda key, cs=cu ^ INT_MIN: key >= cs) for cu in cands_u])
        return tuple(jnp.where(tot >= topk, cu, tu) for tot, cu, tu in zip(totals, cands_u, tus))

    tus = lax.fori_loop(0, 32, bit_step, tuple(jnp.zeros((1, rc), jnp.int32) for _ in range(nrc)))
    thr = [tu ^ INT_MIN for tu in tus]
    n_gt = count([(lambda key, th=th: key > th) for th in thr])
    need = [topk - n for n in n_gt]

    def sel_block(kb, seen):
        new_seen = []
        for rh in range(nrc):
            key = key_ref[kb, rh]
            eq = key == thr[rh]
            eqf = jnp.where(eq, 1.0, 0.0)
            pref = _dot(tril_ref[...], eqf.astype(BF16)) + seen[rh]
            keep_tie = jnp.where(eq, jnp.where(pref <= need[rh], 1, 0), 0)
            sel = jnp.where(key > thr[rh], 1, keep_tie)
            sel = jnp.where(key == INT_MIN, 0, sel)
            bias_ref[kb, rh * rc:(rh + 1) * rc, :] = jnp.where(sel == 1, 0.0, NEG_BIG).T
            new_seen.append(pref[tk - 1:tk, :])
        return tuple(new_seen)

    lax.fori_loop(0, nkb, sel_block, tuple(jnp.zeros((1, rc), F32) for _ in range(nrc)))


def _dsa_body(q_ref, qi_ref, kiwi_ref, k_ref, v_ref, kid_ref, tri_ref, o_ref, key_ref, bias_ref,
              lhs_ref, m_ref, acc_ref, qlhs_ref, wb_ref, *, tq, tk, nq, l_valid, q_pos0, topk):
    t = pl.program_id(1)
    q_first = q_pos0 + t * tq
    q_last_chunk = (q_first + tq - 1) // CHUNK
    kmax = jnp.minimum(l_valid, (q_last_chunk + 1) * CHUNK)
    nkb = (kmax + tk - 1) // tk

    if tq % LANES == 0:
        _dsa_select_key_major(qi_ref, kiwi_ref, kid_ref, tri_ref, key_ref, bias_ref, qlhs_ref, wb_ref,
                              tq=tq, tk=tk, nkb=nkb, q_first=q_first, l_valid=l_valid, topk=topk)
    else:
        _dsa_select_query_major(qi_ref, kiwi_ref, kid_ref, tri_ref, key_ref, bias_ref, qlhs_ref, wb_ref,
                                tq=tq, tk=tk, nkb=nkb, q_first=q_first, l_valid=l_valid, topk=topk)
    _dsa_attend(q_ref, k_ref, v_ref, o_ref, bias_ref, lhs_ref, m_ref, acc_ref, tq=tq, tk=tk, nkb=nkb)


def _dsa_select_query_major(qi_ref, kiwi_ref, kid_ref, tri_ref, key_ref, bias_ref, qlhs_ref, wb_ref, *,
                            tq, tk, nkb, q_first, l_valid, topk):
    rc = min(ROW_CHUNK, tq)
    nrc = tq // rc
    nrep = tk // LANES

    lane128 = lax.broadcasted_iota(jnp.int32, (rc, LANES), 1)
    for rh in range(nrc):
        for c in range(N_IDX_HEADS // 2):
            qc = qi_ref[rh * rc:(rh + 1) * rc, c * LANES:(c + 1) * LANES]
            qlhs_ref[rh, c, :rc, :] = jnp.where(lane128 < IDX_DIM, qc, jnp.zeros_like(qc))
            qlhs_ref[rh, c, rc:, :] = jnp.where(lane128 >= IDX_DIM, qc, jnp.zeros_like(qc))
        wi = kiwi_ref[rh * rc:(rh + 1) * rc, IDX_DIM:IDX_DIM + N_IDX_HEADS]
        for h in range(N_IDX_HEADS):
            wb_ref[rh, h] = jnp.broadcast_to(wi[:, h:h + 1], (rc, LANES))

    def idx_block(kb, carry):
        k0 = pl.multiple_of(kb * tk, tk)
        kblk = kid_ref[pl.ds(k0, tk), :]
        kpos = k0 + lax.broadcasted_iota(jnp.int32, (rc, tk), 1)
        for rh in range(nrc):
            score = jnp.zeros((rc, tk), F32)
            for c in range(N_IDX_HEADS // 2):
                rel = jnp.maximum(_dot_nt(qlhs_ref[rh, c], kblk), 0.0)
                score = (score + _lane_tile(wb_ref[rh, 2 * c], nrep) * rel[:rc]
                         + _lane_tile(wb_ref[rh, 2 * c + 1], nrep) * rel[rc:])
            q_chunk = lax.shift_right_arithmetic(
                q_first + rh * rc + lax.broadcasted_iota(jnp.int32, (rc, tk), 0), CHUNK_SHIFT)
            adm = (lax.shift_right_arithmetic(kpos, CHUNK_SHIFT) <= q_chunk) & (kpos < l_valid)
            key_ref[kb, rh * rc:(rh + 1) * rc, :] = jnp.where(adm, _sortable_key(score), INT_MIN)
        return carry

    lax.fori_loop(0, nkb, idx_block, 0)
    key_ref[nkb] = jnp.full((tq, tk), INT_MIN, jnp.int32)

    def lane_counts(rh, pred):
        def cnt_pair(kp, cnt):
            for kb in (2 * kp, 2 * kp + 1):
                for c in range(nrep):
                    cnt = cnt + jnp.where(pred(key_ref[kb, rh * rc:(rh + 1) * rc, c * LANES:(c + 1) * LANES]), 1.0, 0.0)
            return cnt
        return lax.fori_loop(0, (nkb + 1) // 2, cnt_pair, jnp.zeros((rc, LANES), F32))

    def count(preds):
        partial = [lane_counts(rh, preds[rh]) for rh in range(nrc)]
        return [jnp.sum(c, axis=1, keepdims=True) for c in partial]

    def bit_step(i, tus):
        bit = lax.shift_left(jnp.int32(1), 31 - i)
        cands_u = [tu | bit for tu in tus]
        totals = count([(lambda key, cs=cu ^ INT_MIN: key >= cs) for cu in cands_u])
        return tuple(jnp.where(tot >= topk, cu, tu) for tot, cu, tu in zip(totals, cands_u, tus))

    tus = lax.fori_loop(0, 32, bit_step, tuple(jnp.zeros((rc, LANES), jnp.int32) for _ in range(nrc)))
    thr128 = [tu ^ INT_MIN for tu in tus]
    n_gt = count([(lambda key, th=th: key > th) for th in thr128])
    thr = thr128[0][:, :1] if nrc == 1 else jnp.concatenate([th[:, :1] for th in thr128], axis=0)
    need = topk - (n_gt[0] if nrc == 1 else jnp.concatenate(n_gt, axis=0))

    def sel_block(kb, seen):
        key = key_ref[kb]
        eq = key == thr
        eqf = jnp.where(eq, 1.0, 0.0)
        pref = _dot(eqf.astype(BF16), tri_ref[...]) + seen
        keep_tie = jnp.where(eq, jnp.where(pref <= need, 1, 0), 0)
        sel = jnp.where(key > thr, 1, keep_tie)
        sel = jnp.where(key == INT_MIN, 0, sel)
        bias_ref[kb] = jnp.where(sel == 1, 0.0, NEG_BIG)
        return pref[:, tk - 1:tk]

    lax.fori_loop(0, nkb, sel_block, jnp.zeros((tq, 1), F32))


def _dsa_attend(q_ref, k_ref, v_ref, o_ref, bias_ref, lhs_ref, m_ref, acc_ref, *, tq, tk, nkb):
    rep = N_HEADS_A // N_KV_HEADS_A
    rc = min(ROW_CHUNK, tq)
    ones = jnp.ones((tk, LANES), BF16)
    for g in range(N_KV_HEADS_A):
        for r in range(rep):
            lhs_ref[g, r * tq:(r + 1) * tq, :] = q_ref[:, (g * rep + r) * HEAD_DIM:(g * rep + r + 1) * HEAD_DIM]
    m_ref[...] = jnp.full(m_ref.shape, NEG_BIG, F32)
    acc_ref[...] = jnp.zeros(acc_ref.shape, F32)

    def kv_block(kb, c):
        k0 = pl.multiple_of(kb * tk, tk)
        for g in range(N_KV_HEADS_A):
            kblk = k_ref[pl.ds(k0, tk), g * HEAD_DIM:(g + 1) * HEAD_DIM]
            vext = jnp.concatenate([v_ref[pl.ds(k0, tk), g * HEAD_DIM:(g + 1) * HEAD_DIM], ones], axis=1)
            _softmax_step(lhs_ref.at[g], m_ref.at[g], acc_ref.at[g], kblk, vext,
                          lambda r: bias_ref[kb, (r * rc) % tq:(r * rc) % tq + rc, :],
                          rows=rep * tq, rc=rc, tk=tk)
        return c

    lax.fori_loop(0, nkb, kv_block, 0)
    for g in range(N_KV_HEADS_A):
        acc = acc_ref[g]
        o = acc[:, :HEAD_DIM] / acc[:, HEAD_DIM:]
        for r in range(rep):
            h0 = (g * rep + r) * HEAD_DIM
            o_ref[:, h0:h0 + HEAD_DIM] = o[r * tq:(r + 1) * tq].astype(o_ref.dtype)


def _dsa_attention(q, qi, kiwi, k_all, v_all, kid_all, *, batch, lq, tq, tk, l_valid, q_pos0, topk):
    lk_pad = k_all.shape[1]
    nq = lq // tq
    nkb_max = lk_pad // tk
    a_q = N_HEADS_A * HEAD_DIM
    a_kv = N_KV_HEADS_A * HEAD_DIM
    rep = N_HEADS_A // N_KV_HEADS_A
    rc = min(ROW_CHUNK, tq)
    ones_tri = np.triu(np.ones((tk, tk), np.float32))
    if tq % LANES == 0:
        tri = jnp.asarray(ones_tri.T, BF16)
        key_scratch = pltpu.VMEM((nkb_max + 1, tq // LANES, tk, LANES), jnp.int32)
        w_scratch = pltpu.VMEM((tq // LANES, N_IDX_HEADS, LANES), F32)
    else:
        tri = jnp.asarray(ones_tri, BF16)
        key_scratch = pltpu.VMEM((nkb_max + 1, tq, tk), jnp.int32)
        w_scratch = pltpu.VMEM((tq // rc, N_IDX_HEADS, rc, LANES), F32)
    return pl.pallas_call(
        functools.partial(_dsa_body, tq=tq, tk=tk, nq=nq, l_valid=l_valid, q_pos0=q_pos0, topk=topk),
        out_shape=jax.ShapeDtypeStruct((batch * lq, a_q), BF16),
        grid=(batch, nq),
        in_specs=[
            pl.BlockSpec((tq, a_q), lambda b, t: (b * nq + t, 0)),
            pl.BlockSpec((tq, N_IDX_HEADS * IDX_DIM), lambda b, t: (b * nq + t, 0)),
            pl.BlockSpec((tq, LANES), lambda b, t: (b * nq + t, 0)),
            pl.BlockSpec((None, lk_pad, a_kv), lambda b, t: (b, 0, 0)),
            pl.BlockSpec((None, lk_pad, a_kv), lambda b, t: (b, 0, 0)),
            pl.BlockSpec((None, lk_pad, LANES), lambda b, t: (b, 0, 0)),
            pl.BlockSpec((tk, tk), lambda b, t: (0, 0)),
        ],
        out_specs=pl.BlockSpec((tq, a_q), lambda b, t: (b * nq + t, 0)),
        scratch_shapes=[key_scratch, pltpu.VMEM((nkb_max, tq, tk), F32),
                        pltpu.VMEM((N_KV_HEADS_A, rep * tq, HEAD_DIM), BF16),
                        pltpu.VMEM((N_KV_HEADS_A, rep * tq, LANES), F32),
                        pltpu.VMEM((N_KV_HEADS_A, rep * tq, 2 * HEAD_DIM), F32),
                        pltpu.VMEM((tq // rc, N_IDX_HEADS // 2, 2 * rc, LANES), BF16),
                        w_scratch],
        compiler_params=_cparams(("parallel", "arbitrary")),
        name="dsa_attention",
    )(q, qi, kiwi, k_all, v_all, kid_all, tri)


def _diff_body(q_ref, k_ref, v_ref, lq1_ref, lk1_ref, lq2_ref, lk2_ref, g_ref, o_ref, lhs_ref, m_ref, acc_ref, *,
               tq, tk, nq, l_valid, q_pos0, lam_init, rc, heads):
    lam = (jnp.exp(jnp.sum(lq1_ref[...] * lk1_ref[...], axis=1, keepdims=True))
           - jnp.exp(jnp.sum(lq2_ref[...] * lk2_ref[...], axis=1, keepdims=True)) + lam_init)
    lane = lax.broadcasted_iota(jnp.int32, (tq, LANES), 1)
    ones = jnp.ones((tk, LANES), BF16)
    hd = 2 * DIFF_DIM

    def tile(t, carry):
        q0 = pl.multiple_of(t * tq, tq)
        q_first = q_pos0 + t * tq
        for h in range(heads):
            qc = q_ref[pl.ds(q0, tq), h * hd:(h + 1) * hd]
            lhs_ref[h, :tq, :] = jnp.where(lane < DIFF_DIM, qc, jnp.zeros_like(qc))
            lhs_ref[h, tq:, :] = jnp.where(lane >= DIFF_DIM, qc, jnp.zeros_like(qc))
        m_ref[...] = jnp.full(m_ref.shape, NEG_BIG, F32)
        acc_ref[...] = jnp.zeros(acc_ref.shape, F32)
        nfull = jnp.minimum(((q_first // CHUNK + 1) * CHUNK) // tk, l_valid // tk)
        nkb = (jnp.minimum(l_valid, ((q_first + tq - 1) // CHUNK + 1) * CHUNK) + tk - 1) // tk

        def step(kb, c, masked):
            k0 = pl.multiple_of(kb * tk, tk)

            def bias_fn(r):
                row0 = q_first + (r * rc) % tq
                q_chunk = lax.shift_right_arithmetic(row0 + lax.broadcasted_iota(jnp.int32, (rc, tk), 0), CHUNK_SHIFT)
                kpos = k0 + lax.broadcasted_iota(jnp.int32, (rc, tk), 1)
                adm = (lax.shift_right_arithmetic(kpos, CHUNK_SHIFT) <= q_chunk) & (kpos < l_valid)
                return jnp.where(adm, 0.0, NEG_BIG)

            for h in range(heads):
                kblk = k_ref[pl.ds(k0, tk), h * hd:(h + 1) * hd]
                vext = jnp.concatenate([v_ref[pl.ds(k0, tk), h * hd:(h + 1) * hd], ones], axis=1)
                _softmax_step(lhs_ref.at[h], m_ref.at[h], acc_ref.at[h], kblk, vext, bias_fn if masked else None,
                              rows=2 * tq, rc=rc, tk=tk)
            return c

        lax.fori_loop(0, nfull, functools.partial(step, masked=False), 0)
        lax.fori_loop(nfull, nkb, functools.partial(step, masked=True), 0)
        for h in range(heads):
            acc = acc_ref[h]
            o = acc[:, :hd] / acc[:, hd:]
            o = o[:tq] - lam * o[tq:]
            o = o * lax.rsqrt(jnp.mean(o * o, axis=-1, keepdims=True) + LN_EPS) * g_ref[...] * (1.0 - lam_init)
            o_ref[pl.ds(q0, tq), h * hd:(h + 1) * hd] = o.astype(o_ref.dtype)
        return carry

    lax.fori_loop(0, nq, tile, 0)


def _diff_attention(q, k_all, v_all, lq1, lk1, lq2, lk2, g, j, *, batch, lq, tq, tk, l_valid, q_pos0, lam_init):
    lk_pad = k_all.shape[1]
    nq = lq // tq
    hd = 2 * DIFF_DIM
    heads = DIFF_HEADS_PER_STEP
    rc = min(ROW_CHUNK, 2 * tq)
    single = pl.Buffered(1)
    lam_spec = pl.BlockSpec((None, 1, DIFF_DIM), lambda b, h: (j, 0, 0))
    return pl.pallas_call(
        functools.partial(_diff_body, tq=tq, tk=tk, nq=nq, l_valid=l_valid, q_pos0=q_pos0, lam_init=lam_init,
                          rc=rc, heads=heads),
        out_shape=jax.ShapeDtypeStruct((batch * lq, N_HEADS_B * hd), BF16),
        grid=(batch, N_HEADS_B // heads),
        in_specs=[
            pl.BlockSpec((lq, heads * hd), lambda b, h: (b, h), pipeline_mode=single),
            pl.BlockSpec((None, lk_pad, heads * hd), lambda b, h: (b, 0, h), pipeline_mode=single),
            pl.BlockSpec((None, lk_pad, heads * hd), lambda b, h: (b, 0, h), pipeline_mode=single),
            lam_spec, lam_spec, lam_spec, lam_spec,
            pl.BlockSpec((None, 1, hd), lambda b, h: (j, 0, 0)),
        ],
        out_specs=pl.BlockSpec((lq, heads * hd), lambda b, h: (b, h), pipeline_mode=single),
        scratch_shapes=[pltpu.VMEM((heads, 2 * tq, hd), BF16), pltpu.VMEM((heads, 2 * tq, LANES), F32),
                        pltpu.VMEM((heads, 2 * tq, 2 * hd), F32)],
        compiler_params=_cparams(("parallel", "parallel")),
        name="diff_attention",
    )(q, k_all, v_all, lq1, lk1, lq2, lk2, g)


def _diff_past_body(q_ref, pkt_ref, pv_ref, kn_ref, vn_ref, lq1_ref, lk1_ref, lq2_ref, lk2_ref, g_ref, o_ref,
                    lhs_ref, m_ref, acc_ref, *, tq, tk, l_past, l_new, q_pos0, lam_init, rc, heads):
    lam = (jnp.exp(jnp.sum(lq1_ref[...] * lk1_ref[...], axis=1, keepdims=True))
           - jnp.exp(jnp.sum(lq2_ref[...] * lk2_ref[...], axis=1, keepdims=True)) + lam_init)
    lane = lax.broadcasted_iota(jnp.int32, (tq, LANES), 1)
    ones = jnp.ones((tk, LANES), BF16)
    hd = 2 * DIFF_DIM
    for h in range(heads):
        qc = q_ref[:, h * hd:(h + 1) * hd]
        lhs_ref[h, :tq, :] = jnp.where(lane < DIFF_DIM, qc, jnp.zeros_like(qc))
        lhs_ref[h, tq:, :] = jnp.where(lane >= DIFF_DIM, qc, jnp.zeros_like(qc))
    m_ref[...] = jnp.full(m_ref.shape, NEG_BIG, F32)
    acc_ref[...] = jnp.zeros(acc_ref.shape, F32)

    for kb in range(l_past // tk):
        for h in range(heads):
            kt = pkt_ref[h, :, kb * tk:(kb + 1) * tk].astype(BF16)
            vext = jnp.concatenate([pv_ref[kb * tk:(kb + 1) * tk, h, :].astype(BF16), ones], axis=1)
            _softmax_step(lhs_ref.at[h], m_ref.at[h], acc_ref.at[h], kt, vext, None,
                          rows=2 * tq, rc=rc, tk=tk, k_transposed=True)

    def bias_fn(r):
        row0 = q_pos0 + (r * rc) % tq
        q_chunk = lax.shift_right_arithmetic(row0 + lax.broadcasted_iota(jnp.int32, (rc, tk), 0), CHUNK_SHIFT)
        kpos = l_past + lax.broadcasted_iota(jnp.int32, (rc, tk), 1)
        adm = (lax.shift_right_arithmetic(kpos, CHUNK_SHIFT) <= q_chunk) & (kpos < l_past + l_new)
        return jnp.where(adm, 0.0, NEG_BIG)

    for h in range(heads):
        vext = jnp.concatenate([vn_ref[:, h * hd:(h + 1) * hd], ones], axis=1)
        _softmax_step(lhs_ref.at[h], m_ref.at[h], acc_ref.at[h], kn_ref[:, h * hd:(h + 1) * hd], vext, bias_fn,
                      rows=2 * tq, rc=rc, tk=tk)
    for h in range(heads):
        acc = acc_ref[h]
        o = acc[:, :hd] / acc[:, hd:]
        o = o[:tq] - lam * o[tq:]
        o = o * lax.rsqrt(jnp.mean(o * o, axis=-1, keepdims=True) + LN_EPS) * g_ref[...] * (1.0 - lam_init)
        o_ref[:, h * hd:(h + 1) * hd] = o.astype(o_ref.dtype)


def _diff_attention_past(q, pk_t, pv, k_new, v_new, lq1, lk1, lq2, lk2, g, j, *, batch, lq, tk, l_new, q_pos0, lam_init):
    l_past = pk_t.shape[-1]
    hd = 2 * DIFF_DIM
    heads = DIFF_PAST_HEADS_PER_STEP
    rc = min(ROW_CHUNK, 2 * lq)
    lam_spec = pl.BlockSpec((None, 1, DIFF_DIM), lambda b, h: (j, 0, 0))
    return pl.pallas_call(
        functools.partial(_diff_past_body, tq=lq, tk=tk, l_past=l_past, l_new=l_new, q_pos0=q_pos0,
                          lam_init=lam_init, rc=rc, heads=heads),
        out_shape=jax.ShapeDtypeStruct((batch * lq, N_HEADS_B * hd), BF16),
        grid=(batch, N_HEADS_B // heads),
        in_specs=[
            pl.BlockSpec((lq, heads * hd), lambda b, h: (b, h)),
            pl.BlockSpec((None, heads, hd, l_past), lambda b, h: (b, h, 0, 0)),
            pl.BlockSpec((None, l_past, heads, hd), lambda b, h: (b, 0, h, 0)),
            pl.BlockSpec((None, tk, heads * hd), lambda b, h: (b, 0, h)),
            pl.BlockSpec((None, tk, heads * hd), lambda b, h: (b, 0, h)),
            lam_spec, lam_spec, lam_spec, lam_spec,
            pl.BlockSpec((None, 1, hd), lambda b, h: (j, 0, 0)),
        ],
        out_specs=pl.BlockSpec((lq, heads * hd), lambda b, h: (b, h)),
        scratch_shapes=[pltpu.VMEM((heads, 2 * lq, hd), BF16), pltpu.VMEM((heads, 2 * lq, LANES), F32),
                        pltpu.VMEM((heads, 2 * lq, 2 * hd), F32)],
        compiler_params=_cparams(("parallel", "parallel")),
        name="diff_attention_past",
    )(q, pk_t, pv, k_new, v_new, lq1, lk1, lq2, lk2, g)


def _pad_keys(x, lk_pad):
    pad = lk_pad - x.shape[1]
    return x if pad == 0 else jnp.pad(x, ((0, 0), (0, pad), (0, 0)))


def _trunk(x, p, pos0, pasts, w, *, tm):
    batch, seq, d = x.shape
    n = batch * seq
    depth = w["ln_g"].shape[0]
    alpha = float((2 * depth) ** 0.25)
    a_q = N_HEADS_A * HEAD_DIM
    a_kv = N_KV_HEADS_A * HEAD_DIM
    a_qi = N_IDX_HEADS * IDX_DIM
    b_qk = N_HEADS_B * 2 * DIFF_DIM

    period = max(1, seq // tm)
    table_rows = max(seq, tm)
    positions = (pos0 + np.arange(seq)).astype(np.int64)
    positions = np.tile(positions, table_rows // seq)
    tab128 = _rope_tables(positions, HEAD_DIM, "head")
    tab64 = _rope_tables(positions, IDX_DIM, "head")
    tabkw = _rope_tables(positions, IDX_DIM, "kiwi")

    xf = x.reshape(n, d)
    pf = p.reshape(depth, n, p.shape[-1])
    tf = 512 if w["ffn_w_gate"].shape[-1] % 512 == 0 else 128
    rows = []
    for i in range(depth):
        j = i // 2
        xf = _ffn(xf, w["ffn_w_gate"], w["ffn_w_up"], w["ffn_w_down"], w["ln_g"], w["ln_b"], i, 0, 0,
                  alpha=alpha, tm=tm, tf=tf)
        past = pasts[i]
        lk_new = seq if past is None else past[0].shape[1] + seq
        if i % 2 == 0:
            wa = w["a_w_in"]
            c_k, c_v, c_qi, c_kw = a_q, a_q + a_kv, a_q + 2 * a_kv, a_q + 2 * a_kv + a_qi
            kw_kinds = (F32, KEY_T, KEY_TWICE) if seq % tm == 0 else (F32,)
            outs = _inproj(xf, wa, [tab128, tab64, tabkw],
                           [(0, c_k, 0, HEAD_DIM ** -0.5 * LOG2_E, (BF16,)),
                            (c_k, c_v, 0, 1.0, (ROW_INTERLEAVED, BF16)),
                            (c_v, c_qi, None, 1.0, (ROW_INTERLEAVED, BF16)),
                            (c_qi, c_kw, 1, 1.0, (BF16,)),
                            (c_kw, c_kw + LANES, 2, 1.0, kw_kinds)],
                           tm=tm, period_blocks=period, seq=seq, w_layer=j)
            q, k, kb, v, vb, qi, kiwi = outs[:7]
            k_all, v_all = kb.reshape(batch, seq, a_kv), vb.reshape(batch, seq, a_kv)
            if seq % tm == 0:
                ki_rows = jnp.transpose(outs[7], (0, 2, 1))
                kid_all = outs[8].reshape(batch, seq, LANES)
            else:
                ki_rows = kiwi[:, :IDX_DIM].reshape(batch, seq, IDX_DIM)
                kid_all = jnp.concatenate([ki_rows, ki_rows], axis=-1).astype(BF16)
            rows.append((k.reshape(batch, seq, N_KV_HEADS_A, HEAD_DIM), v.reshape(batch, seq, N_KV_HEADS_A, HEAD_DIM),
                         ki_rows))
            if past is not None:
                pk, pv, pki = past
                k_all = jnp.concatenate([pk.reshape(batch, -1, a_kv).astype(BF16), k_all], axis=1)
                v_all = jnp.concatenate([pv.reshape(batch, -1, a_kv).astype(BF16), v_all], axis=1)
                kid_all = jnp.concatenate([jnp.concatenate([pki, pki], axis=-1).astype(BF16), kid_all], axis=1)
            tq = min(DSA_Q_TILE, seq)
            tk = KEY_BLOCK
            lk_pad = -(-lk_new // tk) * tk
            topk = min(TOPK_MAX, lk_new // 4)
            o = _dsa_attention(q, qi, kiwi, _pad_keys(k_all, lk_pad), _pad_keys(v_all, lk_pad), _pad_keys(kid_all, lk_pad),
                               batch=batch, lq=seq, tq=tq, tk=tk, l_valid=lk_new, q_pos0=pos0, topk=topk)
            xf = _oproj(xf, o, w["a_w_out"], w["ln_g"], w["ln_b"], i, j, alpha=alpha, tm=tm)
        else:
            wb = w["b_w_in"]
            lam_init = 0.8 - 0.6 * math.exp(-0.3 * i)
            k_kinds = (HEAD_MAJOR_T, BF16) if seq % tm == 0 else (F32, BF16)
            q, k, kb = _inproj(xf, wb[j, :, :2 * b_qk], [tab64],
                               [(0, b_qk, 0, DIFF_DIM ** -0.5 * LOG2_E, (BF16,)), (b_qk, 2 * b_qk, 0, 1.0, k_kinds)],
                               tm=tm, period_blocks=period, seq=seq)
            k_rows = (jnp.transpose(k, (0, 4, 1, 2, 3)) if seq % tm == 0
                      else k.reshape(batch, seq, N_HEADS_B, 2, DIFF_DIM))
            v, vb = _inproj(xf, wb[j, :, 2 * b_qk:], [], [(0, b_qk, None, 1.0, (F32, BF16))],
                            tm=tm, period_blocks=period)
            rows.append((k_rows, v.reshape(batch, seq, N_HEADS_B, 2 * DIFF_DIM)))
            k_all, v_all = kb.reshape(batch, seq, b_qk), vb.reshape(batch, seq, b_qk)
            tq = min(256, seq)
            tk = KEY_BLOCK
            lam_args = (w["b_lambda_q1"], w["b_lambda_k1"], w["b_lambda_q2"], w["b_lambda_k2"], w["b_subln"], j)
            if past is not None and past[0].shape[1] % tk == 0 and seq <= tk and pos0 >= past[0].shape[1]:
                pk, pv = past
                l_past = pk.shape[1]
                pk_t = jnp.transpose(pk, (0, 2, 3, 4, 1)).reshape(batch, N_HEADS_B, 2 * DIFF_DIM, l_past)
                o = _diff_attention_past(q, pk_t, pv, _pad_keys(k_all, tk),
                                         _pad_keys(v_all, tk), *lam_args, batch=batch, lq=seq, tk=tk, l_new=seq,
                                         q_pos0=pos0, lam_init=lam_init)
            else:
                if past is not None:
                    pk, pv = past
                    k_all = jnp.concatenate([pk.reshape(batch, -1, b_qk).astype(BF16), k_all], axis=1)
                    v_all = jnp.concatenate([pv.reshape(batch, -1, b_qk).astype(BF16), v_all], axis=1)
                lk_pad = -(-lk_new // tk) * tk
                o = _diff_attention(q, _pad_keys(k_all, lk_pad), _pad_keys(v_all, lk_pad), *lam_args,
                                    batch=batch, lq=seq, tq=tq, tk=tk, l_valid=lk_new, q_pos0=pos0, lam_init=lam_init)
            xf = _oproj(xf, o, w["b_w_out"], w["ln_g"], w["ln_b"], i, j, alpha=alpha, tm=tm)
        xf = _ffn(xf, w["ffn_w_gate"], w["ffn_w_up"], w["ffn_w_down"], w["ln_g"], w["ln_b"], i, 1, 2,
                  alpha=alpha, tm=tm, tf=tf)
        xf = _ple(xf, pf, w["ple_w_gate"], w["ple_b_gate"], w["ple_w_proj"], w["ln_g"], w["ln_b"], i,
                  alpha=alpha, tm=tm)
    return xf.reshape(batch, seq, d), rows


def kernel(x_prompt, x_sample, cache_l0_k, cache_l0_v, cache_l0_kidx, cache_l1_k, cache_l1_v, cache_l2_k, cache_l2_v, cache_l2_kidx, cache_l3_k, cache_l3_v, p_prompt, p_sample, ln_g, ln_b, ffn_w_gate, ffn_w_up, ffn_w_down, ple_w_gate, ple_b_gate, ple_w_proj, a_w_in, a_w_out, b_w_in, b_w_out, b_lambda_q1, b_lambda_k1, b_lambda_q2, b_lambda_k2, b_subln):
    depth, _, d = ln_g.shape
    a_cols = a_w_in.shape[-1]
    a_pad = -(-a_cols // LANES) * LANES - a_cols
    w = {
        "ln_g": ln_g.reshape(depth, 4, 1, d),
        "ln_b": ln_b.reshape(depth, 4, 1, d),
        "ffn_w_gate": ffn_w_gate.astype(BF16),
        "ffn_w_up": ffn_w_up.astype(BF16),
        "ffn_w_down": (0.5 * ffn_w_down).astype(BF16),
        "ple_w_gate": ple_w_gate.astype(BF16),
        "ple_b_gate": ple_b_gate.reshape(depth, 1, d),
        "ple_w_proj": ple_w_proj.astype(BF16),
        "a_w_in": jnp.pad(a_w_in, ((0, 0), (0, 0), (0, a_pad))).astype(BF16),
        "a_w_out": a_w_out.astype(BF16),
        "b_w_in": b_w_in.astype(BF16),
        "b_w_out": b_w_out.astype(BF16),
        "b_lambda_q1": b_lambda_q1.reshape(-1, 1, DIFF_DIM),
        "b_lambda_k1": b_lambda_k1.reshape(-1, 1, DIFF_DIM),
        "b_lambda_q2": b_lambda_q2.reshape(-1, 1, DIFF_DIM),
        "b_lambda_k2": b_lambda_k2.reshape(-1, 1, DIFF_DIM),
        "b_subln": b_subln.reshape(-1, 1, 2 * DIFF_DIM),
    }
    past_len = cache_l0_k.shape[1]
    pasts_s = [(cache_l0_k, cache_l0_v, cache_l0_kidx), (cache_l1_k, cache_l1_v),
               (cache_l2_k, cache_l2_v, cache_l2_kidx), (cache_l3_k, cache_l3_v)]
    n_p = x_prompt.shape[0] * x_prompt.shape[1]
    n_s = x_sample.shape[0] * x_sample.shape[1]
    y_p, rows_p = _trunk(x_prompt, p_prompt, 0, [None] * depth, w, tm=min(512, n_p))
    y_s, rows_s = _trunk(x_sample, p_sample, past_len, pasts_s, w, tm=min(256, n_s))
    out = [y_p, y_s]
    for rp, rs in zip(rows_p, rows_s):
        out.extend(rp)
        out.extend(rs)
    return tuple(out)
```
